```python
import math
import jax, jax.numpy as jnp
from jax import lax
import numpy as np

D_MODEL = 1024
BATCH = 32
SEQ = 2048
DEPTH = 1

CHUNK = 64
N_META = 16
Q_BLOCK = 128
HEAD_DIM = 64
H_SB = 8
H_DIFF = 4
D_SB = H_SB * HEAD_DIM
D_DIFF = H_DIFF * 2 * HEAD_DIM
D_MIX = D_SB + D_DIFF
D_IN = 3 * D_SB + 3 * D_DIFF
N_BUCKETS = 32
MAX_DISTANCE = 128
N_EXPERTS = 32
TOP_K = 4
D_FF = D_MODEL
SWIGLU_ALPHA = 1.702
SWIGLU_LIMIT = 7.0
EXPERT_BLOCK = 128
NORM_EPS = 1e-6
SUBLN_EPS = 1e-5
NEG_BIG = -1e30

kernel_name = "hymba_stickbreak_diffattn_gptoss_moe"


def rmsnorm(x, g, eps=NORM_EPS):
    xf = x.astype(jnp.float32)
    y = xf * lax.rsqrt(jnp.mean(xf * xf, axis=-1, keepdims=True) + eps)
    return (y * g.astype(jnp.float32)).astype(x.dtype)


def chunk_id(pos):
    return jnp.where(pos < N_META, 0, 1 + (pos - N_META) // CHUNK)


def chunk_end(p):
    return N_META if p < N_META else N_META + CHUNK * (1 + (p - N_META) // CHUNK)


def t5_bucket(rel):
    half = N_BUCKETS // 2
    max_exact = half // 2
    ret = jnp.where(rel > 0, half, 0)
    n = jnp.abs(rel)
    nf = jnp.maximum(n, 1).astype(jnp.float32)
    large = max_exact + (jnp.log(nf / max_exact) / math.log(MAX_DISTANCE / max_exact)
                         * (half - max_exact)).astype(jnp.int32)
    large = jnp.minimum(large, half - 1)
    return ret + jnp.where(n < max_exact, n, large)


def stick_breaking_attention(q, k, v):
    Lp = q.shape[1]
    scale = HEAD_DIM ** -0.5
    outs = []
    for start in range(0, Lp, Q_BLOCK):
        stop = start + Q_BLOCK
        z = jnp.einsum('bqhd,bkhd->bhqk', q[:, start:stop], k[:, :stop]).astype(jnp.float32) * scale
        qpos = start + jnp.arange(Q_BLOCK)
        kpos = jnp.arange(stop)
        earlier = kpos[None, :] < qpos[:, None]
        log_stay = jnp.where(earlier, jax.nn.log_sigmoid(-z), 0.0)
        between = lax.cumsum(log_stay, axis=3, reverse=True) - log_stay
        w = jnp.where(earlier, jnp.exp(jax.nn.log_sigmoid(z) + between), 0.0)
        outs.append(jnp.einsum('bhqk,bkhd->bqhd', w.astype(v.dtype), v[:, :stop]))
    return jnp.concatenate(outs, axis=1)


def differential_attention(q, k, v, rel_bias, lam, subln_gain, lambda_init):
    B_, Lp = q.shape[0], q.shape[1]
    scale = HEAD_DIM ** -0.5
    outs = []
    for start in range(0, Lp, Q_BLOCK):
        stop = start + Q_BLOCK
        kend = min(Lp, chunk_end(stop - 1))
        z = jnp.einsum('bqhd,bkhd->bhqk', q[:, start:stop], k[:, :kend]).astype(jnp.float32) * scale
        z = z.reshape(B_, H_DIFF, 2, Q_BLOCK, kend)
        qpos = start + jnp.arange(Q_BLOCK)
        kpos = jnp.arange(kend)
        bias = jnp.transpose(rel_bias[t5_bucket(kpos[None, :] - qpos[:, None])], (2, 0, 1))
        visible = chunk_id(kpos)[None, :] <= chunk_id(qpos)[:, None]
        z = jnp.where(visible, z + bias.astype(jnp.float32)[None, :, None], NEG_BIG)
        p = jax.nn.softmax(z, axis=-1)
        a = p[:, :, 0] - lam * p[:, :, 1]
        outs.append(jnp.einsum('bhqk,bkhe->bqhe', a.astype(v.dtype), v[:, :kend]))
    o = jnp.concatenate(outs, axis=1)
    return rmsnorm(o, subln_gain, SUBLN_EPS) * (1.0 - lambda_init)


def clamped_swiglu(hu):
    gate, lin = hu[..., :D_FF], hu[..., D_FF:]
    gate = jnp.minimum(gate, SWIGLU_LIMIT)
    lin = jnp.clip(lin, -SWIGLU_LIMIT, SWIGLU_LIMIT)
    return gate * jax.nn.sigmoid(SWIGLU_ALPHA * gate) * (lin + 1.0)


def moe(h, w_router, b_router, w1, b1, w2, b2):
    D_ = h.shape[-1]
    xf = h.reshape(-1, D_)
    T = xf.shape[0]
    logits = (xf @ w_router).astype(jnp.float32) + b_router.astype(jnp.float32)
    top_val, top_idx = lax.top_k(logits, TOP_K)
    gates = jax.nn.softmax(top_val, axis=-1)
    A = T * TOP_K
    flat_e = top_idx.reshape(-1)
    flat_tok = jnp.repeat(jnp.arange(T, dtype=jnp.int32), TOP_K)
    flat_g = gates.reshape(-1)
    order = jnp.argsort(flat_e)
    se, stok, sg = flat_e[order], flat_tok[order], flat_g[order]
    counts = jnp.bincount(flat_e, length=N_EXPERTS)
    starts = jnp.cumsum(counts) - counts
    padded = (counts + EXPERT_BLOCK - 1) // EXPERT_BLOCK * EXPERT_BLOCK
    pad_end = jnp.cumsum(padded)
    pad_start = pad_end - padded
    pos = pad_start[se] + jnp.arange(A) - starts[se]
    n_blocks = -(-A // EXPERT_BLOCK) + N_EXPERTS
    buf_tok = jnp.zeros((n_blocks * EXPERT_BLOCK,), jnp.int32).at[pos].set(stok)
    buf_g = jnp.zeros((n_blocks * EXPERT_BLOCK,), jnp.float32).at[pos].set(sg)
    blk_e = jnp.minimum(jnp.searchsorted(pad_end, jnp.arange(n_blocks) * EXPERT_BLOCK, side='right'),
                        N_EXPERTS - 1)

    def expert_block(args):
        tok, e = args
        xb = xf[tok]
        act = clamped_swiglu(xb @ w1[e] + b1[e])
        return act @ w2[e] + b2[e]

    y = lax.map(expert_block, (buf_tok.reshape(n_blocks, EXPERT_BLOCK), blk_e))
    y = y.reshape(-1, D_).astype(jnp.float32) * buf_g[:, None]
    out = jax.ops.segment_sum(y, buf_tok, num_segments=T)
    return out.astype(h.dtype).reshape(h.shape)


def setup_inputs(seed: int = 0) -> dict:
    key = jax.random.key(seed)
    ks = jax.random.split(key, 20)
    f32 = jnp.float32
    nrm = lambda k, shape, s: jax.random.normal(k, shape, f32) * s
    return {
        "x": nrm(ks[0], (BATCH, SEQ, D_MODEL), 1.0),
        "meta_tokens": nrm(ks[1], (N_META, D_MODEL), 1.0),
        "rel_bias": nrm(ks[2], (N_BUCKETS, H_DIFF), 0.3),
        "attn_norm": 1.0 + nrm(ks[3], (DEPTH, D_MODEL), 0.02),
        "w_in": nrm(ks[4], (DEPTH, D_MODEL, D_IN), D_MODEL ** -0.5),
        "w_out": nrm(ks[5], (DEPTH, D_MIX, D_MODEL), D_MIX ** -0.5),
        "lambda_q1": nrm(ks[6], (DEPTH, HEAD_DIM), 0.1),
        "lambda_k1": nrm(ks[7], (DEPTH, HEAD_DIM), 0.1),
        "lambda_q2": nrm(ks[8], (DEPTH, HEAD_DIM), 0.1),
        "lambda_k2": nrm(ks[9], (DEPTH, HEAD_DIM), 0.1),
        "subln_gain": 1.0 + nrm(ks[10], (DEPTH, 2 * HEAD_DIM), 0.02),
        "ffn_norm": 1.0 + nrm(ks[11], (DEPTH, D_MODEL), 0.02),
        "w_router": nrm(ks[12], (DEPTH, D_MODEL, N_EXPERTS), D_MODEL ** -0.5),
        "b_router": nrm(ks[13], (DEPTH, N_EXPERTS), 0.01),
        "w1": nrm(ks[14], (DEPTH, N_EXPERTS, D_MODEL, 2 * D_FF), D_MODEL ** -0.5),
        "b1": nrm(ks[15], (DEPTH, N_EXPERTS, 2 * D_FF), 0.01),
        "w2": nrm(ks[16], (DEPTH, N_EXPERTS, D_FF, D_MODEL), D_FF ** -0.5),
        "b2": nrm(ks[17], (DEPTH, N_EXPERTS, D_MODEL), 0.01),
        "final_norm": 1.0 + nrm(ks[18], (D_MODEL,), 0.02),
    }


def reference(x, meta_tokens, rel_bias, attn_norm, w_in, w_out, lambda_q1, lambda_k1,
              lambda_q2, lambda_k2, subln_gain, ffn_norm, w_router, b_router, w1, b1, w2, b2,
              final_norm):
    B_, S_ = x.shape[0], x.shape[1]
    meta = jnp.broadcast_to(meta_tokens[None].astype(x.dtype), (B_, N_META, D_MODEL))
    h = jnp.concatenate([meta, x], axis=1)
    L = S_ + N_META
    Lp = -(-L // Q_BLOCK) * Q_BLOCK
    split_at = [int(c) for c in np.cumsum([D_SB, D_SB, D_SB, D_DIFF, D_DIFF])]
    for layer in range(DEPTH):
        lambda_init = 0.8 - 0.6 * math.exp(-0.3 * layer)
        n = rmsnorm(h, attn_norm[layer])
        proj = jnp.pad(n @ w_in[layer], ((0, 0), (0, Lp - L), (0, 0)))
        sb_q, sb_k, sb_v, df_q, df_k, df_v = jnp.split(proj, split_at, axis=-1)
        sb_shape = (B_, Lp, H_SB, HEAD_DIM)
        o_sb = stick_breaking_attention(sb_q.reshape(sb_shape), sb_k.reshape(sb_shape),
                                        sb_v.reshape(sb_shape))
        lam = (jnp.exp(jnp.sum(lambda_q1[layer].astype(jnp.float32) * lambda_k1[layer].astype(jnp.float32)))
               - jnp.exp(jnp.sum(lambda_q2[layer].astype(jnp.float32) * lambda_k2[layer].astype(jnp.float32)))
               + lambda_init)
        o_df = differential_attention(df_q.reshape(B_, Lp, 2 * H_DIFF, HEAD_DIM),
                                      df_k.reshape(B_, Lp, 2 * H_DIFF, HEAD_DIM),
                                      df_v.reshape(B_, Lp, H_DIFF, 2 * HEAD_DIM),
                                      rel_bias, lam, subln_gain[layer], lambda_init)
        mix = jnp.concatenate([o_sb.reshape(B_, Lp, D_SB), o_df.reshape(B_, Lp, D_DIFF)], axis=-1)[:, :L]
        h = h + mix @ w_out[layer]
        h = h + moe(rmsnorm(h, ffn_norm[layer]), w_router[layer], b_router[layer],
                    w1[layer], b1[layer], w2[layer], b2[layer])
    h = rmsnorm(h, final_norm)
    return h[:, N_META:]
```

```python
import functools
import math

import jax
import jax.numpy as jnp
from jax import lax
from jax.experimental import pallas as pl
from jax.experimental.pallas import tpu as pltpu

D_MODEL = 1024
N_META = 16
CHUNK = 64
HEAD_DIM = 64
H_SB = 8
H_DIFF = 4
D_SB = H_SB * HEAD_DIM
D_DIFF = H_DIFF * 2 * HEAD_DIM
D_IN = 3 * D_SB + 3 * D_DIFF
N_BUCKETS = 32
N_EXPERTS = 32
TOP_K = 4
D_FF = D_MODEL
SWIGLU_ALPHA = 1.702
SWIGLU_LIMIT = 7.0
NORM_EPS = 1e-6
SUBLN_EPS = 1e-5
NEG_BIG = -1e30
LAMBDA_INIT = 0.8 - 0.6 * math.exp(-0.3 * 0)

LANES = 128
SUBLANES = 8
ROW_TILE = D_MODEL // LANES
VMEM_LIMIT = 56 * 1024 * 1024

TM_PROJ = 512
TQ = 256
TK = 256
KCH = TK // SUBLANES
TM_FFN = 256
TD = 128

F32 = jnp.float32
BF16 = jnp.bfloat16
I32 = jnp.int32

_NT = (((1,), (1,)), ((), ()))


def _cparams(sem, vmem=VMEM_LIMIT):
    return pltpu.CompilerParams(dimension_semantics=sem, vmem_limit_bytes=vmem)


def _inproj_kernel(x_ref, g_ref, cs_ref, w_ref, o_ref):
    x = x_ref[...]
    ms = jnp.mean(x * x, axis=-1, keepdims=True)
    xn = (x * lax.rsqrt(ms + NORM_EPS) * g_ref[...]).astype(BF16)
    y = jnp.dot(xn, w_ref[...], preferred_element_type=F32)
    o_ref[...] = (y * cs_ref[...]).astype(BF16)


def _in_proj(x2d, gain, colscale, w_bf16, tm):
    t = x2d.shape[0]
    return pl.pallas_call(
        _inproj_kernel,
        grid=(t // tm,),
        in_specs=[
            pl.BlockSpec((tm, D_MODEL), lambda i: (i, 0)),
            pl.BlockSpec((1, D_MODEL), lambda i: (0, 0)),
            pl.BlockSpec((1, D_IN), lambda i: (0, 0)),
            pl.BlockSpec((D_MODEL, D_IN), lambda i: (0, 0)),
        ],
        out_specs=pl.BlockSpec((tm, D_IN), lambda i: (i, 0)),
        out_shape=jax.ShapeDtypeStruct((t, D_IN), BF16),
        compiler_params=_cparams(("parallel",)),
        name="in_proj",
    )(x2d, gain, colscale, w_bf16)


def _bias_lookup(rel, rb_ref, h):
    n = jnp.abs(rel)
    n2 = n * n
    large = jnp.full(rel.shape, 8, I32)
    for k in range(1, 8):
        large = large + jnp.where(n2 >= (64 << k), 1, 0)
    bucket = jnp.where(rel > 0, N_BUCKETS // 2, 0) + jnp.where(n < 8, n, large)
    out = jnp.zeros(rel.shape, F32)
    for b in range(N_BUCKETS):
        out = jnp.where(bucket == b, rb_ref[b, h], out)
    return out


def _relbias_kernel(rb_ref, bt_ref, mb_ref):
    h = pl.program_id(0)
    krow = lax.broadcasted_iota(I32, (TK, TQ), 0)
    qcol = lax.broadcasted_iota(I32, (TK, TQ), 1)
    visible = (krow // CHUNK) <= (qcol // CHUNK)
    bt_ref[0, 0] = jnp.where(visible, _bias_lookup(krow - qcol, rb_ref, h), NEG_BIG)
    bt_ref[0, 1] = _bias_lookup(krow - qcol - TK, rb_ref, h)
    s = mb_ref.shape[2]
    mrow = lax.broadcasted_iota(I32, (N_META, s), 0)
    qpos = lax.broadcasted_iota(I32, (N_META, s), 1) + N_META
    mb_ref[0] = _bias_lookup(mrow - qpos, rb_ref, h)


def _rel_bias_tiles(rel_bias, s):
    return pl.pallas_call(
        _relbias_kernel,
        grid=(H_DIFF,),
        in_specs=[pl.BlockSpec(memory_space=pltpu.SMEM)],
        out_specs=[
            pl.BlockSpec((1, 2, TK, TQ), lambda h: (h, 0, 0, 0)),
            pl.BlockSpec((1, N_META, s), lambda h: (h, 0, 0)),
        ],
        out_shape=[
            jax.ShapeDtypeStruct((H_DIFF, 2, TK, TQ), F32),
            jax.ShapeDtypeStruct((H_DIFF, N_META, s), F32),
        ],
        compiler_params=_cparams(("arbitrary",)),
        name="rel_bias",
    )(rel_bias)


def _suffix_incl_sublanes(x):
    r = lax.broadcasted_iota(I32, x.shape, 0)
    for d in (1, 2, 4):
        shifted = pltpu.roll(x, SUBLANES - d, axis=0)
        x = x + jnp.where(r + d < SUBLANES, shifted, 0.0)
    return x


def _softplus(s):
    return jnp.maximum(s, 0.0) + jnp.log(1.0 + jnp.exp(-jnp.abs(s)))


def _head_half(qpair, half):
    lane = lax.broadcasted_iota(I32, qpair.shape, 1)
    keep = (lane >= HEAD_DIM * half) & (lane < HEAD_DIM * (half + 1))
    return jnp.where(keep, qpair, jnp.zeros_like(qpair))


def _sb_block(s, vt, carry, acc, valid):
    sp = _softplus(s)
    if valid is not None:
        sp = jnp.where(valid, sp, 0.0)
    run = jnp.zeros((SUBLANES, s.shape[1]), F32)
    parts = [None] * KCH
    for i in reversed(range(KCH)):
        run = run + sp[SUBLANES * i:SUBLANES * (i + 1), :]
        parts[i] = run
    incl = _suffix_incl_sublanes(run)
    base = (incl - run) + carry
    r_sum = jnp.concatenate([p + base for p in parts], axis=0)
    w = jnp.exp(s - r_sum)
    if valid is not None:
        w = jnp.where(valid, w, 0.0)
    acc = acc + jnp.dot(vt, w.astype(BF16), preferred_element_type=F32)
    return carry + incl[0:1, :], acc


def _sb_meta_block(s, vt, carry, acc):
    sp = _softplus(s)
    lo, hi = sp[0:SUBLANES, :], sp[SUBLANES:2 * SUBLANES, :]
    hi_incl = _suffix_incl_sublanes(hi)
    lo_incl = _suffix_incl_sublanes(lo) + hi_incl[0:1, :]
    r_sum = jnp.concatenate([lo_incl, hi_incl], axis=0) + carry
    w = jnp.exp(s - r_sum)
    return acc + jnp.dot(vt, w.astype(BF16), preferred_element_type=F32)


def _sb_kernel(q_ref, k_ref, vt_ref, mk_ref, mvt_ref, o_ref):
    qi = pl.program_id(2)
    qpair = q_ref[...]
    row = lax.broadcasted_iota(I32, (TK, TQ), 0)
    lane = lax.broadcasted_iota(I32, (TK, TQ), 1)
    key_off = (row % SUBLANES) * KCH + row // SUBLANES
    causal = key_off < lane
    outs = []
    for half in range(2):
        qz = _head_half(qpair, half)

        def scores(kb):
            start = pl.multiple_of(kb * TK, TK)
            return lax.dot_general(k_ref[pl.ds(start, TK), :], qz, _NT, preferred_element_type=F32)

        carry = jnp.zeros((1, TQ), F32)
        acc = jnp.zeros((HEAD_DIM, TQ), F32)
        carry, acc = _sb_block(scores(qi), vt_ref[half, qi], carry, acc, causal)

        def body(j, st):
            kb = qi - 1 - j
            return _sb_block(scores(kb), vt_ref[half, kb], st[0], st[1], None)

        carry, acc = lax.fori_loop(0, qi, body, (carry, acc))
        sm = lax.dot_general(mk_ref[...], qz, _NT, preferred_element_type=F32)
        acc = _sb_meta_block(sm, mvt_ref[half], carry, acc)
        outs.append(acc)
    o_ref[...] = jnp.concatenate(outs, axis=0).T.astype(BF16)


def _sb_attention(proj, k_perm, vt_perm, mproj, mvt, b, s):
    nq = s // TQ
    nkb = s // TK
    qcol0 = 0
    mkcol0 = D_SB // LANES
    return pl.pallas_call(
        _sb_kernel,
        grid=(b, H_SB // 2, nq),
        in_specs=[
            pl.BlockSpec((TQ, LANES), lambda bi, p, qi: (bi * nq + qi, qcol0 + p)),
            pl.BlockSpec((s, LANES), lambda bi, p, qi: (bi, p)),
            pl.BlockSpec((None, 2, nkb, HEAD_DIM, TK), lambda bi, p, qi: (bi, p, 0, 0, 0)),
            pl.BlockSpec((N_META, LANES), lambda bi, p, qi: (0, mkcol0 + p)),
            pl.BlockSpec((2, HEAD_DIM, N_META), lambda bi, p, qi: (p, 0, 0)),
        ],
        out_specs=pl.BlockSpec((TQ, LANES), lambda bi, p, qi: (bi * nq + qi, p)),
        out_shape=jax.ShapeDtypeStruct((b * s, D_SB), BF16),
        compiler_params=_cparams(("parallel", "parallel", "arbitrary")),
        name="sb_attn",
    )(proj, k_perm, vt_perm, mproj, mvt)


def _df_update(s, vt, st):
    m, l, acc = st
    m_new = jnp.maximum(m, jnp.max(s, axis=0, keepdims=True))
    alpha = jnp.exp(m - m_new)
    p = jnp.exp(s - m_new)
    l = alpha * l + jnp.sum(p, axis=0, keepdims=True)
    acc = alpha * acc + jnp.dot(vt, p.astype(BF16), preferred_element_type=F32)
    return m_new, l, acc


def _df_kernel(far_ref, q_ref, k_ref, vt_ref, mk_ref, mvt_ref, bt_ref, mb_ref,
               lq1_ref, lk1_ref, lq2_ref, lk2_ref, gain_ref, o_ref):
    h = pl.program_id(1)
    qi = pl.program_id(2)
    qpair = q_ref[...]
    qz = [_head_half(qpair, 0), _head_half(qpair, 1)]
    far = far_ref[h]

    def block(kb, bias, st):
        start = pl.multiple_of(kb * TK, TK)
        kblk = k_ref[pl.ds(start, TK), :]
        vt = vt_ref[kb]
        return tuple(
            _df_update(lax.dot_general(kblk, qz[i], _NT, preferred_element_type=F32) + bias, vt, st[i])
            for i in range(2))

    def init():
        return (jnp.full((1, TQ), -jnp.inf, F32), jnp.zeros((1, TQ), F32),
                jnp.zeros((2 * HEAD_DIM, TQ), F32))

    st = block(qi, bt_ref[0], (init(), init()))
    st = lax.cond(qi >= 1, lambda c: block(qi - 1, bt_ref[1], c), lambda c: c, st)
    st = lax.fori_loop(0, jnp.maximum(qi - 1, 0), lambda j, c: block(qi - 2 - j, far, c), st)
    mk = mk_ref[...]
    mvt = mvt_ref[...]
    mb = mb_ref[...]
    st = tuple(
        _df_update(lax.dot_general(mk, qz[i], _NT, preferred_element_type=F32) + mb, mvt, st[i])
        for i in range(2))

    lam = (jnp.exp(jnp.sum(lq1_ref[...] * lk1_ref[...], axis=-1, keepdims=True))
           - jnp.exp(jnp.sum(lq2_ref[...] * lk2_ref[...], axis=-1, keepdims=True))
           + LAMBDA_INIT)
    o = st[0][2] / st[0][1] - lam * (st[1][2] / st[1][1])
    ms = jnp.mean(o * o, axis=0, keepdims=True)
    y = o * lax.rsqrt(ms + SUBLN_EPS) * gain_ref[...] * (1.0 - LAMBDA_INIT)
    o_ref[...] = y.T.astype(BF16)


def _diff_attention(proj, vt, mproj, mvt, bias_tiles, meta_bias, far_bias,
                    lq1, lk1, lq2, lk2, gain_col, b, s):
    nq = s // TQ
    nkb = s // TK
    qcol0 = 3 * D_SB // LANES
    kcol0 = (3 * D_SB + D_DIFF) // LANES
    lam_spec = pl.BlockSpec((1, HEAD_DIM), lambda bi, h, qi: (0, 0))
    return pl.pallas_call(
        _df_kernel,
        grid=(b, H_DIFF, nq),
        in_specs=[
            pl.BlockSpec(memory_space=pltpu.SMEM),
            pl.BlockSpec((TQ, LANES), lambda bi, h, qi: (bi * nq + qi, qcol0 + h)),
            pl.BlockSpec((s, LANES), lambda bi, h, qi: (bi, kcol0 + h)),
            pl.BlockSpec((None, None, nkb, 2 * HEAD_DIM, TK), lambda bi, h, qi: (bi, h, 0, 0, 0)),
            pl.BlockSpec((N_META, LANES), lambda bi, h, qi: (0, kcol0 + h)),
            pl.BlockSpec((None, 2 * HEAD_DIM, N_META), lambda bi, h, qi: (h, 0, 0)),
            pl.BlockSpec((None, 2, TK, TQ), lambda bi, h, qi: (h, 0, 0, 0)),
            pl.BlockSpec((None, N_META, TQ), lambda bi, h, qi: (h, 0, qi)),
            lam_spec, lam_spec, lam_spec, lam_spec,
            pl.BlockSpec((2 * HEAD_DIM, 1), lambda bi, h, qi: (0, 0)),
        ],
        out_specs=pl.BlockSpec((TQ, LANES), lambda bi, h, qi: (bi * nq + qi, h)),
        out_shape=jax.ShapeDtypeStruct((b * s, D_DIFF), BF16),
        compiler_params=_cparams(("parallel", "parallel", "arbitrary")),
        name="diff_attn",
    )(far_bias, proj, proj, vt, mproj, mvt, bias_tiles, meta_bias, lq1, lk1, lq2, lk2, gain_col)


def _outrouter_kernel(x_ref, osb_ref, odf_ref, wo_ref, g_ref, wrt_ref, br_ref, tri_ref,
                      h1_ref, xn_ref, idx_ref, gate_ref, rank_ref, cnt_ref, carry_ref):
    @pl.when(pl.program_id(0) == 0)
    def _():
        carry_ref[...] = jnp.zeros_like(carry_ref)

    tm = x_ref.shape[0]
    mix = jnp.concatenate([osb_ref[...], odf_ref[...]], axis=1)
    h1 = x_ref[...] + jnp.dot(mix, wo_ref[...], preferred_element_type=F32)
    h1_ref[...] = h1
    ms = jnp.mean(h1 * h1, axis=-1, keepdims=True)
    xn = h1 * lax.rsqrt(ms + NORM_EPS) * g_ref[...]
    for c in range(ROW_TILE):
        xn_ref[pl.ds(c, tm, stride=ROW_TILE), :] = xn[:, LANES * c:LANES * (c + 1)]

    logits = lax.dot_general(wrt_ref[...], xn.astype(BF16), _NT, preferred_element_type=F32) + br_ref[...]
    e_iota = lax.broadcasted_iota(I32, logits.shape, 0)
    work = logits
    vals, idxs = [], []
    for _ in range(TOP_K):
        m = jnp.max(work, axis=0, keepdims=True)
        ik = jnp.min(jnp.where(work == m, e_iota, N_EXPERTS), axis=0, keepdims=True)
        vals.append(m)
        idxs.append(ik)
        work = jnp.where(e_iota == ik, -jnp.inf, work)
    exps = [jnp.exp(v - vals[0]) for v in vals]
    den = exps[0] + exps[1] + exps[2] + exps[3]
    onehot = jnp.zeros(logits.shape, F32)
    for ik in idxs:
        onehot = onehot + jnp.where(e_iota == ik, 1.0, 0.0)
    prefix = jnp.dot(onehot.astype(BF16), tri_ref[...], preferred_element_type=F32)
    pos = prefix + carry_ref[:, 0:1]
    for k in range(TOP_K):
        idx_ref[k:k + 1, :] = idxs[k]
        gate_ref[k:k + 1, :] = exps[k] / den
        rank_ref[k:k + 1, :] = jnp.sum(jnp.where(e_iota == idxs[k], pos, 0.0), axis=0,
                                       keepdims=True).astype(I32)
    carry_ref[...] = carry_ref[...] + jnp.sum(onehot, axis=1, keepdims=True)
    cnt_ref[...] = carry_ref[...]


def _out_router(x2d, o_sb, o_df, wo_bf16, gain, wr_t, br_col, tri, tm):
    t = x2d.shape[0]
    const = lambda i: (0, 0)
    return pl.pallas_call(
        _outrouter_kernel,
        grid=(t // tm,),
        in_specs=[
            pl.BlockSpec((tm, D_MODEL), lambda i: (i, 0)),
            pl.BlockSpec((tm, D_SB), lambda i: (i, 0)),
            pl.BlockSpec((tm, D_DIFF), lambda i: (i, 0)),
            pl.BlockSpec((D_SB + D_DIFF, D_MODEL), const),
            pl.BlockSpec((1, D_MODEL), const),
            pl.BlockSpec((N_EXPERTS, D_MODEL), const),
            pl.BlockSpec((N_EXPERTS, 1), const),
            pl.BlockSpec((tm, tm), const),
        ],
        out_specs=[
            pl.BlockSpec((tm, D_MODEL), lambda i: (i, 0)),
            pl.BlockSpec((tm * ROW_TILE, LANES), lambda i: (i, 0)),
            pl.BlockSpec((TOP_K, tm), lambda i: (0, i)),
            pl.BlockSpec((TOP_K, tm), lambda i: (0, i)),
            pl.BlockSpec((TOP_K, tm), lambda i: (0, i)),
            pl.BlockSpec((N_EXPERTS, LANES), const),
        ],
        out_shape=[
            jax.ShapeDtypeStruct((t, D_MODEL), F32),
            jax.ShapeDtypeStruct((t * ROW_TILE, LANES), F32),
            jax.ShapeDtypeStruct((TOP_K, t), I32),
            jax.ShapeDtypeStruct((TOP_K, t), F32),
            jax.ShapeDtypeStruct((TOP_K, t), I32),
            jax.ShapeDtypeStruct((N_EXPERTS, LANES), F32),
        ],
        scratch_shapes=[pltpu.VMEM((N_EXPERTS, LANES), F32)],
        compiler_params=_cparams(("arbitrary",)),
        name="out_router",
    )(x2d, o_sb, o_df, wo_bf16, gain, wr_t, br_col, tri)


def _dest_kernel(ps_ref, idx_ref, rank_ref, dest_ref):
    idx = idx_ref[...]
    off = jnp.zeros(idx.shape, I32)
    for e in range(N_EXPERTS):
        off = jnp.where(idx == e, ps_ref[e], off)
    dest_ref[...] = rank_ref[...] + off


def _route_dest(pad_start, idx, rank):
    t = idx.shape[1]
    tt = min(t, 8192)
    grid_spec = pltpu.PrefetchScalarGridSpec(
        num_scalar_prefetch=1,
        grid=(t // tt,),
        in_specs=[pl.BlockSpec((TOP_K, tt), lambda i, ps: (0, i)),
                  pl.BlockSpec((TOP_K, tt), lambda i, ps: (0, i))],
        out_specs=pl.BlockSpec((TOP_K, tt), lambda i, ps: (0, i)),
    )
    return pl.pallas_call(
        _dest_kernel,
        grid_spec=grid_spec,
        out_shape=jax.ShapeDtypeStruct((TOP_K, t), I32),
        compiler_params=_cparams(("parallel",)),
        name="route_dest",
    )(pad_start, idx, rank)


def _zero_fill_padding(pe_ref, nu_ref, xs_hbm, zbuf, zsem, first_tail_block):
    zbuf[...] = jnp.zeros_like(zbuf)
    conds, copies = [], []
    for e in range(N_EXPERTS):
        prev_end = pe_ref[e - 1] if e > 0 else 0
        conds.append(pe_ref[e] > prev_end)
        start = jnp.maximum(pe_ref[e] - TM_FFN, 0)
        copies.append(pltpu.make_async_copy(zbuf, xs_hbm.at[pl.ds(start, TM_FFN)], zsem))
    for j in range(N_EXPERTS):
        blk = first_tail_block + j
        conds.append(blk >= nu_ref[0])
        copies.append(pltpu.make_async_copy(zbuf, xs_hbm.at[pl.ds(blk * TM_FFN, TM_FFN)], zsem))
    for cond, c in zip(conds, copies):
        pl.when(cond)(c.start)
    for cond, c in zip(conds, copies):
        pl.when(cond)(c.wait)


def _dispatch_kernel(pe_ref, nu_ref, dest_ref, xn_hbm, xs_hbm, zbuf, sem, zsem):
    step = pl.program_id(0)
    n = pl.num_programs(0)
    first_tail_block = xs_hbm.shape[0] // TM_FFN - N_EXPERTS

    @pl.when(step == 0)
    def _():
        _zero_fill_padding(pe_ref, nu_ref, xs_hbm, zbuf, zsem, first_tail_block)

    def wait_one_step(slot):
        for _ in range(TD * TOP_K):
            pltpu.make_async_copy(xn_hbm.at[0], xs_hbm.at[0], sem.at[slot]).wait()

    for slot in range(2):
        @pl.when(step % 2 == slot)
        def _(slot=slot):
            for r in range(TD):
                src = xn_hbm.at[step * TD + r]
                for k in range(TOP_K):
                    pltpu.make_async_copy(src, xs_hbm.at[dest_ref[k, r]], sem.at[slot]).start()

            @pl.when(step > 0)
            def _():
                wait_one_step(1 - slot)

            @pl.when(step == n - 1)
            def _():
                wait_one_step(slot)


def _dispatch(pad_end, n_used, dest, xn3, a_pad):
    t = dest.shape[1]
    grid_spec = pltpu.PrefetchScalarGridSpec(
        num_scalar_prefetch=2,
        grid=(t // TD,),
        in_specs=[
            pl.BlockSpec((TOP_K, TD), lambda i, pe, nu: (0, i), memory_space=pltpu.SMEM),
            pl.BlockSpec(memory_space=pl.ANY),
        ],
        out_specs=pl.BlockSpec(memory_space=pl.ANY),
        scratch_shapes=[
            pltpu.VMEM((TM_FFN, ROW_TILE, LANES), F32),
            pltpu.SemaphoreType.DMA((2,)),
            pltpu.SemaphoreType.DMA(()),
        ],
    )
    return pl.pallas_call(
        _dispatch_kernel,
        grid_spec=grid_spec,
        out_shape=jax.ShapeDtypeStruct((a_pad, ROW_TILE, LANES), F32),
        compiler_params=_cparams(("arbitrary",)),
        name="dispatch",
    )(pad_end, n_used, dest, xn3)


def _ffn_kernel(be_ref, nu_ref, xs_ref, w1_ref, b1_ref, w2_ref, b2_ref, y_ref):
    i = pl.program_id(0)

    @pl.when(i >= nu_ref[0])
    def _():
        y_ref[...] = jnp.zeros_like(y_ref)

    @pl.when(i < nu_ref[0])
    def _():
        x = jnp.concatenate(
            [xs_ref[pl.ds(c, TM_FFN, stride=ROW_TILE), :] for c in range(ROW_TILE)], axis=1).astype(BF16)
        hu = jnp.dot(x, w1_ref[...], preferred_element_type=F32) + b1_ref[...]
        gate = jnp.minimum(hu[:, :D_FF], SWIGLU_LIMIT)
        lin = jnp.clip(hu[:, D_FF:], -SWIGLU_LIMIT, SWIGLU_LIMIT)
        act = gate * jax.nn.sigmoid(SWIGLU_ALPHA * gate) * (lin + 1.0)
        y = jnp.dot(act.astype(BF16), w2_ref[...], preferred_element_type=F32) + b2_ref[...]
        for c in range(ROW_TILE):
            y_ref[pl.ds(c, TM_FFN, stride=ROW_TILE), :] = y[:, LANES * c:LANES * (c + 1)]


def _expert_ffn(blk_e, n_used, xs2d, w1, b1, w2, b2, n_blocks):
    rows = TM_FFN * ROW_TILE

    def xmap(i, be, nu):
        return (jnp.minimum(i, nu[0] - 1), 0)

    def wmap(i, be, nu):
        return (be[i], 0, 0)

    grid_spec = pltpu.PrefetchScalarGridSpec(
        num_scalar_prefetch=2,
        grid=(n_blocks,),
        in_specs=[
            pl.BlockSpec((rows, LANES), xmap),
            pl.BlockSpec((None, D_MODEL, 2 * D_FF), wmap),
            pl.BlockSpec((None, 1, 2 * D_FF), wmap),
            pl.BlockSpec((None, D_FF, D_MODEL), wmap),
            pl.BlockSpec((None, 1, D_MODEL), wmap),
        ],
        out_specs=pl.BlockSpec((rows, LANES), lambda i, be, nu: (i, 0)),
    )
    return pl.pallas_call(
        _ffn_kernel,
        grid_spec=grid_spec,
        out_shape=jax.ShapeDtypeStruct(xs2d.shape, F32),
        compiler_params=_cparams(("arbitrary",)),
        name="expert_ffn",
    )(blk_e, n_used, xs2d, w1, b1, w2, b2)


def _combine_copies(dest_ref, y_hbm, buf, sem, slot):
    copies = []
    for k in range(TOP_K):
        for r in range(TD):
            row0 = pl.multiple_of(dest_ref[k, r] * ROW_TILE, ROW_TILE)
            dst0 = ((slot * TOP_K + k) * TD + r) * ROW_TILE
            copies.append(pltpu.make_async_copy(
                y_hbm.at[pl.ds(row0, ROW_TILE)], buf.at[pl.ds(dst0, ROW_TILE)], sem.at[slot]))
    return copies


def _combine_kernel(dcur_ref, dnext_ref, y_hbm, gates_ref, h1_ref, g_ref, o_ref, buf, sem):
    i = pl.program_id(0)
    n = pl.num_programs(0)

    @pl.when(i == 0)
    def _():
        for c in _combine_copies(dcur_ref, y_hbm, buf, sem, 0):
            c.start()

    for slot in range(2):
        @pl.when((i % 2 == slot) & (i + 1 < n))
        def _(slot=slot):
            for c in _combine_copies(dnext_ref, y_hbm, buf, sem, 1 - slot):
                c.start()

    for slot in range(2):
        @pl.when(i % 2 == slot)
        def _(slot=slot):
            for c in _combine_copies(dcur_ref, y_hbm, buf, sem, slot):
                c.wait()
            acc = h1_ref[...]
            gates = gates_ref[...]
            for k in range(TOP_K):
                base = (slot * TOP_K + k) * TD * ROW_TILE
                yk = jnp.concatenate(
                    [buf[pl.ds(base + c, TD, stride=ROW_TILE), :] for c in range(ROW_TILE)], axis=1)
                acc = acc + yk * gates[:, k:k + 1]
            ms = jnp.mean(acc * acc, axis=-1, keepdims=True)
            o_ref[...] = acc * lax.rsqrt(ms + NORM_EPS) * g_ref[...]


def _combine(dest, y2d, gates_t, h1, gain):
    t = h1.shape[0]
    n = t // TD
    return pl.pallas_call(
        _combine_kernel,
        grid=(n,),
        in_specs=[
            pl.BlockSpec((TOP_K, TD), lambda i: (0, i), memory_space=pltpu.SMEM),
            pl.BlockSpec((TOP_K, TD), lambda i: (0, jnp.minimum(i + 1, n - 1)), memory_space=pltpu.SMEM),
            pl.BlockSpec(memory_space=pl.ANY),
            pl.BlockSpec((TD, TOP_K), lambda i: (i, 0)),
            pl.BlockSpec((TD, D_MODEL), lambda i: (i, 0)),
            pl.BlockSpec((1, D_MODEL), lambda i: (0, 0)),
        ],
        out_specs=pl.BlockSpec((TD, D_MODEL), lambda i: (i, 0)),
        out_shape=jax.ShapeDtypeStruct((t, D_MODEL), F32),
        scratch_shapes=[
            pltpu.VMEM((2 * TOP_K * TD * ROW_TILE, LANES), F32),
            pltpu.SemaphoreType.DMA((2,)),
        ],
        compiler_params=_cparams(("arbitrary",)),
        name="combine",
    )(dest, dest, y2d, gates_t, h1, gain)


def _block_plan(counts, n_blocks):
    padded = (counts + TM_FFN - 1) // TM_FFN * TM_FFN
    pad_end = jnp.cumsum(padded)
    pad_start = pad_end - padded
    blk_start = jnp.arange(n_blocks, dtype=I32) * TM_FFN
    blk_e = jnp.minimum(jnp.searchsorted(pad_end, blk_start, side="right"), N_EXPERTS - 1).astype(I32)
    n_used = (pad_end[-1:] // TM_FFN).astype(I32)
    return pad_start.astype(I32), pad_end.astype(I32), blk_e, n_used


@jax.jit
def _forward(x, meta_tokens, rel_bias, attn_norm, w_in, w_out, lambda_q1, lambda_k1, lambda_q2,
             lambda_k2, subln_gain, ffn_norm, w_router, b_router, w1, b1, w2, b2, final_norm):
    b, s, _ = x.shape
    t = b * s
    assert s % TQ == 0 and TQ == TK and t % TM_PROJ == 0 and t % TD == 0
    nkb = s // TK
    x2d = x.reshape(t, D_MODEL)

    scale = HEAD_DIM ** -0.5
    colscale = jnp.ones((D_IN,), F32)
    colscale = colscale.at[0:D_SB].set(scale).at[3 * D_SB:3 * D_SB + D_DIFF].set(scale)[None, :]
    w_in_b = w_in[0].astype(BF16)
    g_attn = attn_norm[0][None, :]
    proj = _in_proj(x2d, g_attn, colscale, w_in_b, TM_PROJ)
    mproj = _in_proj(meta_tokens, g_attn, colscale, w_in_b, N_META)

    k_sb = proj[:, D_SB:2 * D_SB].reshape(b, nkb, SUBLANES, KCH, D_SB)
    k_perm = k_sb.transpose(0, 1, 3, 2, 4).reshape(t, D_SB)
    v_sb = proj[:, 2 * D_SB:3 * D_SB].reshape(b, nkb, SUBLANES, KCH, H_SB, HEAD_DIM)
    vt_perm = v_sb.transpose(0, 4, 1, 5, 3, 2).reshape(b, H_SB, nkb, HEAD_DIM, TK)
    v_df = proj[:, 3 * D_SB + 2 * D_DIFF:].reshape(b, nkb, TK, H_DIFF, 2 * HEAD_DIM)
    vt_df = v_df.transpose(0, 3, 1, 4, 2)
    mvt_sb = mproj[:, 2 * D_SB:3 * D_SB].reshape(N_META, H_SB, HEAD_DIM).transpose(1, 2, 0)
    mvt_df = mproj[:, 3 * D_SB + 2 * D_DIFF:].reshape(N_META, H_DIFF, 2 * HEAD_DIM).transpose(1, 2, 0)

    o_sb = _sb_attention(proj, k_perm, vt_perm, mproj, mvt_sb, b, s)

    bias_tiles, meta_bias = _rel_bias_tiles(rel_bias, s)
    far_bias = rel_bias[N_BUCKETS // 2 - 1]
    o_df = _diff_attention(proj, vt_df, mproj, mvt_df, bias_tiles, meta_bias, far_bias,
                           lambda_q1, lambda_k1, lambda_q2, lambda_k2,
                           subln_gain[0][:, None], b, s)

    tri = jnp.triu(jnp.ones((TM_PROJ, TM_PROJ), BF16), k=1)
    h1, xn2d, idx, gates, rank, cnt = _out_router(
        x2d, o_sb, o_df, w_out[0].astype(BF16), ffn_norm[0][None, :],
        w_router[0].T.astype(BF16), b_router[0][:, None], tri, TM_PROJ)

    a = t * TOP_K
    n_blocks = a // TM_FFN + N_EXPERTS
    a_pad = n_blocks * TM_FFN
    counts = cnt[:, 0].astype(I32)
    pad_start, pad_end, blk_e, n_used = _block_plan(counts, n_blocks)
    dest = _route_dest(pad_start, idx, rank)

    xs = _dispatch(pad_end, n_used, dest, xn2d.reshape(t, ROW_TILE, LANES), a_pad)
    y2d = _expert_ffn(blk_e, n_used, xs.reshape(a_pad * ROW_TILE, LANES),
                      w1[0].astype(BF16), b1[0][:, None, :], w2[0].astype(BF16), b2[0][:, None, :],
                      n_blocks)
    out = _combine(dest, y2d, gates.T, h1, final_norm[None, :])
    return out.reshape(b, s, D_MODEL)


def kernel(x, meta_tokens, rel_bias, attn_norm, w_in, w_out, lambda_q1, lambda_k1, lambda_q2,
           lambda_k2, subln_gain, ffn_norm, w_router, b_router, w1, b1, w2, b2, final_norm):
    return _forward(x, meta_tokens, rel_bias, attn_norm, w_in, w_out, lambda_q1, lambda_k1,
                    lambda_q2, lambda_k2, subln_gain, ffn_norm, w_router, b_router, w1, b1, w2, b2,
                    final_norm)
```

```python
import functools
import math

import jax
import jax.numpy as jnp
from jax import lax
from jax.experimental import pallas as pl
from jax.experimental.pallas import tpu as pltpu

D_MODEL = 1024
N_META = 16
CHUNK = 64
HEAD_DIM = 64
H_SB = 8
H_DIFF = 4
D_SB = H_SB * HEAD_DIM
D_DIFF = H_DIFF * 2 * HEAD_DIM
D_IN = 3 * D_SB + 3 * D_DIFF
N_BUCKETS = 32
N_EXPERTS = 32
TOP_K = 4
D_FF = D_MODEL
SWIGLU_ALPHA = 1.702
SWIGLU_LIMIT = 7.0
NORM_EPS = 1e-6
SUBLN_EPS = 1e-5
NEG_BIG = -1e30
LAMBDA_INIT = 0.8 - 0.6 * math.exp(-0.3 * 0)

LANES = 128
SUBLANES = 8
ROW_TILE = D_MODEL // LANES
VMEM_LIMIT = 56 * 1024 * 1024

TM_PROJ = 512
TQ = 256
TK = 256
KCH = TK // SUBLANES
TM_FFN = 256
TD = 256
DRAIN_UNROLL = 128
N_BIAS_TILES = 3
SB_EXIT = 104.0

F32 = jnp.float32
BF16 = jnp.bfloat16
I32 = jnp.int32

_NT = (((1,), (1,)), ((), ()))


def _cparams(sem, vmem=VMEM_LIMIT):
    return pltpu.CompilerParams(dimension_semantics=sem, vmem_limit_bytes=vmem)


def _inproj_kernel(x_ref, g_ref, cs_ref, w_ref, o_ref):
    x = x_ref[...]
    ms = jnp.mean(x * x, axis=-1, keepdims=True)
    xn = (x * lax.rsqrt(ms + NORM_EPS) * g_ref[...]).astype(BF16)
    y = jnp.dot(xn, w_ref[...], preferred_element_type=F32)
    o_ref[...] = (y * cs_ref[...]).astype(BF16)


def _in_proj(x2d, gain, colscale, w_bf16, tm):
    t = x2d.shape[0]
    return pl.pallas_call(
        _inproj_kernel,
        grid=(t // tm,),
        in_specs=[
            pl.BlockSpec((tm, D_MODEL), lambda i: (i, 0)),
            pl.BlockSpec((1, D_MODEL), lambda i: (0, 0)),
            pl.BlockSpec((1, D_IN), lambda i: (0, 0)),
            pl.BlockSpec((D_MODEL, D_IN), lambda i: (0, 0)),
        ],
        out_specs=pl.BlockSpec((tm, D_IN), lambda i: (i, 0)),
        out_shape=jax.ShapeDtypeStruct((t, D_IN), BF16),
        compiler_params=_cparams(("parallel",)),
        name="in_proj",
    )(x2d, gain, colscale, w_bf16)


def _bias_lookup(rel, rb_ref, h):
    n = jnp.abs(rel)
    n2 = n * n
    large = jnp.full(rel.shape, 8, I32)
    for k in range(1, 8):
        large = large + jnp.where(n2 >= (64 << k), 1, 0)
    bucket = jnp.where(rel > 0, N_BUCKETS // 2, 0) + jnp.where(n < 8, n, large)
    out = jnp.zeros(rel.shape, F32)
    for b in range(N_BUCKETS):
        out = jnp.where(bucket == b, rb_ref[b, h], out)
    return out


def _relbias_kernel(rb_ref, bt_ref, mb_ref):
    h = pl.program_id(0)
    krow = lax.broadcasted_iota(I32, (TK, TQ), 0)
    qcol = lax.broadcasted_iota(I32, (TK, TQ), 1)
    visible = (krow // CHUNK) <= (qcol // CHUNK)
    bt_ref[0, 0] = jnp.where(visible, _bias_lookup(krow - qcol, rb_ref, h), NEG_BIG)
    bt_ref[0, 1] = _bias_lookup(krow - qcol - TK, rb_ref, h)
    bt_ref[0, 2] = _bias_lookup(krow - qcol - 2 * TK, rb_ref, h)
    s = mb_ref.shape[2]
    mrow = lax.broadcasted_iota(I32, (N_META, s), 0)
    qpos = lax.broadcasted_iota(I32, (N_META, s), 1) + N_META
    mb_ref[0] = _bias_lookup(mrow - qpos, rb_ref, h)


def _rel_bias_tiles(rel_bias, s):
    return pl.pallas_call(
        _relbias_kernel,
        grid=(H_DIFF,),
        in_specs=[pl.BlockSpec(memory_space=pltpu.SMEM)],
        out_specs=[
            pl.BlockSpec((1, N_BIAS_TILES, TK, TQ), lambda h: (h, 0, 0, 0)),
            pl.BlockSpec((1, N_META, s), lambda h: (h, 0, 0)),
        ],
        out_shape=[
            jax.ShapeDtypeStruct((H_DIFF, N_BIAS_TILES, TK, TQ), F32),
            jax.ShapeDtypeStruct((H_DIFF, N_META, s), F32),
        ],
        compiler_params=_cparams(("arbitrary",)),
        name="rel_bias",
    )(rel_bias)


def _suffix_incl_sublanes(x):
    r = lax.broadcasted_iota(I32, x.shape, 0)
    for d in (1, 2, 4):
        shifted = pltpu.roll(x, SUBLANES - d, axis=0)
        x = x + jnp.where(r + d < SUBLANES, shifted, 0.0)
    return x


def _softplus(s):
    return jnp.maximum(s, 0.0) + jnp.log(1.0 + jnp.exp(-jnp.abs(s)))


def _head_half(qpair, half):
    lane = lax.broadcasted_iota(I32, qpair.shape, 1)
    keep = (lane >= HEAD_DIM * half) & (lane < HEAD_DIM * (half + 1))
    return jnp.where(keep, qpair, jnp.zeros_like(qpair))


def _sb_block(s, vt, carry, acc, valid):
    sp = _softplus(s)
    if valid is not None:
        sp = jnp.where(valid, sp, 0.0)
    run = jnp.zeros((SUBLANES, s.shape[1]), F32)
    parts = [None] * KCH
    for i in reversed(range(KCH)):
        run = run + sp[SUBLANES * i:SUBLANES * (i + 1), :]
        parts[i] = run
    incl = _suffix_incl_sublanes(run)
    base = (incl - run) + carry
    r_sum = jnp.concatenate([p + base for p in parts], axis=0)
    w = jnp.exp(s - r_sum)
    if valid is not None:
        w = jnp.where(valid, w, 0.0)
    acc = acc + jnp.dot(vt, w.astype(BF16), preferred_element_type=F32)
    return carry + incl[0:1, :], acc


def _sb_meta_block(s, vt, carry, acc):
    sp = _softplus(s)
    lo, hi = sp[0:SUBLANES, :], sp[SUBLANES:2 * SUBLANES, :]
    hi_incl = _suffix_incl_sublanes(hi)
    lo_incl = _suffix_incl_sublanes(lo) + hi_incl[0:1, :]
    r_sum = jnp.concatenate([lo_incl, hi_incl], axis=0) + carry
    w = jnp.exp(s - r_sum)
    return acc + jnp.dot(vt, w.astype(BF16), preferred_element_type=F32)


def _sb_kernel(q_ref, k_ref, vt_ref, mk_ref, mvt_ref, o_ref):
    qi = pl.program_id(2)
    qpair = q_ref[...]
    row = lax.broadcasted_iota(I32, (TK, TQ), 0)
    lane = lax.broadcasted_iota(I32, (TK, TQ), 1)
    key_off = (row % SUBLANES) * KCH + row // SUBLANES
    causal = key_off < lane
    outs = []
    for half in range(2):
        qz = _head_half(qpair, half)

        def scores(kb):
            start = pl.multiple_of(kb * TK, TK)
            return lax.dot_general(k_ref[pl.ds(start, TK), :], qz, _NT, preferred_element_type=F32)

        def alive(carry):
            return (jnp.min(carry) < SB_EXIT).astype(I32)

        s_diag = scores(qi)
        s_next = scores(jnp.maximum(qi - 1, 0))
        carry = jnp.zeros((1, TQ), F32)
        acc = jnp.zeros((HEAD_DIM, TQ), F32)
        carry, acc = _sb_block(s_diag, vt_ref[half, qi], carry, acc, causal)

        def cond(st):
            return (st[0] < qi) & (st[1] > 0)

        def body(st):
            j, _, carry, acc, s_cur = st
            kb = qi - 1 - j
            s_nxt = scores(jnp.maximum(kb - 1, 0))
            carry, acc = _sb_block(s_cur, vt_ref[half, kb], carry, acc, None)
            return j + 1, alive(carry), carry, acc, s_nxt

        _, live, carry, acc, _ = lax.while_loop(cond, body, (jnp.int32(0), alive(carry), carry, acc, s_next))

        def meta(acc):
            sm = lax.dot_general(mk_ref[...], qz, _NT, preferred_element_type=F32)
            return _sb_meta_block(sm, mvt_ref[half], carry, acc)

        acc = lax.cond(live > 0, meta, lambda a: a, acc)
        outs.append(acc)
    o_ref[...] = jnp.concatenate(outs, axis=0).T.astype(BF16)


def _sb_attention(proj, k_perm, vt_perm, mproj, mvt, b, s):
    nq = s // TQ
    nkb = s // TK
    qcol0 = 0
    mkcol0 = D_SB // LANES
    return pl.pallas_call(
        _sb_kernel,
        grid=(b, H_SB // 2, nq),
        in_specs=[
            pl.BlockSpec((TQ, LANES), lambda bi, p, qi: (bi * nq + qi, qcol0 + p)),
            pl.BlockSpec((s, LANES), lambda bi, p, qi: (bi, p)),
            pl.BlockSpec((None, 2, nkb, HEAD_DIM, TK), lambda bi, p, qi: (bi, p, 0, 0, 0)),
            pl.BlockSpec((N_META, LANES), lambda bi, p, qi: (0, mkcol0 + p)),
            pl.BlockSpec((2, HEAD_DIM, N_META), lambda bi, p, qi: (p, 0, 0)),
        ],
        out_specs=pl.BlockSpec((TQ, LANES), lambda bi, p, qi: (bi * nq + qi, p)),
        out_shape=jax.ShapeDtypeStruct((b * s, D_SB), BF16),
        compiler_params=_cparams(("parallel", "parallel", "arbitrary")),
        name="sb_attn",
    )(proj, k_perm, vt_perm, mproj, mvt)


def _df_update(s, vt, st):
    m, l, acc = st
    m_new = jnp.maximum(m, jnp.max(s, axis=0, keepdims=True))
    alpha = jnp.exp(m - m_new)
    p = jnp.exp(s - m_new)
    l = alpha * l + jnp.sum(p, axis=0, keepdims=True)
    acc = alpha * acc + jnp.dot(vt, p.astype(BF16), preferred_element_type=F32)
    return m_new, l, acc


def _df_kernel(q_ref, k_ref, vt_ref, mk_ref, mvt_ref, bt_ref, mb_ref,
               lq1_ref, lk1_ref, lq2_ref, lk2_ref, gain_ref, o_ref):
    qi = pl.program_id(2)
    qpair = q_ref[...]
    qz = [_head_half(qpair, 0), _head_half(qpair, 1)]

    def scores(d):
        kb = jnp.maximum(qi - d, 0)
        kblk = k_ref[pl.ds(pl.multiple_of(kb * TK, TK), TK), :]
        bias = bt_ref[jnp.minimum(d, N_BIAS_TILES - 1)]
        return tuple(lax.dot_general(kblk, qz[i], _NT, preferred_element_type=F32) + bias
                     for i in range(2))

    def init():
        return (jnp.full((1, TQ), -jnp.inf, F32), jnp.zeros((1, TQ), F32),
                jnp.zeros((2 * HEAD_DIM, TQ), F32))

    def body(d, carry):
        st, s_cur = carry
        s_nxt = scores(d + 1)
        vt = vt_ref[qi - d]
        return tuple(_df_update(s_cur[i], vt, st[i]) for i in range(2)), s_nxt

    st, _ = lax.fori_loop(0, qi + 1, body, ((init(), init()), scores(0)))
    mk = mk_ref[...]
    mvt = mvt_ref[...]
    mb = mb_ref[...]
    st = tuple(
        _df_update(lax.dot_general(mk, qz[i], _NT, preferred_element_type=F32) + mb, mvt, st[i])
        for i in range(2))

    lam = (jnp.exp(jnp.sum(lq1_ref[...] * lk1_ref[...], axis=-1, keepdims=True))
           - jnp.exp(jnp.sum(lq2_ref[...] * lk2_ref[...], axis=-1, keepdims=True))
           + LAMBDA_INIT)
    o = st[0][2] / st[0][1] - lam * (st[1][2] / st[1][1])
    ms = jnp.mean(o * o, axis=0, keepdims=True)
    y = o * lax.rsqrt(ms + SUBLN_EPS) * gain_ref[...] * (1.0 - LAMBDA_INIT)
    o_ref[...] = y.T.astype(BF16)


def _diff_attention(proj, vt, mproj, mvt, bias_tiles, meta_bias,
                    lq1, lk1, lq2, lk2, gain_col, b, s):
    nq = s // TQ
    nkb = s // TK
    qcol0 = 3 * D_SB // LANES
    kcol0 = (3 * D_SB + D_DIFF) // LANES
    lam_spec = pl.BlockSpec((1, HEAD_DIM), lambda bi, h, qi: (0, 0))
    return pl.pallas_call(
        _df_kernel,
        grid=(b, H_DIFF, nq),
        in_specs=[
            pl.BlockSpec((TQ, LANES), lambda bi, h, qi: (bi * nq + qi, qcol0 + h)),
            pl.BlockSpec((s, LANES), lambda bi, h, qi: (bi, kcol0 + h)),
            pl.BlockSpec((None, None, nkb, 2 * HEAD_DIM, TK), lambda bi, h, qi: (bi, h, 0, 0, 0)),
            pl.BlockSpec((N_META, LANES), lambda bi, h, qi: (0, kcol0 + h)),
            pl.BlockSpec((None, 2 * HEAD_DIM, N_META), lambda bi, h, qi: (h, 0, 0)),
            pl.BlockSpec((None, N_BIAS_TILES, TK, TQ), lambda bi, h, qi: (h, 0, 0, 0)),
            pl.BlockSpec((None, N_META, TQ), lambda bi, h, qi: (h, 0, qi)),
            lam_spec, lam_spec, lam_spec, lam_spec,
            pl.BlockSpec((2 * HEAD_DIM, 1), lambda bi, h, qi: (0, 0)),
        ],
        out_specs=pl.BlockSpec((TQ, LANES), lambda bi, h, qi: (bi * nq + qi, h)),
        out_shape=jax.ShapeDtypeStruct((b * s, D_DIFF), BF16),
        compiler_params=_cparams(("parallel", "parallel", "arbitrary")),
        name="diff_attn",
    )(proj, proj, vt, mproj, mvt, bias_tiles, meta_bias, lq1, lk1, lq2, lk2, gain_col)


def _outrouter_kernel(x_ref, osb_ref, odf_ref, wo_ref, g_ref, wrt_ref, br_ref, tri_ref,
                      h1_ref, xn_ref, idx_ref, gate_ref, rank_ref, cnt_ref, carry_ref):
    @pl.when(pl.program_id(0) == 0)
    def _():
        carry_ref[...] = jnp.zeros_like(carry_ref)

    tm = x_ref.shape[0]
    mix = jnp.concatenate([osb_ref[...], odf_ref[...]], axis=1)
    h1 = x_ref[...] + jnp.dot(mix, wo_ref[...], preferred_element_type=F32)
    h1_ref[...] = h1
    ms = jnp.mean(h1 * h1, axis=-1, keepdims=True)
    xn = h1 * lax.rsqrt(ms + NORM_EPS) * g_ref[...]
    for c in range(ROW_TILE):
        xn_ref[pl.ds(c, tm, stride=ROW_TILE), :] = xn[:, LANES * c:LANES * (c + 1)]

    logits = lax.dot_general(wrt_ref[...], xn.astype(BF16), _NT, preferred_element_type=F32) + br_ref[...]
    e_iota = lax.broadcasted_iota(I32, logits.shape, 0)
    work = logits
    vals, idxs = [], []
    for _ in range(TOP_K):
        m = jnp.max(work, axis=0, keepdims=True)
        ik = jnp.min(jnp.where(work == m, e_iota, N_EXPERTS), axis=0, keepdims=True)
        vals.append(m)
        idxs.append(ik)
        work = jnp.where(e_iota == ik, -jnp.inf, work)
    exps = [jnp.exp(v - vals[0]) for v in vals]
    den = exps[0] + exps[1] + exps[2] + exps[3]
    onehot = jnp.zeros(logits.shape, F32)
    for ik in idxs:
        onehot = onehot + jnp.where(e_iota == ik, 1.0, 0.0)
    prefix = jnp.dot(onehot.astype(BF16), tri_ref[...], preferred_element_type=F32)
    pos = prefix + carry_ref[:, 0:1]
    for k in range(TOP_K):
        idx_ref[k:k + 1, :] = idxs[k]
        gate_ref[k:k + 1, :] = exps[k] / den
        rank_ref[k:k + 1, :] = jnp.sum(jnp.where(e_iota == idxs[k], pos, 0.0), axis=0,
                                       keepdims=True).astype(I32)
    carry_ref[...] = carry_ref[...] + jnp.sum(onehot, axis=1, keepdims=True)
    cnt_ref[...] = carry_ref[...]


def _out_router(x2d, o_sb, o_df, wo_bf16, gain, wr_t, br_col, tri, tm):
    t = x2d.shape[0]
    const = lambda i: (0, 0)
    return pl.pallas_call(
        _outrouter_kernel,
        grid=(t // tm,),
        in_specs=[
            pl.BlockSpec((tm, D_MODEL), lambda i: (i, 0)),
            pl.BlockSpec((tm, D_SB), lambda i: (i, 0)),
            pl.BlockSpec((tm, D_DIFF), lambda i: (i, 0)),
            pl.BlockSpec((D_SB + D_DIFF, D_MODEL), const),
            pl.BlockSpec((1, D_MODEL), const),
            pl.BlockSpec((N_EXPERTS, D_MODEL), const),
            pl.BlockSpec((N_EXPERTS, 1), const),
            pl.BlockSpec((tm, tm), const),
        ],
        out_specs=[
            pl.BlockSpec((tm, D_MODEL), lambda i: (i, 0)),
            pl.BlockSpec((tm * ROW_TILE, LANES), lambda i: (i, 0)),
            pl.BlockSpec((TOP_K, tm), lambda i: (0, i)),
            pl.BlockSpec((TOP_K, tm), lambda i: (0, i)),
            pl.BlockSpec((TOP_K, tm), lambda i: (0, i)),
            pl.BlockSpec((N_EXPERTS, LANES), const),
        ],
        out_shape=[
            jax.ShapeDtypeStruct((t, D_MODEL), F32),
            jax.ShapeDtypeStruct((t * ROW_TILE, LANES), F32),
            jax.ShapeDtypeStruct((TOP_K, t), I32),
            jax.ShapeDtypeStruct((TOP_K, t), F32),
            jax.ShapeDtypeStruct((TOP_K, t), I32),
            jax.ShapeDtypeStruct((N_EXPERTS, LANES), F32),
        ],
        scratch_shapes=[pltpu.VMEM((N_EXPERTS, LANES), F32)],
        compiler_params=_cparams(("arbitrary",)),
        name="out_router",
    )(x2d, o_sb, o_df, wo_bf16, gain, wr_t, br_col, tri)


def _dest_kernel(ps_ref, idx_ref, rank_ref, dest_ref):
    idx = idx_ref[...]
    off = jnp.zeros(idx.shape, I32)
    for e in range(N_EXPERTS):
        off = jnp.where(idx == e, ps_ref[e], off)
    dest_ref[...] = rank_ref[...] + off


def _route_dest(pad_start, idx, rank):
    t = idx.shape[1]
    tt = min(t, 8192)
    grid_spec = pltpu.PrefetchScalarGridSpec(
        num_scalar_prefetch=1,
        grid=(t // tt,),
        in_specs=[pl.BlockSpec((TOP_K, tt), lambda i, ps: (0, i)),
                  pl.BlockSpec((TOP_K, tt), lambda i, ps: (0, i))],
        out_specs=pl.BlockSpec((TOP_K, tt), lambda i, ps: (0, i)),
    )
    return pl.pallas_call(
        _dest_kernel,
        grid_spec=grid_spec,
        out_shape=jax.ShapeDtypeStruct((TOP_K, t), I32),
        compiler_params=_cparams(("parallel",)),
        name="route_dest",
    )(pad_start, idx, rank)


def _zero_fill_padding(pe_ref, nu_ref, xs_hbm, zbuf, zsem, first_tail_block):
    zbuf[...] = jnp.zeros_like(zbuf)
    conds, copies = [], []
    for e in range(N_EXPERTS):
        prev_end = pe_ref[e - 1] if e > 0 else 0
        conds.append(pe_ref[e] > prev_end)
        start = jnp.maximum(pe_ref[e] - TM_FFN, 0)
        copies.append(pltpu.make_async_copy(zbuf, xs_hbm.at[pl.ds(start, TM_FFN)], zsem))
    for j in range(N_EXPERTS):
        blk = first_tail_block + j
        conds.append(blk >= nu_ref[0])
        copies.append(pltpu.make_async_copy(zbuf, xs_hbm.at[pl.ds(blk * TM_FFN, TM_FFN)], zsem))
    for cond, c in zip(conds, copies):
        pl.when(cond)(c.start)
    for cond, c in zip(conds, copies):
        pl.when(cond)(c.wait)


def _dispatch_kernel(pe_ref, nu_ref, dest_ref, xn_ref, xs_hbm, zbuf, sem, zsem):
    step = pl.program_id(0)
    first_tail_block = xs_hbm.shape[0] // TM_FFN - N_EXPERTS

    @pl.when(step == 0)
    def _():
        _zero_fill_padding(pe_ref, nu_ref, xs_hbm, zbuf, zsem, first_tail_block)

    def issue(r, _):
        for k in range(TOP_K):
            pltpu.make_async_copy(xn_ref.at[r], xs_hbm.at[dest_ref[k, r]], sem).start()
        return 0

    lax.fori_loop(0, TD, issue, 0, unroll=8)

    def drain(_, carry):
        for _ in range(DRAIN_UNROLL):
            pltpu.make_async_copy(xn_ref.at[0], xs_hbm.at[0], sem).wait()
        return carry

    lax.fori_loop(0, TD * TOP_K // DRAIN_UNROLL, drain, 0)


def _dispatch(pad_end, n_used, dest, xn3, a_pad):
    t = dest.shape[1]
    grid_spec = pltpu.PrefetchScalarGridSpec(
        num_scalar_prefetch=2,
        grid=(t // TD,),
        in_specs=[
            pl.BlockSpec((TOP_K, TD), lambda i, pe, nu: (0, i), memory_space=pltpu.SMEM),
            pl.BlockSpec((TD, ROW_TILE, LANES), lambda i, pe, nu: (i, 0, 0)),
        ],
        out_specs=pl.BlockSpec(memory_space=pl.ANY),
        scratch_shapes=[
            pltpu.VMEM((TM_FFN, ROW_TILE, LANES), F32),
            pltpu.SemaphoreType.DMA(()),
            pltpu.SemaphoreType.DMA(()),
        ],
    )
    return pl.pallas_call(
        _dispatch_kernel,
        grid_spec=grid_spec,
        out_shape=jax.ShapeDtypeStruct((a_pad, ROW_TILE, LANES), F32),
        compiler_params=_cparams(("arbitrary",)),
        name="dispatch",
    )(pad_end, n_used, dest, xn3)


def _ffn_kernel(be_ref, nu_ref, xs_ref, w1_ref, b1_ref, w2_ref, b2_ref, y_ref):
    i = pl.program_id(0)

    @pl.when(i >= nu_ref[0])
    def _():
        y_ref[...] = jnp.zeros_like(y_ref)

    @pl.when(i < nu_ref[0])
    def _():
        x = jnp.concatenate(
            [xs_ref[pl.ds(c, TM_FFN, stride=ROW_TILE), :] for c in range(ROW_TILE)], axis=1).astype(BF16)
        hu = jnp.dot(x, w1_ref[...], preferred_element_type=F32) + b1_ref[...]
        gate = jnp.minimum(hu[:, :D_FF], SWIGLU_LIMIT)
        lin = jnp.clip(hu[:, D_FF:], -SWIGLU_LIMIT, SWIGLU_LIMIT)
        act = gate * jax.nn.sigmoid(SWIGLU_ALPHA * gate) * (lin + 1.0)
        y = jnp.dot(act.astype(BF16), w2_ref[...], preferred_element_type=F32) + b2_ref[...]
        for c in range(ROW_TILE):
            y_ref[pl.ds(c, TM_FFN, stride=ROW_TILE), :] = y[:, LANES * c:LANES * (c + 1)]


def _expert_ffn(blk_e, n_used, xs2d, w1, b1, w2, b2, n_blocks):
    rows = TM_FFN * ROW_TILE

    def xmap(i, be, nu):
        return (jnp.minimum(i, nu[0] - 1), 0)

    def wmap(i, be, nu):
        return (be[i], 0, 0)

    grid_spec = pltpu.PrefetchScalarGridSpec(
        num_scalar_prefetch=2,
        grid=(n_blocks,),
        in_specs=[
            pl.BlockSpec((rows, LANES), xmap),
            pl.BlockSpec((None, D_MODEL, 2 * D_FF), wmap),
            pl.BlockSpec((None, 1, 2 * D_FF), wmap),
            pl.BlockSpec((None, D_FF, D_MODEL), wmap),
            pl.BlockSpec((None, 1, D_MODEL), wmap),
        ],
        out_specs=pl.BlockSpec((rows, LANES), lambda i, be, nu: (i, 0)),
    )
    return pl.pallas_call(
        _ffn_kernel,
        grid_spec=grid_spec,
        out_shape=jax.ShapeDtypeStruct(xs2d.shape, F32),
        compiler_params=_cparams(("arbitrary",)),
        name="expert_ffn",
    )(blk_e, n_used, xs2d, w1, b1, w2, b2)


def _combine_gather(dest_ref, y_hbm, buf, sem, slot):
    def issue(r, carry):
        for k in range(TOP_K):
            row0 = pl.multiple_of(dest_ref[k, r] * ROW_TILE, ROW_TILE)
            dst0 = pl.multiple_of(((slot * TOP_K + k) * TD + r) * ROW_TILE, ROW_TILE)
            pltpu.make_async_copy(
                y_hbm.at[pl.ds(row0, ROW_TILE)], buf.at[pl.ds(dst0, ROW_TILE)], sem.at[slot]).start()
        return carry

    lax.fori_loop(0, TD, issue, 0, unroll=8)


def _combine_drain(y_hbm, buf, sem, slot):
    def drain(_, carry):
        for _ in range(DRAIN_UNROLL):
            pltpu.make_async_copy(
                y_hbm.at[pl.ds(0, ROW_TILE)], buf.at[pl.ds(0, ROW_TILE)], sem.at[slot]).wait()
        return carry

    lax.fori_loop(0, TD * TOP_K // DRAIN_UNROLL, drain, 0)


def _combine_kernel(dcur_ref, dnext_ref, y_hbm, gates_ref, h1_ref, g_ref, o_ref, buf, sem):
    i = pl.program_id(0)
    n = pl.num_programs(0)

    @pl.when(i == 0)
    def _():
        _combine_gather(dcur_ref, y_hbm, buf, sem, 0)

    for slot in range(2):
        @pl.when((i % 2 == slot) & (i + 1 < n))
        def _(slot=slot):
            _combine_gather(dnext_ref, y_hbm, buf, sem, 1 - slot)

    for slot in range(2):
        @pl.when(i % 2 == slot)
        def _(slot=slot):
            _combine_drain(y_hbm, buf, sem, slot)
            acc = h1_ref[...]
            gates = gates_ref[...]
            for k in range(TOP_K):
                base = (slot * TOP_K + k) * TD * ROW_TILE
                yk = jnp.concatenate(
                    [buf[pl.ds(base + c, TD, stride=ROW_TILE), :] for c in range(ROW_TILE)], axis=1)
                acc = acc + yk * gates[:, k:k + 1]
            ms = jnp.mean(acc * acc, axis=-1, keepdims=True)
            o_ref[...] = acc * lax.rsqrt(ms + NORM_EPS) * g_ref[...]


def _combine(dest, y2d, gates_t, h1, gain):
    t = h1.shape[0]
    n = t // TD
    return pl.pallas_call(
        _combine_kernel,
        grid=(n,),
        in_specs=[
            pl.BlockSpec((TOP_K, TD), lambda i: (0, i), memory_space=pltpu.SMEM),
            pl.BlockSpec((TOP_K, TD), lambda i: (0, jnp.minimum(i + 1, n - 1)), memory_space=pltpu.SMEM),
            pl.BlockSpec(memory_space=pl.ANY),
            pl.BlockSpec((TD, TOP_K), lambda i: (i, 0)),
            pl.BlockSpec((TD, D_MODEL), lambda i: (i, 0)),
            pl.BlockSpec((1, D_MODEL), lambda i: (0, 0)),
        ],
        out_specs=pl.BlockSpec((TD, D_MODEL), lambda i: (i, 0)),
        out_shape=jax.ShapeDtypeStruct((t, D_MODEL), F32),
        scratch_shapes=[
            pltpu.VMEM((2 * TOP_K * TD * ROW_TILE, LANES), F32),
            pltpu.SemaphoreType.DMA((2,)),
        ],
        compiler_params=_cparams(("arbitrary",)),
        name="combine",
    )(dest, dest, y2d, gates_t, h1, gain)


def _block_plan(counts, n_blocks):
    padded = (counts + TM_FFN - 1) // TM_FFN * TM_FFN
    pad_end = jnp.cumsum(padded)
    pad_start = pad_end - padded
    blk_start = jnp.arange(n_blocks, dtype=I32) * TM_FFN
    blk_e = jnp.minimum(jnp.searchsorted(pad_end, blk_start, side="right"), N_EXPERTS - 1).astype(I32)
    n_used = (pad_end[-1:] // TM_FFN).astype(I32)
    return pad_start.astype(I32), pad_end.astype(I32), blk_e, n_used


@jax.jit
def _forward(x, meta_tokens, rel_bias, attn_norm, w_in, w_out, lambda_q1, lambda_k1, lambda_q2,
             lambda_k2, subln_gain, ffn_norm, w_router, b_router, w1, b1, w2, b2, final_norm):
    b, s, _ = x.shape
    t = b * s
    assert s % TQ == 0 and TQ == TK and t % TM_PROJ == 0 and t % TD == 0
    nkb = s // TK
    x2d = x.reshape(t, D_MODEL)

    scale = HEAD_DIM ** -0.5
    colscale = jnp.ones((D_IN,), F32)
    colscale = colscale.at[0:D_SB].set(scale).at[3 * D_SB:3 * D_SB + D_DIFF].set(scale)[None, :]
    w_in_b = w_in[0].astype(BF16)
    g_attn = attn_norm[0][None, :]
    proj = _in_proj(x2d, g_attn, colscale, w_in_b, TM_PROJ)
    mproj = _in_proj(meta_tokens, g_attn, colscale, w_in_b, N_META)

    k_sb = proj[:, D_SB:2 * D_SB].reshape(b, nkb, SUBLANES, KCH, D_SB)
    k_perm = k_sb.transpose(0, 1, 3, 2, 4).reshape(t, D_SB)
    v_sb = proj[:, 2 * D_SB:3 * D_SB].reshape(b, nkb, SUBLANES, KCH, H_SB, HEAD_DIM)
    vt_perm = v_sb.transpose(0, 4, 1, 5, 3, 2).reshape(b, H_SB, nkb, HEAD_DIM, TK)
    v_df = proj[:, 3 * D_SB + 2 * D_DIFF:].reshape(b, nkb, TK, H_DIFF, 2 * HEAD_DIM)
    vt_df = v_df.transpose(0, 3, 1, 4, 2)
    mvt_sb = mproj[:, 2 * D_SB:3 * D_SB].reshape(N_META, H_SB, HEAD_DIM).transpose(1, 2, 0)
    mvt_df = mproj[:, 3 * D_SB + 2 * D_DIFF:].reshape(N_META, H_DIFF, 2 * HEAD_DIM).transpose(1, 2, 0)

    o_sb = _sb_attention(proj, k_perm, vt_perm, mproj, mvt_sb, b, s)

    bias_tiles, meta_bias = _rel_bias_tiles(rel_bias, s)
    o_df = _diff_attention(proj, vt_df, mproj, mvt_df, bias_tiles, meta_bias,
                           lambda_q1, lambda_k1, lambda_q2, lambda_k2,
                           subln_gain[0][:, None], b, s)

    tri = jnp.triu(jnp.ones((TM_PROJ, TM_PROJ), BF16), k=1)
    h1, xn2d, idx, gates, rank, cnt = _out_router(
        x2d, o_sb, o_df, w_out[0].astype(BF16), ffn_norm[0][None, :],
        w_router[0].T.astype(BF16), b_router[0][:, None], tri, TM_PROJ)

    a = t * TOP_K
    n_blocks = a // TM_FFN + N_EXPERTS
    a_pad = n_blocks * TM_FFN
    counts = cnt[:, 0].astype(I32)
    pad_start, pad_end, blk_e, n_used = _block_plan(counts, n_blocks)
    dest = _route_dest(pad_start, idx, rank)

    xs = _dispatch(pad_end, n_used, dest, xn2d.reshape(t, ROW_TILE, LANES), a_pad)
    y2d = _expert_ffn(blk_e, n_used, xs.reshape(a_pad * ROW_TILE, LANES),
                      w1[0].astype(BF16), b1[0][:, None, :], w2[0].astype(BF16), b2[0][:, None, :],
                      n_blocks)
    out = _combine(dest, y2d, gates.T, h1, final_norm[None, :])
    return out.reshape(b, s, D_MODEL)


def kernel(x, meta_tokens, rel_bias, attn_norm, w_in, w_out, lambda_q1, lambda_k1, lambda_q2,
           lambda_k2, subln_gain, ffn_norm, w_router, b_router, w1, b1, w2, b2, final_norm):
    return _forward(x, meta_tokens, rel_bias, attn_norm, w_in, w_out, lambda_q1, lambda_k1,
                    lambda_q2, lambda_k2, subln_gain, ffn_norm, w_router, b_router, w1, b1, w2, b2,
                    final_norm)
```

```python
import functools
import math

import jax
import jax.numpy as jnp
from jax import lax
from jax.experimental import pallas as pl
from jax.experimental.pallas import tpu as pltpu

D_MODEL = 1024
N_META = 16
CHUNK = 64
HEAD_DIM = 64
H_SB = 8
H_DIFF = 4
D_SB = H_SB * HEAD_DIM
D_DIFF = H_DIFF * 2 * HEAD_DIM
D_IN = 3 * D_SB + 3 * D_DIFF
N_BUCKETS = 32
N_EXPERTS = 32
TOP_K = 4
D_FF = D_MODEL
SWIGLU_ALPHA = 1.702
SWIGLU_LIMIT = 7.0
NORM_EPS = 1e-6
SUBLN_EPS = 1e-5
NEG_BIG = -1e30
LAMBDA_INIT = 0.8 - 0.6 * math.exp(-0.3 * 0)

LANES = 128
SUBLANES = 8
ROW_TILE = D_MODEL // LANES
VMEM_LIMIT = 56 * 1024 * 1024

TM_PROJ = 512
TQ = 256
TK = 256
KCH = TK // SUBLANES
TB = 512
L_ROWS = 16
TM_FFN = 512
TD = 256
DRAIN_UNROLL = 128
N_BIAS_TILES = 3
SB_EXIT = 104.0

F32 = jnp.float32
BF16 = jnp.bfloat16
I32 = jnp.int32

_NT = (((1,), (1,)), ((), ()))


def _cparams(sem, vmem=VMEM_LIMIT):
    return pltpu.CompilerParams(dimension_semantics=sem, vmem_limit_bytes=vmem)


def _inproj_kernel(x_ref, g_ref, cs_ref, w_ref, o_ref):
    x = x_ref[...]
    ms = jnp.mean(x * x, axis=-1, keepdims=True)
    xn = (x * lax.rsqrt(ms + NORM_EPS) * g_ref[...]).astype(BF16)
    y = jnp.dot(xn, w_ref[...], preferred_element_type=F32)
    o_ref[...] = (y * cs_ref[...]).astype(BF16)


def _in_proj_meta(x2d, gain, colscale, w_bf16):
    t = x2d.shape[0]
    return pl.pallas_call(
        _inproj_kernel,
        grid=(1,),
        in_specs=[
            pl.BlockSpec((t, D_MODEL), lambda i: (0, 0)),
            pl.BlockSpec((1, D_MODEL), lambda i: (0, 0)),
            pl.BlockSpec((1, D_IN), lambda i: (0, 0)),
            pl.BlockSpec((D_MODEL, D_IN), lambda i: (0, 0)),
        ],
        out_specs=pl.BlockSpec((t, D_IN), lambda i: (0, 0)),
        out_shape=jax.ShapeDtypeStruct((t, D_IN), BF16),
        compiler_params=_cparams(("arbitrary",)),
        name="in_proj_meta",
    )(x2d, gain, colscale, w_bf16)


def _rms_bf16(x, g):
    ms = jnp.mean(x * x, axis=-1, keepdims=True)
    return (x * lax.rsqrt(ms + NORM_EPS) * g).astype(BF16)


def _inproj_tokens_kernel(x_ref, g_ref, cs_ref, perm_ref, wa_ref, wk_ref, wvs_ref, wvd_ref,
                          a_ref, kp_ref, vts_ref, vtd_ref):
    xn = _rms_bf16(x_ref[...], g_ref[...])
    xnp = jnp.concatenate(
        [jnp.dot(perm_ref[...], xn[blk * TK:(blk + 1) * TK], preferred_element_type=F32)
         for blk in range(TM_PROJ // TK)], axis=0).astype(BF16)
    a_ref[...] = (jnp.dot(xn, wa_ref[...], preferred_element_type=F32) * cs_ref[...]).astype(BF16)
    kp_ref[...] = jnp.dot(xnp, wk_ref[...], preferred_element_type=F32).astype(BF16)
    vts = lax.dot_general(wvs_ref[...], xnp, _NT, preferred_element_type=F32).astype(BF16)
    for blk in range(TM_PROJ // TK):
        vts_ref[blk] = vts[:, blk * TK:(blk + 1) * TK]
    vtd = lax.dot_general(wvd_ref[...], xn, _NT, preferred_element_type=F32).astype(BF16)
    for blk in range(TM_PROJ // TB):
        vtd_ref[blk] = vtd[:, blk * TB:(blk + 1) * TB]


def _chunk_order_matrix():
    dst = jnp.arange(TK, dtype=I32)
    src = (dst % SUBLANES) * KCH + dst // SUBLANES
    return (src[:, None] == jnp.arange(TK, dtype=I32)[None, :]).astype(BF16)


def _in_proj_tokens(x2d, gain, cs_a, w_a, w_k, w_vs_t, w_vd_t):
    t = x2d.shape[0]
    tm = TM_PROJ
    n_a = w_a.shape[1]
    const = lambda i: (0, 0)
    return pl.pallas_call(
        _inproj_tokens_kernel,
        grid=(t // tm,),
        in_specs=[
            pl.BlockSpec((tm, D_MODEL), lambda i: (i, 0)),
            pl.BlockSpec((1, D_MODEL), const),
            pl.BlockSpec((1, n_a), const),
            pl.BlockSpec((TK, TK), const),
            pl.BlockSpec((D_MODEL, n_a), const),
            pl.BlockSpec((D_MODEL, D_SB), const),
            pl.BlockSpec((D_SB, D_MODEL), const),
            pl.BlockSpec((D_DIFF, D_MODEL), const),
        ],
        out_specs=[
            pl.BlockSpec((tm, n_a), lambda i: (i, 0)),
            pl.BlockSpec((tm, D_SB), lambda i: (i, 0)),
            pl.BlockSpec((tm // TK, D_SB, TK), lambda i: (i, 0, 0)),
            pl.BlockSpec((tm // TB, D_DIFF, TB), lambda i: (i, 0, 0)),
        ],
        out_shape=[
            jax.ShapeDtypeStruct((t, n_a), BF16),
            jax.ShapeDtypeStruct((t, D_SB), BF16),
            jax.ShapeDtypeStruct((t // TK, D_SB, TK), BF16),
            jax.ShapeDtypeStruct((t // TB, D_DIFF, TB), BF16),
        ],
        compiler_params=_cparams(("parallel",)),
        name="in_proj",
    )(x2d, gain, cs_a, _chunk_order_matrix(), w_a, w_k, w_vs_t, w_vd_t)


def _bias_lookup(rel, rb_ref, h):
    n = jnp.abs(rel)
    n2 = n * n
    large = jnp.full(rel.shape, 8, I32)
    for k in range(1, 8):
        large = large + jnp.where(n2 >= (64 << k), 1, 0)
    bucket = jnp.where(rel > 0, N_BUCKETS // 2, 0) + jnp.where(n < 8, n, large)
    out = jnp.zeros(rel.shape, F32)
    for b in range(N_BUCKETS):
        out = jnp.where(bucket == b, rb_ref[b, h], out)
    return out


def _relbias_kernel(rb_ref, bt_ref, mb_ref):
    h = pl.program_id(0)
    krow = lax.broadcasted_iota(I32, (TB, TB), 0)
    qcol = lax.broadcasted_iota(I32, (TB, TB), 1)
    visible = (krow // CHUNK) <= (qcol // CHUNK)
    bt_ref[0, 0] = jnp.where(visible, _bias_lookup(krow - qcol, rb_ref, h), NEG_BIG)
    bt_ref[0, 1] = _bias_lookup(krow - qcol - TB, rb_ref, h)
    bt_ref[0, 2] = _bias_lookup(krow - qcol - 2 * TB, rb_ref, h)
    s = mb_ref.shape[2]
    mrow = lax.broadcasted_iota(I32, (N_META, s), 0)
    qpos = lax.broadcasted_iota(I32, (N_META, s), 1) + N_META
    mb_ref[0] = _bias_lookup(mrow - qpos, rb_ref, h)


def _rel_bias_tiles(rel_bias, s):
    return pl.pallas_call(
        _relbias_kernel,
        grid=(H_DIFF,),
        in_specs=[pl.BlockSpec(memory_space=pltpu.SMEM)],
        out_specs=[
            pl.BlockSpec((1, N_BIAS_TILES, TB, TB), lambda h: (h, 0, 0, 0)),
            pl.BlockSpec((1, N_META, s), lambda h: (h, 0, 0)),
        ],
        out_shape=[
            jax.ShapeDtypeStruct((H_DIFF, N_BIAS_TILES, TB, TB), F32),
            jax.ShapeDtypeStruct((H_DIFF, N_META, s), F32),
        ],
        compiler_params=_cparams(("arbitrary",)),
        name="rel_bias",
    )(rel_bias)


def _suffix_incl_sublanes(x):
    r = lax.broadcasted_iota(I32, x.shape, 0)
    for d in (1, 2, 4):
        shifted = pltpu.roll(x, SUBLANES - d, axis=0)
        x = x + jnp.where(r + d < SUBLANES, shifted, 0.0)
    return x


def _softplus(s):
    return jnp.maximum(s, 0.0) + jnp.log(1.0 + jnp.exp(-jnp.abs(s)))


def _head_half(qpair, half):
    lane = lax.broadcasted_iota(I32, qpair.shape, 1)
    keep = (lane >= HEAD_DIM * half) & (lane < HEAD_DIM * (half + 1))
    return jnp.where(keep, qpair, jnp.zeros_like(qpair))


def _sb_block(s, vt, carry, acc, valid):
    sp = _softplus(s)
    if valid is not None:
        sp = jnp.where(valid, sp, 0.0)
    run = jnp.zeros((SUBLANES, s.shape[1]), F32)
    parts = [None] * KCH
    for i in reversed(range(KCH)):
        run = run + sp[SUBLANES * i:SUBLANES * (i + 1), :]
        parts[i] = run
    incl = _suffix_incl_sublanes(run)
    base = (incl - run) + carry
    r_sum = jnp.concatenate([p + base for p in parts], axis=0)
    w = jnp.exp(s - r_sum)
    if valid is not None:
        w = jnp.where(valid, w, 0.0)
    acc = acc + jnp.dot(vt, w.astype(BF16), preferred_element_type=F32)
    return carry + incl[0:1, :], acc


def _sb_meta_block(s, vt, carry, acc):
    sp = _softplus(s)
    lo, hi = sp[0:SUBLANES, :], sp[SUBLANES:2 * SUBLANES, :]
    hi_incl = _suffix_incl_sublanes(hi)
    lo_incl = _suffix_incl_sublanes(lo) + hi_incl[0:1, :]
    r_sum = jnp.concatenate([lo_incl, hi_incl], axis=0) + carry
    w = jnp.exp(s - r_sum)
    return acc + jnp.dot(vt, w.astype(BF16), preferred_element_type=F32)


def _sb_kernel(q_ref, k_ref, vt_ref, mk_ref, mvt_ref, o_ref):
    qi = pl.program_id(2)
    qpair = q_ref[...]
    row = lax.broadcasted_iota(I32, (TK, TQ), 0)
    lane = lax.broadcasted_iota(I32, (TK, TQ), 1)
    key_off = (row % SUBLANES) * KCH + row // SUBLANES
    causal = key_off < lane
    qz = [_head_half(qpair, half) for half in range(2)]

    def scores(kb, half):
        start = pl.multiple_of(kb * TK, TK)
        return lax.dot_general(k_ref[pl.ds(start, TK), :], qz[half], _NT, preferred_element_type=F32)

    def values(kb, half):
        return vt_ref[kb, HEAD_DIM * half:HEAD_DIM * (half + 1), :]

    def alive(carry):
        return (jnp.min(carry) < SB_EXIT).astype(I32)

    has_prev = qi > 0
    kprev = jnp.maximum(qi - 1, 0)
    s_diag = [scores(qi, half) for half in range(2)]
    s_prev = [scores(kprev, half) for half in range(2)]
    state = []
    for half in range(2):
        carry = jnp.zeros((1, TQ), F32)
        acc = jnp.zeros((HEAD_DIM, TQ), F32)
        carry, acc = _sb_block(s_diag[half], values(qi, half), carry, acc, causal)
        carry2, acc2 = _sb_block(s_prev[half], values(kprev, half), carry, acc, None)
        state.append((jnp.where(has_prev, carry2, carry), jnp.where(has_prev, acc2, acc)))

    outs = []
    for half in range(2):
        carry, acc = state[half]

        def cond(st):
            return (st[0] >= 0) & (st[1] > 0)

        def body(st, half=half):
            kb, _, carry, acc = st
            carry, acc = _sb_block(scores(kb, half), values(kb, half), carry, acc, None)
            return kb - 1, alive(carry), carry, acc

        _, live, carry, acc = lax.while_loop(cond, body, (qi - 2, alive(carry), carry, acc))

        def meta(acc, half=half, carry=carry):
            sm = lax.dot_general(mk_ref[...], qz[half], _NT, preferred_element_type=F32)
            return _sb_meta_block(sm, mvt_ref[half], carry, acc)

        outs.append(lax.cond(live > 0, meta, lambda a: a, acc))
    o_ref[...] = jnp.concatenate(outs, axis=0).T.astype(BF16)


def _sb_attention(a_proj, k_perm, vt_perm, mproj, mvt, b, s):
    nq = s // TQ
    nkb = s // TK
    mkcol0 = D_SB // LANES
    return pl.pallas_call(
        _sb_kernel,
        grid=(b, H_SB // 2, nq),
        in_specs=[
            pl.BlockSpec((TQ, LANES), lambda bi, p, qi: (bi * nq + qi, p)),
            pl.BlockSpec((s, LANES), lambda bi, p, qi: (bi, p)),
            pl.BlockSpec((nkb, 2 * HEAD_DIM, TK), lambda bi, p, qi: (bi, p, 0)),
            pl.BlockSpec((N_META, LANES), lambda bi, p, qi: (0, mkcol0 + p)),
            pl.BlockSpec((2, HEAD_DIM, N_META), lambda bi, p, qi: (p, 0, 0)),
        ],
        out_specs=pl.BlockSpec((TQ, LANES), lambda bi, p, qi: (bi * nq + qi, p)),
        out_shape=jax.ShapeDtypeStruct((b * s, D_SB), BF16),
        compiler_params=_cparams(("parallel", "parallel", "arbitrary")),
        name="sb_attn",
    )(a_proj, k_perm, vt_perm, mproj, mvt)


def _with_ones_rows(vt):
    r = lax.broadcasted_iota(I32, (L_ROWS, vt.shape[1]), 0)
    ones = jnp.where(r == 0, 1.0, 0.0).astype(vt.dtype)
    return jnp.concatenate([vt, ones], axis=0)


def _df_update(s, vt_ext, st):
    m, acc = st
    m_new = jnp.maximum(m, jnp.max(s, axis=0, keepdims=True))
    alpha = jnp.exp(m - m_new)
    p = jnp.exp(s - m_new)
    acc = alpha * acc + jnp.dot(vt_ext, p.astype(BF16), preferred_element_type=F32)
    return m_new, acc


def _df_kernel(q_ref, k_ref, vt_ref, mk_ref, mvt_ref, bt_ref, mb_ref,
               lq1_ref, lk1_ref, lq2_ref, lk2_ref, gain_ref, o_ref):
    qi = pl.program_id(2)
    qpair = q_ref[...]
    qz = [_head_half(qpair, 0), _head_half(qpair, 1)]
    dv = 2 * HEAD_DIM

    def scores(d):
        kb = jnp.maximum(qi - d, 0)
        kblk = k_ref[pl.ds(pl.multiple_of(kb * TB, TB), TB), :]
        bias = bt_ref[jnp.minimum(d, N_BIAS_TILES - 1)]
        return tuple(lax.dot_general(kblk, qz[i], _NT, preferred_element_type=F32) + bias
                     for i in range(2))

    def init():
        return (jnp.full((1, TB), -jnp.inf, F32), jnp.zeros((dv + L_ROWS, TB), F32))

    def body(d, carry):
        st, s_cur = carry
        s_nxt = scores(d + 1)
        vt = _with_ones_rows(vt_ref[qi - d])
        return tuple(_df_update(s_cur[i], vt, st[i]) for i in range(2)), s_nxt

    st, _ = lax.fori_loop(0, qi + 1, body, ((init(), init()), scores(0)))
    mk = mk_ref[...]
    mvt = _with_ones_rows(mvt_ref[...])
    mb = mb_ref[...]
    st = tuple(
        _df_update(lax.dot_general(mk, qz[i], _NT, preferred_element_type=F32) + mb, mvt, st[i])
        for i in range(2))

    lam = (jnp.exp(jnp.sum(lq1_ref[...] * lk1_ref[...], axis=-1, keepdims=True))
           - jnp.exp(jnp.sum(lq2_ref[...] * lk2_ref[...], axis=-1, keepdims=True))
           + LAMBDA_INIT)
    acc1, acc2 = st[0][1], st[1][1]
    o = acc1[:dv] / acc1[dv:dv + 1] - lam * (acc2[:dv] / acc2[dv:dv + 1])
    ms = jnp.mean(o * o, axis=0, keepdims=True)
    y = o * lax.rsqrt(ms + SUBLN_EPS) * gain_ref[...] * (1.0 - LAMBDA_INIT)
    o_ref[...] = y.T.astype(BF16)


def _diff_attention(a_proj, vt, mproj, mvt, bias_tiles, meta_bias,
                    lq1, lk1, lq2, lk2, gain_col, b, s):
    nq = s // TB
    nkb = s // TB
    qcol0 = D_SB // LANES
    kcol0 = (D_SB + D_DIFF) // LANES
    mkcol0 = (3 * D_SB + D_DIFF) // LANES
    lam_spec = pl.BlockSpec((1, HEAD_DIM), lambda bi, h, qi: (0, 0))
    return pl.pallas_call(
        _df_kernel,
        grid=(b, H_DIFF, nq),
        in_specs=[
            pl.BlockSpec((TB, LANES), lambda bi, h, qi: (bi * nq + qi, qcol0 + h)),
            pl.BlockSpec((s, LANES), lambda bi, h, qi: (bi, kcol0 + h)),
            pl.BlockSpec((nkb, 2 * HEAD_DIM, TB), lambda bi, h, qi: (bi, h, 0)),
            pl.BlockSpec((N_META, LANES), lambda bi, h, qi: (0, mkcol0 + h)),
            pl.BlockSpec((None, 2 * HEAD_DIM, N_META), lambda bi, h, qi: (h, 0, 0)),
            pl.BlockSpec((None, N_BIAS_TILES, TB, TB), lambda bi, h, qi: (h, 0, 0, 0)),
            pl.BlockSpec((None, N_META, TB), lambda bi, h, qi: (h, 0, qi)),
            lam_spec, lam_spec, lam_spec, lam_spec,
            pl.BlockSpec((2 * HEAD_DIM, 1), lambda bi, h, qi: (0, 0)),
        ],
        out_specs=pl.BlockSpec((TB, LANES), lambda bi, h, qi: (bi * nq + qi, h)),
        out_shape=jax.ShapeDtypeStruct((b * s, D_DIFF), BF16),
        compiler_params=_cparams(("parallel", "parallel", "arbitrary")),
        name="diff_attn",
    )(a_proj, a_proj, vt, mproj, mvt, bias_tiles, meta_bias, lq1, lk1, lq2, lk2, gain_col)


def _outrouter_kernel(x_ref, osb_ref, odf_ref, wo_ref, g_ref, wrt_ref, br_ref, tri_ref,
                      h1_ref, xn_ref, idx_ref, gate_ref, rank_ref, cnt_ref, carry_ref):
    @pl.when(pl.program_id(0) == 0)
    def _():
        carry_ref[...] = jnp.zeros_like(carry_ref)

    tm = x_ref.shape[0]
    mix = jnp.concatenate([osb_ref[...], odf_ref[...]], axis=1)
    h1 = x_ref[...] + jnp.dot(mix, wo_ref[...], preferred_element_type=F32)
    h1_ref[...] = h1
    ms = jnp.mean(h1 * h1, axis=-1, keepdims=True)
    xn = h1 * lax.rsqrt(ms + NORM_EPS) * g_ref[...]
    for c in range(ROW_TILE):
        xn_ref[pl.ds(c, tm, stride=ROW_TILE), :] = xn[:, LANES * c:LANES * (c + 1)]

    logits = lax.dot_general(wrt_ref[...], xn.astype(BF16), _NT, preferred_element_type=F32) + br_ref[...]
    e_iota = lax.broadcasted_iota(I32, logits.shape, 0)
    work = logits
    vals, idxs = [], []
    for _ in range(TOP_K):
        m = jnp.max(work, axis=0, keepdims=True)
        ik = jnp.min(jnp.where(work == m, e_iota, N_EXPERTS), axis=0, keepdims=True)
        vals.append(m)
        idxs.append(ik)
        work = jnp.where(e_iota == ik, -jnp.inf, work)
    exps = [jnp.exp(v - vals[0]) for v in vals]
    den = exps[0] + exps[1] + exps[2] + exps[3]
    onehot = jnp.zeros(logits.shape, F32)
    for ik in idxs:
        onehot = onehot + jnp.where(e_iota == ik, 1.0, 0.0)
    prefix = jnp.dot(onehot.astype(BF16), tri_ref[...], preferred_element_type=F32)
    pos = prefix + carry_ref[:, 0:1]
    for k in range(TOP_K):
        idx_ref[k:k + 1, :] = idxs[k]
        gate_ref[k:k + 1, :] = exps[k] / den
        rank_ref[k:k + 1, :] = jnp.sum(jnp.where(e_iota == idxs[k], pos, 0.0), axis=0,
                                       keepdims=True).astype(I32)
    carry_ref[...] = carry_ref[...] + jnp.sum(onehot, axis=1, keepdims=True)
    cnt_ref[...] = carry_ref[...]


def _out_router(x2d, o_sb, o_df, wo_bf16, gain, wr_t, br_col, tri, tm):
    t = x2d.shape[0]
    const = lambda i: (0, 0)
    return pl.pallas_call(
        _outrouter_kernel,
        grid=(t // tm,),
        in_specs=[
            pl.BlockSpec((tm, D_MODEL), lambda i: (i, 0)),
            pl.BlockSpec((tm, D_SB), lambda i: (i, 0)),
            pl.BlockSpec((tm, D_DIFF), lambda i: (i, 0)),
            pl.BlockSpec((D_SB + D_DIFF, D_MODEL), const),
            pl.BlockSpec((1, D_MODEL), const),
            pl.BlockSpec((N_EXPERTS, D_MODEL), const),
            pl.BlockSpec((N_EXPERTS, 1), const),
            pl.BlockSpec((tm, tm), const),
        ],
        out_specs=[
            pl.BlockSpec((tm, D_MODEL), lambda i: (i, 0)),
            pl.BlockSpec((tm * ROW_TILE, LANES), lambda i: (i, 0)),
            pl.BlockSpec((TOP_K, tm), lambda i: (0, i)),
            pl.BlockSpec((TOP_K, tm), lambda i: (0, i)),
            pl.BlockSpec((TOP_K, tm), lambda i: (0, i)),
            pl.BlockSpec((N_EXPERTS, LANES), const),
        ],
        out_shape=[
            jax.ShapeDtypeStruct((t, D_MODEL), F32),
            jax.ShapeDtypeStruct((t * ROW_TILE, LANES), F32),
            jax.ShapeDtypeStruct((TOP_K, t), I32),
            jax.ShapeDtypeStruct((TOP_K, t), F32),
            jax.ShapeDtypeStruct((TOP_K, t), I32),
            jax.ShapeDtypeStruct((N_EXPERTS, LANES), F32),
        ],
        scratch_shapes=[pltpu.VMEM((N_EXPERTS, LANES), F32)],
        compiler_params=_cparams(("arbitrary",)),
        name="out_router",
    )(x2d, o_sb, o_df, wo_bf16, gain, wr_t, br_col, tri)


def _dest_kernel(ps_ref, idx_ref, rank_ref, dest_ref):
    idx = idx_ref[...]
    off = jnp.zeros(idx.shape, I32)
    for e in range(N_EXPERTS):
        off = jnp.where(idx == e, ps_ref[e], off)
    dest_ref[...] = rank_ref[...] + off


def _route_dest(pad_start, idx, rank):
    t = idx.shape[1]
    tt = min(t, 8192)
    grid_spec = pltpu.PrefetchScalarGridSpec(
        num_scalar_prefetch=1,
        grid=(t // tt,),
        in_specs=[pl.BlockSpec((TOP_K, tt), lambda i, ps: (0, i)),
                  pl.BlockSpec((TOP_K, tt), lambda i, ps: (0, i))],
        out_specs=pl.BlockSpec((TOP_K, tt), lambda i, ps: (0, i)),
    )
    return pl.pallas_call(
        _dest_kernel,
        grid_spec=grid_spec,
        out_shape=jax.ShapeDtypeStruct((TOP_K, t), I32),
        compiler_params=_cparams(("parallel",)),
        name="route_dest",
    )(pad_start, idx, rank)


def _zero_fill_padding(pe_ref, nu_ref, xs_hbm, zbuf, zsem, first_tail_block):
    zbuf[...] = jnp.zeros_like(zbuf)
    conds, copies = [], []
    for e in range(N_EXPERTS):
        prev_end = pe_ref[e - 1] if e > 0 else 0
        conds.append(pe_ref[e] > prev_end)
        start = jnp.maximum(pe_ref[e] - TM_FFN, 0)
        copies.append(pltpu.make_async_copy(zbuf, xs_hbm.at[pl.ds(start, TM_FFN)], zsem))
    for j in range(N_EXPERTS):
        blk = first_tail_block + j
        conds.append(blk >= nu_ref[0])
        copies.append(pltpu.make_async_copy(zbuf, xs_hbm.at[pl.ds(blk * TM_FFN, TM_FFN)], zsem))
    for cond, c in zip(conds, copies):
        pl.when(cond)(c.start)
    for cond, c in zip(conds, copies):
        pl.when(cond)(c.wait)


def _dispatch_kernel(pe_ref, nu_ref, dest_ref, xn_ref, xs_hbm, zbuf, sem, zsem):
    step = pl.program_id(0)
    first_tail_block = xs_hbm.shape[0] // TM_FFN - N_EXPERTS

    @pl.when(step == 0)
    def _():
        _zero_fill_padding(pe_ref, nu_ref, xs_hbm, zbuf, zsem, first_tail_block)

    def issue(r, _):
        for k in range(TOP_K):
            pltpu.make_async_copy(xn_ref.at[r], xs_hbm.at[dest_ref[k, r]], sem).start(priority=k % 2)
        return 0

    lax.fori_loop(0, TD, issue, 0, unroll=8)

    def drain(_, carry):
        for _ in range(DRAIN_UNROLL):
            pltpu.make_async_copy(xn_ref.at[0], xs_hbm.at[0], sem).wait()
        return carry

    lax.fori_loop(0, TD * TOP_K // DRAIN_UNROLL, drain, 0)


def _dispatch(pad_end, n_used, dest, xn3, a_pad):
    t = dest.shape[1]
    grid_spec = pltpu.PrefetchScalarGridSpec(
        num_scalar_prefetch=2,
        grid=(t // TD,),
        in_specs=[
            pl.BlockSpec((TOP_K, TD), lambda i, pe, nu: (0, i), memory_space=pltpu.SMEM),
            pl.BlockSpec((TD, ROW_TILE, LANES), lambda i, pe, nu: (i, 0, 0)),
        ],
        out_specs=pl.BlockSpec(memory_space=pl.ANY),
        scratch_shapes=[
            pltpu.VMEM((TM_FFN, ROW_TILE, LANES), F32),
            pltpu.SemaphoreType.DMA(()),
            pltpu.SemaphoreType.DMA(()),
        ],
    )
    return pl.pallas_call(
        _dispatch_kernel,
        grid_spec=grid_spec,
        out_shape=jax.ShapeDtypeStruct((a_pad, ROW_TILE, LANES), F32),
        compiler_params=_cparams(("arbitrary",)),
        name="dispatch",
    )(pad_end, n_used, dest, xn3)


def _ffn_kernel(be_ref, nu_ref, xs_ref, w1_ref, b1_ref, w2_ref, b2_ref, y_ref):
    i = pl.program_id(0)

    @pl.when(i >= nu_ref[0])
    def _():
        y_ref[...] = jnp.zeros_like(y_ref)

    @pl.when(i < nu_ref[0])
    def _():
        x = jnp.concatenate(
            [xs_ref[pl.ds(c, TM_FFN, stride=ROW_TILE), :] for c in range(ROW_TILE)], axis=1).astype(BF16)
        hu = jnp.dot(x, w1_ref[...], preferred_element_type=F32) + b1_ref[...]
        gate = jnp.minimum(hu[:, :D_FF], SWIGLU_LIMIT)
        lin = jnp.clip(hu[:, D_FF:], -SWIGLU_LIMIT, SWIGLU_LIMIT)
        act = gate * jax.nn.sigmoid(SWIGLU_ALPHA * gate) * (lin + 1.0)
        y = jnp.dot(act.astype(BF16), w2_ref[...], preferred_element_type=F32) + b2_ref[...]
        for c in range(ROW_TILE):
            y_ref[pl.ds(c, TM_FFN, stride=ROW_TILE), :] = y[:, LANES * c:LANES * (c + 1)]


def _expert_ffn(blk_e, n_used, xs2d, w1, b1, w2, b2, n_blocks):
    rows = TM_FFN * ROW_TILE

    def xmap(i, be, nu):
        return (jnp.minimum(i, nu[0] - 1), 0)

    def wmap(i, be, nu):
        return (be[i], 0, 0)

    grid_spec = pltpu.PrefetchScalarGridSpec(
        num_scalar_prefetch=2,
        grid=(n_blocks,),
        in_specs=[
            pl.BlockSpec((rows, LANES), xmap),
            pl.BlockSpec((None, D_MODEL, 2 * D_FF), wmap),
            pl.BlockSpec((None, 1, 2 * D_FF), wmap),
            pl.BlockSpec((None, D_FF, D_MODEL), wmap),
            pl.BlockSpec((None, 1, D_MODEL), wmap),
        ],
        out_specs=pl.BlockSpec((rows, LANES), lambda i, be, nu: (i, 0)),
    )
    return pl.pallas_call(
        _ffn_kernel,
        grid_spec=grid_spec,
        out_shape=jax.ShapeDtypeStruct(xs2d.shape, F32),
        compiler_params=_cparams(("arbitrary",)),
        name="expert_ffn",
    )(blk_e, n_used, xs2d, w1, b1, w2, b2)


def _combine_gather(dest_ref, y_hbm, buf, sem, slot):
    def issue(r, carry):
        for k in range(TOP_K):
            row0 = pl.multiple_of(dest_ref[k, r] * ROW_TILE, ROW_TILE)
            dst0 = pl.multiple_of(((slot * TOP_K + k) * TD + r) * ROW_TILE, ROW_TILE)
            pltpu.make_async_copy(
                y_hbm.at[pl.ds(row0, ROW_TILE)], buf.at[pl.ds(dst0, ROW_TILE)],
                sem.at[slot]).start(priority=k % 2)
        return carry

    lax.fori_loop(0, TD, issue, 0, unroll=8)


def _combine_drain(y_hbm, buf, sem, slot):
    def drain(_, carry):
        for _ in range(DRAIN_UNROLL):
            pltpu.make_async_copy(
                y_hbm.at[pl.ds(0, ROW_TILE)], buf.at[pl.ds(0, ROW_TILE)], sem.at[slot]).wait()
        return carry

    lax.fori_loop(0, TD * TOP_K // DRAIN_UNROLL, drain, 0)


def _combine_kernel(dcur_ref, dnext_ref, y_hbm, gates_ref, h1_ref, g_ref, o_ref, buf, sem):
    i = pl.program_id(0)
    n = pl.num_programs(0)

    @pl.when(i == 0)
    def _():
        _combine_gather(dcur_ref, y_hbm, buf, sem, 0)

    for slot in range(2):
        @pl.when((i % 2 == slot) & (i + 1 < n))
        def _(slot=slot):
            _combine_gather(dnext_ref, y_hbm, buf, sem, 1 - slot)

    for slot in range(2):
        @pl.when(i % 2 == slot)
        def _(slot=slot):
            _combine_drain(y_hbm, buf, sem, slot)
            acc = h1_ref[...]
            gates = gates_ref[...]
            for k in range(TOP_K):
                base = (slot * TOP_K + k) * TD * ROW_TILE
                yk = jnp.concatenate(
                    [buf[pl.ds(base + c, TD, stride=ROW_TILE), :] for c in range(ROW_TILE)], axis=1)
                acc = acc + yk * gates[:, k:k + 1]
            ms = jnp.mean(acc * acc, axis=-1, keepdims=True)
            o_ref[...] = acc * lax.rsqrt(ms + NORM_EPS) * g_ref[...]


def _combine(dest, y2d, gates_t, h1, gain):
    t = h1.shape[0]
    n = t // TD
    return pl.pallas_call(
        _combine_kernel,
        grid=(n,),
        in_specs=[
            pl.BlockSpec((TOP_K, TD), lambda i: (0, i), memory_space=pltpu.SMEM),
            pl.BlockSpec((TOP_K, TD), lambda i: (0, jnp.minimum(i + 1, n - 1)), memory_space=pltpu.SMEM),
            pl.BlockSpec(memory_space=pl.ANY),
            pl.BlockSpec((TD, TOP_K), lambda i: (i, 0)),
            pl.BlockSpec((TD, D_MODEL), lambda i: (i, 0)),
            pl.BlockSpec((1, D_MODEL), lambda i: (0, 0)),
        ],
        out_specs=pl.BlockSpec((TD, D_MODEL), lambda i: (i, 0)),
        out_shape=jax.ShapeDtypeStruct((t, D_MODEL), F32),
        scratch_shapes=[
            pltpu.VMEM((2 * TOP_K * TD * ROW_TILE, LANES), F32),
            pltpu.SemaphoreType.DMA((2,)),
        ],
        compiler_params=_cparams(("arbitrary",)),
        name="combine",
    )(dest, dest, y2d, gates_t, h1, gain)


def _block_plan(counts, n_blocks):
    padded = (counts + TM_FFN - 1) // TM_FFN * TM_FFN
    pad_end = jnp.cumsum(padded)
    pad_start = pad_end - padded
    blk_start = jnp.arange(n_blocks, dtype=I32) * TM_FFN
    blk_e = jnp.sum((pad_end[None, :] <= blk_start[:, None]).astype(I32), axis=1)
    blk_e = jnp.minimum(blk_e, N_EXPERTS - 1)
    n_used = (pad_end[-1:] // TM_FFN).astype(I32)
    return pad_start.astype(I32), pad_end.astype(I32), blk_e, n_used


@jax.jit
def _forward(x, meta_tokens, rel_bias, attn_norm, w_in, w_out, lambda_q1, lambda_k1, lambda_q2,
             lambda_k2, subln_gain, ffn_norm, w_router, b_router, w1, b1, w2, b2, final_norm):
    b, s, _ = x.shape
    t = b * s
    assert TQ == TK and TM_PROJ % TK == 0 and TM_PROJ % TB == 0
    assert s % TQ == 0 and s % TB == 0 and t % TM_PROJ == 0 and t % TD == 0
    x2d = x.reshape(t, D_MODEL)

    scale = HEAD_DIM ** -0.5
    c_sbk, c_sbv, c_dfq, c_dfk, c_dfv = D_SB, 2 * D_SB, 3 * D_SB, 3 * D_SB + D_DIFF, 3 * D_SB + 2 * D_DIFF
    w_in_b = w_in[0].astype(BF16)
    g_attn = attn_norm[0][None, :]
    colscale = jnp.ones((D_IN,), F32).at[0:D_SB].set(scale).at[c_dfq:c_dfk].set(scale)[None, :]
    w_a = jnp.concatenate([w_in_b[:, :c_sbk], w_in_b[:, c_dfq:c_dfv]], axis=1)
    cs_a = jnp.concatenate([colscale[:, :c_sbk], colscale[:, c_dfq:c_dfv]], axis=1)
    a_proj, k_perm, vt_sb, vt_df = _in_proj_tokens(
        x2d, g_attn, cs_a, w_a, w_in_b[:, c_sbk:c_sbv], w_in_b[:, c_sbv:c_dfq].T, w_in_b[:, c_dfv:].T)
    mproj = _in_proj_meta(meta_tokens, g_attn, colscale, w_in_b)
    mvt_sb = mproj[:, c_sbv:c_dfq].reshape(N_META, H_SB, HEAD_DIM).transpose(1, 2, 0)
    mvt_df = mproj[:, c_dfv:].reshape(N_META, H_DIFF, 2 * HEAD_DIM).transpose(1, 2, 0)

    o_sb = _sb_attention(a_proj, k_perm, vt_sb, mproj, mvt_sb, b, s)

    bias_tiles, meta_bias = _rel_bias_tiles(rel_bias, s)
    o_df = _diff_attention(a_proj, vt_df, mproj, mvt_df, bias_tiles, meta_bias,
                           lambda_q1, lambda_k1, lambda_q2, lambda_k2,
                           subln_gain[0][:, None], b, s)

    tri = jnp.triu(jnp.ones((TM_PROJ, TM_PROJ), BF16), k=1)
    h1, xn2d, idx, gates, rank, cnt = _out_router(
        x2d, o_sb, o_df, w_out[0].astype(BF16), ffn_norm[0][None, :],
        w_router[0].T.astype(BF16), b_router[0][:, None], tri, TM_PROJ)

    a = t * TOP_K
    n_blocks = a // TM_FFN + N_EXPERTS
    a_pad = n_blocks * TM_FFN
    counts = cnt[:, 0].astype(I32)
    pad_start, pad_end, blk_e, n_used = _block_plan(counts, n_blocks)
    dest = _route_dest(pad_start, idx, rank)

    xs = _dispatch(pad_end, n_used, dest, xn2d.reshape(t, ROW_TILE, LANES), a_pad)
    y2d = _expert_ffn(blk_e, n_used, xs.reshape(a_pad * ROW_TILE, LANES),
                      w1[0].astype(BF16), b1[0][:, None, :], w2[0].astype(BF16), b2[0][:, None, :],
                      n_blocks)
    out = _combine(dest, y2d, gates.T, h1, final_norm[None, :])
    return out.reshape(b, s, D_MODEL)


def kernel(x, meta_tokens, rel_bias, attn_norm, w_in, w_out, lambda_q1, lambda_k1, lambda_q2,
           lambda_k2, subln_gain, ffn_norm, w_router, b_router, w1, b1, w2, b2, final_norm):
    return _forward(x, meta_tokens, rel_bias, attn_norm, w_in, w_out, lambda_q1, lambda_k1,
                    lambda_q2, lambda_k2, subln_gain, ffn_norm, w_router, b_router, w1, b1, w2, b2,
                    final_norm)
```

```python
import functools
import math

import jax
import jax.numpy as jnp
from jax import lax
from jax.experimental import pallas as pl
from jax.experimental.pallas import tpu as pltpu

D_MODEL = 1024
N_META = 16
CHUNK = 64
HEAD_DIM = 64
H_SB = 8
H_DIFF = 4
D_SB = H_SB * HEAD_DIM
D_DIFF = H_DIFF * 2 * HEAD_DIM
D_IN = 3 * D_SB + 3 * D_DIFF
N_BUCKETS = 32
N_EXPERTS = 32
TOP_K = 4
D_FF = D_MODEL
SWIGLU_ALPHA = 1.702
SWIGLU_LIMIT = 7.0
NORM_EPS = 1e-6
SUBLN_EPS = 1e-5
NEG_BIG = -1e30
LAMBDA_INIT = 0.8 - 0.6 * math.exp(-0.3 * 0)

LANES = 128
SUBLANES = 8
ROW_TILE = D_MODEL // LANES
VMEM_LIMIT = 56 * 1024 * 1024

TM_PROJ = 512
TQ = 256
TK = 256
KCH = TK // SUBLANES
TB = 512
L_ROWS = 16
TM_FFN = 512
TD = 256
DRAIN_UNROLL = 128
N_BIAS_TILES = 3
SB_EXIT = 104.0

F32 = jnp.float32
BF16 = jnp.bfloat16
I32 = jnp.int32

_NT = (((1,), (1,)), ((), ()))


def _cparams(sem, vmem=VMEM_LIMIT):
    return pltpu.CompilerParams(dimension_semantics=sem, vmem_limit_bytes=vmem)


def _inproj_kernel(x_ref, g_ref, cs_ref, w_ref, o_ref):
    x = x_ref[...]
    ms = jnp.mean(x * x, axis=-1, keepdims=True)
    xn = (x * lax.rsqrt(ms + NORM_EPS) * g_ref[...]).astype(BF16)
    y = jnp.dot(xn, w_ref[...], preferred_element_type=F32)
    o_ref[...] = (y * cs_ref[...]).astype(BF16)


def _in_proj_meta(x2d, gain, colscale, w_bf16):
    t = x2d.shape[0]
    return pl.pallas_call(
        _inproj_kernel,
        grid=(1,),
        in_specs=[
            pl.BlockSpec((t, D_MODEL), lambda i: (0, 0)),
            pl.BlockSpec((1, D_MODEL), lambda i: (0, 0)),
            pl.BlockSpec((1, D_IN), lambda i: (0, 0)),
            pl.BlockSpec((D_MODEL, D_IN), lambda i: (0, 0)),
        ],
        out_specs=pl.BlockSpec((t, D_IN), lambda i: (0, 0)),
        out_shape=jax.ShapeDtypeStruct((t, D_IN), BF16),
        compiler_params=_cparams(("arbitrary",)),
        name="in_proj_meta",
    )(x2d, gain, colscale, w_bf16)


def _rms_bf16(x, g):
    ms = jnp.mean(x * x, axis=-1, keepdims=True)
    return (x * lax.rsqrt(ms + NORM_EPS) * g).astype(BF16)


def _inproj_tokens_kernel(x_ref, g_ref, cs_ref, perm_ref, wa_ref, wk_ref, wvs_ref, wvd_ref,
                          a_ref, kp_ref, vts_ref, vtd_ref):
    xn = _rms_bf16(x_ref[...], g_ref[...])
    xnp = jnp.concatenate(
        [jnp.dot(perm_ref[...], xn[blk * TK:(blk + 1) * TK], preferred_element_type=F32)
         for blk in range(TM_PROJ // TK)], axis=0).astype(BF16)
    a_ref[...] = (jnp.dot(xn, wa_ref[...], preferred_element_type=F32) * cs_ref[...]).astype(BF16)
    kp_ref[...] = jnp.dot(xnp, wk_ref[...], preferred_element_type=F32).astype(BF16)
    vts = lax.dot_general(wvs_ref[...], xnp, _NT, preferred_element_type=F32).astype(BF16)
    for blk in range(TM_PROJ // TK):
        vts_ref[blk] = vts[:, blk * TK:(blk + 1) * TK]
    vtd = lax.dot_general(wvd_ref[...], xn, _NT, preferred_element_type=F32).astype(BF16)
    for blk in range(TM_PROJ // TB):
        vtd_ref[blk] = vtd[:, blk * TB:(blk + 1) * TB]


def _chunk_order_matrix():
    dst = jnp.arange(TK, dtype=I32)
    src = (dst % SUBLANES) * KCH + dst // SUBLANES
    return (src[:, None] == jnp.arange(TK, dtype=I32)[None, :]).astype(BF16)


def _in_proj_tokens(x2d, gain, cs_a, w_a, w_k, w_vs_t, w_vd_t):
    t = x2d.shape[0]
    tm = TM_PROJ
    n_a = w_a.shape[1]
    const = lambda i: (0, 0)
    return pl.pallas_call(
        _inproj_tokens_kernel,
        grid=(t // tm,),
        in_specs=[
            pl.BlockSpec((tm, D_MODEL), lambda i: (i, 0)),
            pl.BlockSpec((1, D_MODEL), const),
            pl.BlockSpec((1, n_a), const),
            pl.BlockSpec((TK, TK), const),
            pl.BlockSpec((D_MODEL, n_a), const),
            pl.BlockSpec((D_MODEL, D_SB), const),
            pl.BlockSpec((D_SB, D_MODEL), const),
            pl.BlockSpec((D_DIFF, D_MODEL), const),
        ],
        out_specs=[
            pl.BlockSpec((tm, n_a), lambda i: (i, 0)),
            pl.BlockSpec((tm, D_SB), lambda i: (i, 0)),
            pl.BlockSpec((tm // TK, D_SB, TK), lambda i: (i, 0, 0)),
            pl.BlockSpec((tm // TB, D_DIFF, TB), lambda i: (i, 0, 0)),
        ],
        out_shape=[
            jax.ShapeDtypeStruct((t, n_a), BF16),
            jax.ShapeDtypeStruct((t, D_SB), BF16),
            jax.ShapeDtypeStruct((t // TK, D_SB, TK), BF16),
            jax.ShapeDtypeStruct((t // TB, D_DIFF, TB), BF16),
        ],
        compiler_params=_cparams(("parallel",)),
        name="in_proj",
    )(x2d, gain, cs_a, _chunk_order_matrix(), w_a, w_k, w_vs_t, w_vd_t)


def _bias_lookup(rel, rb_ref, h):
    n = jnp.abs(rel)
    n2 = n * n
    large = jnp.full(rel.shape, 8, I32)
    for k in range(1, 8):
        large = large + jnp.where(n2 >= (64 << k), 1, 0)
    bucket = jnp.where(rel > 0, N_BUCKETS // 2, 0) + jnp.where(n < 8, n, large)
    out = jnp.zeros(rel.shape, F32)
    for b in range(N_BUCKETS):
        out = jnp.where(bucket == b, rb_ref[b, h], out)
    return out


def _relbias_kernel(rb_ref, bt_ref, mb_ref):
    h = pl.program_id(0)
    krow = lax.broadcasted_iota(I32, (TB, TB), 0)
    qcol = lax.broadcasted_iota(I32, (TB, TB), 1)
    visible = (krow // CHUNK) <= (qcol // CHUNK)
    bt_ref[0, 0] = jnp.where(visible, _bias_lookup(krow - qcol, rb_ref, h), NEG_BIG)
    bt_ref[0, 1] = _bias_lookup(krow - qcol - TB, rb_ref, h)
    bt_ref[0, 2] = _bias_lookup(krow - qcol - 2 * TB, rb_ref, h)
    s = mb_ref.shape[2]
    mrow = lax.broadcasted_iota(I32, (N_META, s), 0)
    qpos = lax.broadcasted_iota(I32, (N_META, s), 1) + N_META
    mb_ref[0] = _bias_lookup(mrow - qpos, rb_ref, h)


def _rel_bias_tiles(rel_bias, s):
    return pl.pallas_call(
        _relbias_kernel,
        grid=(H_DIFF,),
        in_specs=[pl.BlockSpec(memory_space=pltpu.SMEM)],
        out_specs=[
            pl.BlockSpec((1, N_BIAS_TILES, TB, TB), lambda h: (h, 0, 0, 0)),
            pl.BlockSpec((1, N_META, s), lambda h: (h, 0, 0)),
        ],
        out_shape=[
            jax.ShapeDtypeStruct((H_DIFF, N_BIAS_TILES, TB, TB), F32),
            jax.ShapeDtypeStruct((H_DIFF, N_META, s), F32),
        ],
        compiler_params=_cparams(("arbitrary",)),
        name="rel_bias",
    )(rel_bias)


def _suffix_incl_sublanes(x):
    r = lax.broadcasted_iota(I32, x.shape, 0)
    for d in (1, 2, 4):
        shifted = pltpu.roll(x, SUBLANES - d, axis=0)
        x = x + jnp.where(r + d < SUBLANES, shifted, 0.0)
    return x


def _softplus(s):
    return jnp.maximum(s, 0.0) + jnp.log(1.0 + jnp.exp(-jnp.abs(s)))


def _head_half(qpair, half):
    lane = lax.broadcasted_iota(I32, qpair.shape, 1)
    keep = (lane >= HEAD_DIM * half) & (lane < HEAD_DIM * (half + 1))
    return jnp.where(keep, qpair, jnp.zeros_like(qpair))


def _sb_block(s, vt, carry, acc, valid):
    sp = _softplus(s)
    if valid is not None:
        sp = jnp.where(valid, sp, 0.0)
    run = jnp.zeros((SUBLANES, s.shape[1]), F32)
    parts = [None] * KCH
    for i in reversed(range(KCH)):
        run = run + sp[SUBLANES * i:SUBLANES * (i + 1), :]
        parts[i] = run
    incl = _suffix_incl_sublanes(run)
    base = (incl - run) + carry
    r_sum = jnp.concatenate([p + base for p in parts], axis=0)
    w = jnp.exp(s - r_sum)
    if valid is not None:
        w = jnp.where(valid, w, 0.0)
    acc = acc + jnp.dot(vt, w.astype(BF16), preferred_element_type=F32)
    return carry + incl[0:1, :], acc


def _sb_meta_block(s, vt, carry, acc):
    sp = _softplus(s)
    lo, hi = sp[0:SUBLANES, :], sp[SUBLANES:2 * SUBLANES, :]
    hi_incl = _suffix_incl_sublanes(hi)
    lo_incl = _suffix_incl_sublanes(lo) + hi_incl[0:1, :]
    r_sum = jnp.concatenate([lo_incl, hi_incl], axis=0) + carry
    w = jnp.exp(s - r_sum)
    return acc + jnp.dot(vt, w.astype(BF16), preferred_element_type=F32)


def _sb_kernel(q_ref, k_ref, vt_ref, mk_ref, mvt_ref, o_ref):
    qi = pl.program_id(2)
    qpair = q_ref[...]
    row = lax.broadcasted_iota(I32, (TK, TQ), 0)
    lane = lax.broadcasted_iota(I32, (TK, TQ), 1)
    key_off = (row % SUBLANES) * KCH + row // SUBLANES
    causal = key_off < lane
    qz = [_head_half(qpair, half) for half in range(2)]

    def scores(kb, half):
        start = pl.multiple_of(kb * TK, TK)
        return lax.dot_general(k_ref[pl.ds(start, TK), :], qz[half], _NT, preferred_element_type=F32)

    def values(kb, half):
        return vt_ref[kb, HEAD_DIM * half:HEAD_DIM * (half + 1), :]

    def alive(carry):
        return (jnp.min(carry) < SB_EXIT).astype(I32)

    has_prev = qi > 0
    kprev = jnp.maximum(qi - 1, 0)
    s_diag = [scores(qi, half) for half in range(2)]
    s_prev = [scores(kprev, half) for half in range(2)]
    state = []
    for half in range(2):
        carry = jnp.zeros((1, TQ), F32)
        acc = jnp.zeros((HEAD_DIM, TQ), F32)
        carry, acc = _sb_block(s_diag[half], values(qi, half), carry, acc, causal)
        carry2, acc2 = _sb_block(s_prev[half], values(kprev, half), carry, acc, None)
        state.append((jnp.where(has_prev, carry2, carry), jnp.where(has_prev, acc2, acc)))

    outs = []
    for half in range(2):
        carry, acc = state[half]

        def cond(st):
            return (st[0] >= 0) & (st[1] > 0)

        def body(st, half=half):
            kb, _, carry, acc = st
            carry, acc = _sb_block(scores(kb, half), values(kb, half), carry, acc, None)
            return kb - 1, alive(carry), carry, acc

        _, live, carry, acc = lax.while_loop(cond, body, (qi - 2, alive(carry), carry, acc))

        def meta(acc, half=half, carry=carry):
            sm = lax.dot_general(mk_ref[...], qz[half], _NT, preferred_element_type=F32)
            return _sb_meta_block(sm, mvt_ref[half], carry, acc)

        outs.append(lax.cond(live > 0, meta, lambda a: a, acc))
    o_ref[...] = jnp.concatenate(outs, axis=0).T.astype(BF16)


def _sb_attention(a_proj, k_perm, vt_perm, mproj, mvt, b, s):
    nq = s // TQ
    nkb = s // TK
    mkcol0 = D_SB // LANES
    return pl.pallas_call(
        _sb_kernel,
        grid=(b, H_SB // 2, nq),
        in_specs=[
            pl.BlockSpec((TQ, LANES), lambda bi, p, qi: (bi * nq + qi, p)),
            pl.BlockSpec((s, LANES), lambda bi, p, qi: (bi, p)),
            pl.BlockSpec((nkb, 2 * HEAD_DIM, TK), lambda bi, p, qi: (bi, p, 0)),
            pl.BlockSpec((N_META, LANES), lambda bi, p, qi: (0, mkcol0 + p)),
            pl.BlockSpec((2, HEAD_DIM, N_META), lambda bi, p, qi: (p, 0, 0)),
        ],
        out_specs=pl.BlockSpec((TQ, LANES), lambda bi, p, qi: (bi * nq + qi, p)),
        out_shape=jax.ShapeDtypeStruct((b * s, D_SB), BF16),
        compiler_params=_cparams(("parallel", "parallel", "arbitrary")),
        name="sb_attn",
    )(a_proj, k_perm, vt_perm, mproj, mvt)


def _with_ones_rows(vt):
    r = lax.broadcasted_iota(I32, (L_ROWS, vt.shape[1]), 0)
    ones = jnp.where(r == 0, 1.0, 0.0).astype(vt.dtype)
    return jnp.concatenate([vt, ones], axis=0)


def _df_update(s, vt_ext, st):
    m, acc = st
    m_new = jnp.maximum(m, jnp.max(s, axis=0, keepdims=True).astype(F32))
    alpha = jnp.exp(m - m_new)
    p = jnp.exp(s - m_new.astype(BF16))
    acc = alpha * acc + jnp.dot(vt_ext, p, preferred_element_type=F32)
    return m_new, acc


def _df_kernel(q_ref, k_ref, vt_ref, mk_ref, mvt_ref, bt_ref, mb_ref,
               lq1_ref, lk1_ref, lq2_ref, lk2_ref, gain_ref, o_ref):
    qi = pl.program_id(2)
    qpair = q_ref[...]
    qz = [_head_half(qpair, 0), _head_half(qpair, 1)]
    dv = 2 * HEAD_DIM

    def scores(d):
        kb = jnp.maximum(qi - d, 0)
        kblk = k_ref[pl.ds(pl.multiple_of(kb * TB, TB), TB), :]
        bias = bt_ref[jnp.minimum(d, N_BIAS_TILES - 1)]
        return tuple((lax.dot_general(kblk, qz[i], _NT, preferred_element_type=F32) + bias).astype(BF16)
                     for i in range(2))

    def init():
        return (jnp.full((1, TB), -jnp.inf, F32), jnp.zeros((dv + L_ROWS, TB), F32))

    def body(d, carry):
        st, s_cur = carry
        s_nxt = scores(d + 1)
        vt = _with_ones_rows(vt_ref[qi - d])
        return tuple(_df_update(s_cur[i], vt, st[i]) for i in range(2)), s_nxt

    st, _ = lax.fori_loop(0, qi + 1, body, ((init(), init()), scores(0)))
    mk = mk_ref[...]
    mvt = _with_ones_rows(mvt_ref[...])
    mb = mb_ref[...]
    st = tuple(
        _df_update((lax.dot_general(mk, qz[i], _NT, preferred_element_type=F32) + mb).astype(BF16),
                   mvt, st[i])
        for i in range(2))

    lam = (jnp.exp(jnp.sum(lq1_ref[...] * lk1_ref[...], axis=-1, keepdims=True))
           - jnp.exp(jnp.sum(lq2_ref[...] * lk2_ref[...], axis=-1, keepdims=True))
           + LAMBDA_INIT)
    acc1, acc2 = st[0][1], st[1][1]
    o = acc1[:dv] / acc1[dv:dv + 1] - lam * (acc2[:dv] / acc2[dv:dv + 1])
    ms = jnp.mean(o * o, axis=0, keepdims=True)
    y = o * lax.rsqrt(ms + SUBLN_EPS) * gain_ref[...] * (1.0 - LAMBDA_INIT)
    o_ref[...] = y.T.astype(BF16)


def _diff_attention(a_proj, vt, mproj, mvt, bias_tiles, meta_bias,
                    lq1, lk1, lq2, lk2, gain_col, b, s):
    nq = s // TB
    nkb = s // TB
    qcol0 = D_SB // LANES
    kcol0 = (D_SB + D_DIFF) // LANES
    mkcol0 = (3 * D_SB + D_DIFF) // LANES
    lam_spec = pl.BlockSpec((1, HEAD_DIM), lambda bi, h, qi: (0, 0))
    return pl.pallas_call(
        _df_kernel,
        grid=(b, H_DIFF, nq),
        in_specs=[
            pl.BlockSpec((TB, LANES), lambda bi, h, qi: (bi * nq + qi, qcol0 + h)),
            pl.BlockSpec((s, LANES), lambda bi, h, qi: (bi, kcol0 + h)),
            pl.BlockSpec((nkb, 2 * HEAD_DIM, TB), lambda bi, h, qi: (bi, h, 0)),
            pl.BlockSpec((N_META, LANES), lambda bi, h, qi: (0, mkcol0 + h)),
            pl.BlockSpec((None, 2 * HEAD_DIM, N_META), lambda bi, h, qi: (h, 0, 0)),
            pl.BlockSpec((None, N_BIAS_TILES, TB, TB), lambda bi, h, qi: (h, 0, 0, 0)),
            pl.BlockSpec((None, N_META, TB), lambda bi, h, qi: (h, 0, qi)),
            lam_spec, lam_spec, lam_spec, lam_spec,
            pl.BlockSpec((2 * HEAD_DIM, 1), lambda bi, h, qi: (0, 0)),
        ],
        out_specs=pl.BlockSpec((TB, LANES), lambda bi, h, qi: (bi * nq + qi, h)),
        out_shape=jax.ShapeDtypeStruct((b * s, D_DIFF), BF16),
        compiler_params=_cparams(("parallel", "parallel", "arbitrary")),
        name="diff_attn",
    )(a_proj, a_proj, vt, mproj, mvt, bias_tiles, meta_bias, lq1, lk1, lq2, lk2, gain_col)


def _outrouter_kernel(x_ref, osb_ref, odf_ref, wo_ref, g_ref, wrt_ref, br_ref, tri_ref,
                      h1_ref, xn_ref, idx_ref, gate_ref, rank_ref, cnt_ref, carry_ref):
    @pl.when(pl.program_id(0) == 0)
    def _():
        carry_ref[...] = jnp.zeros_like(carry_ref)

    tm = x_ref.shape[0]
    mix = jnp.concatenate([osb_ref[...], odf_ref[...]], axis=1)
    h1 = x_ref[...] + jnp.dot(mix, wo_ref[...], preferred_element_type=F32)
    h1_ref[...] = h1
    ms = jnp.mean(h1 * h1, axis=-1, keepdims=True)
    xn = h1 * lax.rsqrt(ms + NORM_EPS) * g_ref[...]
    for c in range(ROW_TILE):
        xn_ref[pl.ds(c, tm, stride=ROW_TILE), :] = xn[:, LANES * c:LANES * (c + 1)]

    logits = lax.dot_general(wrt_ref[...], xn.astype(BF16), _NT, preferred_element_type=F32) + br_ref[...]
    e_iota = lax.broadcasted_iota(I32, logits.shape, 0)
    work = logits
    vals, idxs = [], []
    for _ in range(TOP_K):
        m = jnp.max(work, axis=0, keepdims=True)
        ik = jnp.min(jnp.where(work == m, e_iota, N_EXPERTS), axis=0, keepdims=True)
        vals.append(m)
        idxs.append(ik)
        work = jnp.where(e_iota == ik, -jnp.inf, work)
    exps = [jnp.exp(v - vals[0]) for v in vals]
    den = exps[0] + exps[1] + exps[2] + exps[3]
    onehot = jnp.zeros(logits.shape, F32)
    for ik in idxs:
        onehot = onehot + jnp.where(e_iota == ik, 1.0, 0.0)
    prefix = jnp.dot(onehot.astype(BF16), tri_ref[...], preferred_element_type=F32)
    pos = prefix + carry_ref[:, 0:1]
    for k in range(TOP_K):
        idx_ref[k:k + 1, :] = idxs[k]
        gate_ref[k:k + 1, :] = exps[k] / den
        rank_ref[k:k + 1, :] = jnp.sum(jnp.where(e_iota == idxs[k], pos, 0.0), axis=0,
                                       keepdims=True).astype(I32)
    carry_ref[...] = carry_ref[...] + jnp.sum(onehot, axis=1, keepdims=True)
    cnt_ref[...] = carry_ref[...]


def _out_router(x2d, o_sb, o_df, wo_bf16, gain, wr_t, br_col, tri, tm):
    t = x2d.shape[0]
    const = lambda i: (0, 0)
    return pl.pallas_call(
        _outrouter_kernel,
        grid=(t // tm,),
        in_specs=[
            pl.BlockSpec((tm, D_MODEL), lambda i: (i, 0)),
            pl.BlockSpec((tm, D_SB), lambda i: (i, 0)),
            pl.BlockSpec((tm, D_DIFF), lambda i: (i, 0)),
            pl.BlockSpec((D_SB + D_DIFF, D_MODEL), const),
            pl.BlockSpec((1, D_MODEL), const),
            pl.BlockSpec((N_EXPERTS, D_MODEL), const),
            pl.BlockSpec((N_EXPERTS, 1), const),
            pl.BlockSpec((tm, tm), const),
        ],
        out_specs=[
            pl.BlockSpec((tm, D_MODEL), lambda i: (i, 0)),
            pl.BlockSpec((tm * ROW_TILE, LANES), lambda i: (i, 0)),
            pl.BlockSpec((TOP_K, tm), lambda i: (0, i)),
            pl.BlockSpec((TOP_K, tm), lambda i: (0, i)),
            pl.BlockSpec((TOP_K, tm), lambda i: (0, i)),
            pl.BlockSpec((N_EXPERTS, LANES), const),
        ],
        out_shape=[
            jax.ShapeDtypeStruct((t, D_MODEL), F32),
            jax.ShapeDtypeStruct((t * ROW_TILE, LANES), F32),
            jax.ShapeDtypeStruct((TOP_K, t), I32),
            jax.ShapeDtypeStruct((TOP_K, t), F32),
            jax.ShapeDtypeStruct((TOP_K, t), I32),
            jax.ShapeDtypeStruct((N_EXPERTS, LANES), F32),
        ],
        scratch_shapes=[pltpu.VMEM((N_EXPERTS, LANES), F32)],
        compiler_params=_cparams(("arbitrary",)),
        name="out_router",
    )(x2d, o_sb, o_df, wo_bf16, gain, wr_t, br_col, tri)


def _dest_kernel(ps_ref, idx_ref, rank_ref, dest_ref):
    idx = idx_ref[...]
    off = jnp.zeros(idx.shape, I32)
    for e in range(N_EXPERTS):
        off = jnp.where(idx == e, ps_ref[e], off)
    dest_ref[...] = rank_ref[...] + off


def _route_dest(pad_start, idx, rank):
    t = idx.shape[1]
    tt = min(t, 8192)
    grid_spec = pltpu.PrefetchScalarGridSpec(
        num_scalar_prefetch=1,
        grid=(t // tt,),
        in_specs=[pl.BlockSpec((TOP_K, tt), lambda i, ps: (0, i)),
                  pl.BlockSpec((TOP_K, tt), lambda i, ps: (0, i))],
        out_specs=pl.BlockSpec((TOP_K, tt), lambda i, ps: (0, i)),
    )
    return pl.pallas_call(
        _dest_kernel,
        grid_spec=grid_spec,
        out_shape=jax.ShapeDtypeStruct((TOP_K, t), I32),
        compiler_params=_cparams(("parallel",)),
        name="route_dest",
    )(pad_start, idx, rank)


def _zero_fill_padding(pe_ref, nu_ref, xs_hbm, zbuf, zsem, first_tail_block):
    zbuf[...] = jnp.zeros_like(zbuf)
    conds, copies = [], []
    for e in range(N_EXPERTS):
        prev_end = pe_ref[e - 1] if e > 0 else 0
        conds.append(pe_ref[e] > prev_end)
        start = jnp.maximum(pe_ref[e] - TM_FFN, 0)
        copies.append(pltpu.make_async_copy(zbuf, xs_hbm.at[pl.ds(start, TM_FFN)], zsem))
    for j in range(N_EXPERTS):
        blk = first_tail_block + j
        conds.append(blk >= nu_ref[0])
        copies.append(pltpu.make_async_copy(zbuf, xs_hbm.at[pl.ds(blk * TM_FFN, TM_FFN)], zsem))
    for cond, c in zip(conds, copies):
        pl.when(cond)(c.start)
    for cond, c in zip(conds, copies):
        pl.when(cond)(c.wait)


def _dispatch_kernel(pe_ref, nu_ref, dest_ref, xn_ref, xs_hbm, zbuf, sem, zsem):
    step = pl.program_id(0)
    first_tail_block = xs_hbm.shape[0] // TM_FFN - N_EXPERTS

    @pl.when(step == 0)
    def _():
        _zero_fill_padding(pe_ref, nu_ref, xs_hbm, zbuf, zsem, first_tail_block)

    def issue(r, _):
        for k in range(TOP_K):
            pltpu.make_async_copy(xn_ref.at[r], xs_hbm.at[dest_ref[k, r]], sem).start(priority=k % 2)
        return 0

    lax.fori_loop(0, TD, issue, 0, unroll=8)

    def drain(_, carry):
        for _ in range(DRAIN_UNROLL):
            pltpu.make_async_copy(xn_ref.at[0], xs_hbm.at[0], sem).wait()
        return carry

    lax.fori_loop(0, TD * TOP_K // DRAIN_UNROLL, drain, 0)


def _dispatch(pad_end, n_used, dest, xn3, a_pad):
    t = dest.shape[1]
    grid_spec = pltpu.PrefetchScalarGridSpec(
        num_scalar_prefetch=2,
        grid=(t // TD,),
        in_specs=[
            pl.BlockSpec((TOP_K, TD), lambda i, pe, nu: (0, i), memory_space=pltpu.SMEM),
            pl.BlockSpec((TD, ROW_TILE, LANES), lambda i, pe, nu: (i, 0, 0)),
        ],
        out_specs=pl.BlockSpec(memory_space=pl.ANY),
        scratch_shapes=[
            pltpu.VMEM((TM_FFN, ROW_TILE, LANES), F32),
            pltpu.SemaphoreType.DMA(()),
            pltpu.SemaphoreType.DMA(()),
        ],
    )
    return pl.pallas_call(
        _dispatch_kernel,
        grid_spec=grid_spec,
        out_shape=jax.ShapeDtypeStruct((a_pad, ROW_TILE, LANES), F32),
        compiler_params=_cparams(("arbitrary",)),
        name="dispatch",
    )(pad_end, n_used, dest, xn3)


def _ffn_kernel(be_ref, new_ref, nu_ref, xs_ref, w1_ref, b1_ref, w2_ref, b2_ref, y_ref, w1b, w2b):
    i = pl.program_id(0)

    @pl.when(i >= nu_ref[0])
    def _():
        y_ref[...] = jnp.zeros_like(y_ref)

    @pl.when((i < nu_ref[0]) & (new_ref[i] > 0))
    def _():
        w1b[...] = w1_ref[...].astype(BF16)
        w2b[...] = w2_ref[...].astype(BF16)

    @pl.when(i < nu_ref[0])
    def _():
        x = jnp.concatenate(
            [xs_ref[pl.ds(c, TM_FFN, stride=ROW_TILE), :] for c in range(ROW_TILE)], axis=1).astype(BF16)
        hu = jnp.dot(x, w1b[...], preferred_element_type=F32) + b1_ref[...]
        gate = jnp.minimum(hu[:, :D_FF], SWIGLU_LIMIT)
        lin = jnp.clip(hu[:, D_FF:], -SWIGLU_LIMIT, SWIGLU_LIMIT)
        act = gate * jax.nn.sigmoid(SWIGLU_ALPHA * gate) * (lin + 1.0)
        y = jnp.dot(act.astype(BF16), w2b[...], preferred_element_type=F32) + b2_ref[...]
        for c in range(ROW_TILE):
            y_ref[pl.ds(c, TM_FFN, stride=ROW_TILE), :] = y[:, LANES * c:LANES * (c + 1)]


def _expert_ffn(blk_e, blk_new, n_used, xs2d, w1, b1, w2, b2, n_blocks):
    rows = TM_FFN * ROW_TILE

    def xmap(i, be, new, nu):
        return (jnp.minimum(i, nu[0] - 1), 0)

    def wmap(i, be, new, nu):
        return (be[i], 0, 0)

    grid_spec = pltpu.PrefetchScalarGridSpec(
        num_scalar_prefetch=3,
        grid=(n_blocks,),
        in_specs=[
            pl.BlockSpec((rows, LANES), xmap),
            pl.BlockSpec((None, D_MODEL, 2 * D_FF), wmap),
            pl.BlockSpec((None, 1, 2 * D_FF), wmap),
            pl.BlockSpec((None, D_FF, D_MODEL), wmap),
            pl.BlockSpec((None, 1, D_MODEL), wmap),
        ],
        out_specs=pl.BlockSpec((rows, LANES), lambda i, be, new, nu: (i, 0)),
        scratch_shapes=[pltpu.VMEM((D_MODEL, 2 * D_FF), BF16), pltpu.VMEM((D_FF, D_MODEL), BF16)],
    )
    return pl.pallas_call(
        _ffn_kernel,
        grid_spec=grid_spec,
        out_shape=jax.ShapeDtypeStruct(xs2d.shape, F32),
        compiler_params=_cparams(("arbitrary",)),
        name="expert_ffn",
    )(blk_e, blk_new, n_used, xs2d, w1, b1, w2, b2)


def _combine_gather(dest_ref, y_hbm, buf, sem, slot):
    def issue(r, carry):
        for k in range(TOP_K):
            row0 = pl.multiple_of(dest_ref[k, r] * ROW_TILE, ROW_TILE)
            dst0 = pl.multiple_of(((slot * TOP_K + k) * TD + r) * ROW_TILE, ROW_TILE)
            pltpu.make_async_copy(
                y_hbm.at[pl.ds(row0, ROW_TILE)], buf.at[pl.ds(dst0, ROW_TILE)],
                sem.at[slot]).start(priority=k % 2)
        return carry

    lax.fori_loop(0, TD, issue, 0, unroll=8)


def _combine_drain(y_hbm, buf, sem, slot):
    def drain(_, carry):
        for _ in range(DRAIN_UNROLL):
            pltpu.make_async_copy(
                y_hbm.at[pl.ds(0, ROW_TILE)], buf.at[pl.ds(0, ROW_TILE)], sem.at[slot]).wait()
        return carry

    lax.fori_loop(0, TD * TOP_K // DRAIN_UNROLL, drain, 0)


def _combine_kernel(dcur_ref, dnext_ref, y_hbm, gates_ref, h1_ref, g_ref, o_ref, buf, sem):
    i = pl.program_id(0)
    n = pl.num_programs(0)

    @pl.when(i == 0)
    def _():
        _combine_gather(dcur_ref, y_hbm, buf, sem, 0)

    for slot in range(2):
        @pl.when((i % 2 == slot) & (i + 1 < n))
        def _(slot=slot):
            _combine_gather(dnext_ref, y_hbm, buf, sem, 1 - slot)

    for slot in range(2):
        @pl.when(i % 2 == slot)
        def _(slot=slot):
            _combine_drain(y_hbm, buf, sem, slot)
            acc = h1_ref[...]
            gates = gates_ref[...]
            for k in range(TOP_K):
                base = (slot * TOP_K + k) * TD * ROW_TILE
                yk = jnp.concatenate(
                    [buf[pl.ds(base + c, TD, stride=ROW_TILE), :] for c in range(ROW_TILE)], axis=1)
                acc = acc + yk * gates[:, k:k + 1]
            ms = jnp.mean(acc * acc, axis=-1, keepdims=True)
            o_ref[...] = acc * lax.rsqrt(ms + NORM_EPS) * g_ref[...]


def _combine(dest, y2d, gates_t, h1, gain):
    t = h1.shape[0]
    n = t // TD
    return pl.pallas_call(
        _combine_kernel,
        grid=(n,),
        in_specs=[
            pl.BlockSpec((TOP_K, TD), lambda i: (0, i), memory_space=pltpu.SMEM),
            pl.BlockSpec((TOP_K, TD), lambda i: (0, jnp.minimum(i + 1, n - 1)), memory_space=pltpu.SMEM),
            pl.BlockSpec(memory_space=pl.ANY),
            pl.BlockSpec((TD, TOP_K), lambda i: (i, 0)),
            pl.BlockSpec((TD, D_MODEL), lambda i: (i, 0)),
            pl.BlockSpec((1, D_MODEL), lambda i: (0, 0)),
        ],
        out_specs=pl.BlockSpec((TD, D_MODEL), lambda i: (i, 0)),
        out_shape=jax.ShapeDtypeStruct((t, D_MODEL), F32),
        scratch_shapes=[
            pltpu.VMEM((2 * TOP_K * TD * ROW_TILE, LANES), F32),
            pltpu.SemaphoreType.DMA((2,)),
        ],
        compiler_params=_cparams(("arbitrary",)),
        name="combine",
    )(dest, dest, y2d, gates_t, h1, gain)


def _block_plan(counts, n_blocks):
    padded = (counts + TM_FFN - 1) // TM_FFN * TM_FFN
    pad_end = jnp.cumsum(padded)
    pad_start = pad_end - padded
    blk_start = jnp.arange(n_blocks, dtype=I32) * TM_FFN
    blk_e = jnp.sum((pad_end[None, :] <= blk_start[:, None]).astype(I32), axis=1)
    blk_e = jnp.minimum(blk_e, N_EXPERTS - 1)
    blk_new = jnp.concatenate([jnp.ones((1,), I32), (blk_e[1:] != blk_e[:-1]).astype(I32)])
    n_used = (pad_end[-1:] // TM_FFN).astype(I32)
    return pad_start.astype(I32), pad_end.astype(I32), blk_e, blk_new, n_used


@jax.jit
def _forward(x, meta_tokens, rel_bias, attn_norm, w_in, w_out, lambda_q1, lambda_k1, lambda_q2,
             lambda_k2, subln_gain, ffn_norm, w_router, b_router, w1, b1, w2, b2, final_norm):
    b, s, _ = x.shape
    t = b * s
    assert TQ == TK and TM_PROJ % TK == 0 and TM_PROJ % TB == 0
    assert s % TQ == 0 and s % TB == 0 and t % TM_PROJ == 0 and t % TD == 0
    x2d = x.reshape(t, D_MODEL)

    scale = HEAD_DIM ** -0.5
    c_sbk, c_sbv, c_dfq, c_dfk, c_dfv = D_SB, 2 * D_SB, 3 * D_SB, 3 * D_SB + D_DIFF, 3 * D_SB + 2 * D_DIFF
    w_in_b = w_in[0].astype(BF16)
    g_attn = attn_norm[0][None, :]
    colscale = jnp.ones((D_IN,), F32).at[0:D_SB].set(scale).at[c_dfq:c_dfk].set(scale)[None, :]
    w_a = jnp.concatenate([w_in_b[:, :c_sbk], w_in_b[:, c_dfq:c_dfv]], axis=1)
    cs_a = jnp.concatenate([colscale[:, :c_sbk], colscale[:, c_dfq:c_dfv]], axis=1)
    a_proj, k_perm, vt_sb, vt_df = _in_proj_tokens(
        x2d, g_attn, cs_a, w_a, w_in_b[:, c_sbk:c_sbv], w_in_b[:, c_sbv:c_dfq].T, w_in_b[:, c_dfv:].T)
    mproj = _in_proj_meta(meta_tokens, g_attn, colscale, w_in_b)
    mvt_sb = mproj[:, c_sbv:c_dfq].reshape(N_META, H_SB, HEAD_DIM).transpose(1, 2, 0)
    mvt_df = mproj[:, c_dfv:].reshape(N_META, H_DIFF, 2 * HEAD_DIM).transpose(1, 2, 0)

    o_sb = _sb_attention(a_proj, k_perm, vt_sb, mproj, mvt_sb, b, s)

    bias_tiles, meta_bias = _rel_bias_tiles(rel_bias, s)
    o_df = _diff_attention(a_proj, vt_df, mproj, mvt_df, bias_tiles, meta_bias,
                           lambda_q1, lambda_k1, lambda_q2, lambda_k2,
                           subln_gain[0][:, None], b, s)

    tri = jnp.triu(jnp.ones((TM_PROJ, TM_PROJ), BF16), k=1)
    h1, xn2d, idx, gates, rank, cnt = _out_router(
        x2d, o_sb, o_df, w_out[0].astype(BF16), ffn_norm[0][None, :],
        w_router[0].T.astype(BF16), b_router[0][:, None], tri, TM_PROJ)

    a = t * TOP_K
    n_blocks = a // TM_FFN + N_EXPERTS
    a_pad = n_blocks * TM_FFN
    counts = cnt[:, 0].astype(I32)
    pad_start, pad_end, blk_e, blk_new, n_used = _block_plan(counts, n_blocks)
    dest = _route_dest(pad_start, idx, rank)

    xs = _dispatch(pad_end, n_used, dest, xn2d.reshape(t, ROW_TILE, LANES), a_pad)
    y2d = _expert_ffn(blk_e, blk_new, n_used, xs.reshape(a_pad * ROW_TILE, LANES),
                      w1[0], b1[0][:, None, :], w2[0], b2[0][:, None, :], n_blocks)
    out = _combine(dest, y2d, gates.T, h1, final_norm[None, :])
    return out.reshape(b, s, D_MODEL)


def kernel(x, meta_tokens, rel_bias, attn_norm, w_in, w_out, lambda_q1, lambda_k1, lambda_q2,
           lambda_k2, subln_gain, ffn_norm, w_router, b_router, w1, b1, w2, b2, final_norm):
    return _forward(x, meta_tokens, rel_bias, attn_norm, w_in, w_out, lambda_q1, lambda_k1,
                    lambda_q2, lambda_k2, subln_gain, ffn_norm, w_router, b_router, w1, b1, w2, b2,
                    final_norm)
```

```python
import functools
import math

import jax
import jax.numpy as jnp
from jax import lax
from jax.experimental import pallas as pl
from jax.experimental.pallas import tpu as pltpu

D_MODEL = 1024
N_META = 16
CHUNK = 64
HEAD_DIM = 64
H_SB = 8
H_DIFF = 4
D_SB = H_SB * HEAD_DIM
D_DIFF = H_DIFF * 2 * HEAD_DIM
D_IN = 3 * D_SB + 3 * D_DIFF
N_BUCKETS = 32
N_EXPERTS = 32
TOP_K = 4
D_FF = D_MODEL
SWIGLU_ALPHA = 1.702
SWIGLU_LIMIT = 7.0
NORM_EPS = 1e-6
SUBLN_EPS = 1e-5
NEG_BIG = -1e30
LAMBDA_INIT = 0.8 - 0.6 * math.exp(-0.3 * 0)

LANES = 128
SUBLANES = 8
ROW_TILE = D_MODEL // LANES
VMEM_LIMIT = 56 * 1024 * 1024

TM_PROJ = 512
TQ = 256
TK = 256
KCH = TK // SUBLANES
TB = 512
L_ROWS = 16
TM_FFN = 512
TD = 256
DRAIN_UNROLL = 128
RING = 3
COMBINE_ROWS = 32
N_BIAS_TILES = 3
SB_EXIT = 104.0

F32 = jnp.float32
BF16 = jnp.bfloat16
I32 = jnp.int32

_NT = (((1,), (1,)), ((), ()))


def _cparams(sem, vmem=VMEM_LIMIT):
    return pltpu.CompilerParams(dimension_semantics=sem, vmem_limit_bytes=vmem)


def _inproj_kernel(x_ref, g_ref, cs_ref, w_ref, o_ref):
    x = x_ref[...]
    ms = jnp.mean(x * x, axis=-1, keepdims=True)
    xn = (x * lax.rsqrt(ms + NORM_EPS) * g_ref[...]).astype(BF16)
    y = jnp.dot(xn, w_ref[...], preferred_element_type=F32)
    o_ref[...] = (y * cs_ref[...]).astype(BF16)


def _in_proj_meta(x2d, gain, colscale, w_bf16):
    t = x2d.shape[0]
    return pl.pallas_call(
        _inproj_kernel,
        grid=(1,),
        in_specs=[
            pl.BlockSpec((t, D_MODEL), lambda i: (0, 0)),
            pl.BlockSpec((1, D_MODEL), lambda i: (0, 0)),
            pl.BlockSpec((1, D_IN), lambda i: (0, 0)),
            pl.BlockSpec((D_MODEL, D_IN), lambda i: (0, 0)),
        ],
        out_specs=pl.BlockSpec((t, D_IN), lambda i: (0, 0)),
        out_shape=jax.ShapeDtypeStruct((t, D_IN), BF16),
        compiler_params=_cparams(("arbitrary",)),
        name="in_proj_meta",
    )(x2d, gain, colscale, w_bf16)


def _rms_bf16(x, g):
    ms = jnp.mean(x * x, axis=-1, keepdims=True)
    return (x * lax.rsqrt(ms + NORM_EPS) * g).astype(BF16)


def _inproj_tokens_kernel(x_ref, g_ref, cs_ref, perm_ref, wa_ref, wk_ref, wvs_ref, wvd_ref,
                          a_ref, kp_ref, vts_ref, vtd_ref):
    xn = _rms_bf16(x_ref[...], g_ref[...])
    xnp = jnp.concatenate(
        [jnp.dot(perm_ref[...], xn[blk * TK:(blk + 1) * TK], preferred_element_type=F32)
         for blk in range(TM_PROJ // TK)], axis=0).astype(BF16)
    a_ref[...] = (jnp.dot(xn, wa_ref[...], preferred_element_type=F32) * cs_ref[...]).astype(BF16)
    kp_ref[...] = jnp.dot(xnp, wk_ref[...], preferred_element_type=F32).astype(BF16)
    vts = lax.dot_general(wvs_ref[...], xnp, _NT, preferred_element_type=F32).astype(BF16)
    for blk in range(TM_PROJ // TK):
        vts_ref[blk] = vts[:, blk * TK:(blk + 1) * TK]
    vtd = lax.dot_general(wvd_ref[...], xn, _NT, preferred_element_type=F32).astype(BF16)
    for blk in range(TM_PROJ // TB):
        vtd_ref[blk] = vtd[:, blk * TB:(blk + 1) * TB]


def _chunk_order_matrix():
    dst = jnp.arange(TK, dtype=I32)
    src = (dst % SUBLANES) * KCH + dst // SUBLANES
    return (src[:, None] == jnp.arange(TK, dtype=I32)[None, :]).astype(BF16)


def _in_proj_tokens(x2d, gain, cs_a, w_a, w_k, w_vs_t, w_vd_t):
    t = x2d.shape[0]
    tm = TM_PROJ
    n_a = w_a.shape[1]
    const = lambda i: (0, 0)
    return pl.pallas_call(
        _inproj_tokens_kernel,
        grid=(t // tm,),
        in_specs=[
            pl.BlockSpec((tm, D_MODEL), lambda i: (i, 0)),
            pl.BlockSpec((1, D_MODEL), const),
            pl.BlockSpec((1, n_a), const),
            pl.BlockSpec((TK, TK), const),
            pl.BlockSpec((D_MODEL, n_a), const),
            pl.BlockSpec((D_MODEL, D_SB), const),
            pl.BlockSpec((D_SB, D_MODEL), const),
            pl.BlockSpec((D_DIFF, D_MODEL), const),
        ],
        out_specs=[
            pl.BlockSpec((tm, n_a), lambda i: (i, 0)),
            pl.BlockSpec((tm, D_SB), lambda i: (i, 0)),
            pl.BlockSpec((tm // TK, D_SB, TK), lambda i: (i, 0, 0)),
            pl.BlockSpec((tm // TB, D_DIFF, TB), lambda i: (i, 0, 0)),
        ],
        out_shape=[
            jax.ShapeDtypeStruct((t, n_a), BF16),
            jax.ShapeDtypeStruct((t, D_SB), BF16),
            jax.ShapeDtypeStruct((t // TK, D_SB, TK), BF16),
            jax.ShapeDtypeStruct((t // TB, D_DIFF, TB), BF16),
        ],
        compiler_params=_cparams(("parallel",)),
        name="in_proj",
    )(x2d, gain, cs_a, _chunk_order_matrix(), w_a, w_k, w_vs_t, w_vd_t)


def _bias_lookup(rel, rb_ref, h):
    n = jnp.abs(rel)
    n2 = n * n
    large = jnp.full(rel.shape, 8, I32)
    for k in range(1, 8):
        large = large + jnp.where(n2 >= (64 << k), 1, 0)
    bucket = jnp.where(rel > 0, N_BUCKETS // 2, 0) + jnp.where(n < 8, n, large)
    out = jnp.zeros(rel.shape, F32)
    for b in range(N_BUCKETS):
        out = jnp.where(bucket == b, rb_ref[b, h], out)
    return out


def _relbias_kernel(rb_ref, bt_ref, mb_ref):
    h = pl.program_id(0)
    krow = lax.broadcasted_iota(I32, (TB, TB), 0)
    qcol = lax.broadcasted_iota(I32, (TB, TB), 1)
    visible = (krow // CHUNK) <= (qcol // CHUNK)
    bt_ref[0, 0] = jnp.where(visible, _bias_lookup(krow - qcol, rb_ref, h), NEG_BIG)
    bt_ref[0, 1] = _bias_lookup(krow - qcol - TB, rb_ref, h)
    bt_ref[0, 2] = _bias_lookup(krow - qcol - 2 * TB, rb_ref, h)
    s = mb_ref.shape[2]
    mrow = lax.broadcasted_iota(I32, (N_META, s), 0)
    qpos = lax.broadcasted_iota(I32, (N_META, s), 1) + N_META
    mb_ref[0] = _bias_lookup(mrow - qpos, rb_ref, h)


def _rel_bias_tiles(rel_bias, s):
    return pl.pallas_call(
        _relbias_kernel,
        grid=(H_DIFF,),
        in_specs=[pl.BlockSpec(memory_space=pltpu.SMEM)],
        out_specs=[
            pl.BlockSpec((1, N_BIAS_TILES, TB, TB), lambda h: (h, 0, 0, 0)),
            pl.BlockSpec((1, N_META, s), lambda h: (h, 0, 0)),
        ],
        out_shape=[
            jax.ShapeDtypeStruct((H_DIFF, N_BIAS_TILES, TB, TB), F32),
            jax.ShapeDtypeStruct((H_DIFF, N_META, s), F32),
        ],
        compiler_params=_cparams(("arbitrary",)),
        name="rel_bias",
    )(rel_bias)


def _suffix_incl_sublanes(x):
    r = lax.broadcasted_iota(I32, x.shape, 0)
    for d in (1, 2, 4):
        shifted = pltpu.roll(x, SUBLANES - d, axis=0)
        x = x + jnp.where(r + d < SUBLANES, shifted, 0.0)
    return x


def _softplus(s):
    return jnp.maximum(s, 0.0) + jnp.log(1.0 + jnp.exp(-jnp.abs(s)))


def _head_half(qpair, half):
    lane = lax.broadcasted_iota(I32, qpair.shape, 1)
    keep = (lane >= HEAD_DIM * half) & (lane < HEAD_DIM * (half + 1))
    return jnp.where(keep, qpair, jnp.zeros_like(qpair))


def _sb_block(s, vt, carry, acc, valid):
    sp = _softplus(s)
    if valid is not None:
        sp = jnp.where(valid, sp, 0.0)
    run = jnp.zeros((SUBLANES, s.shape[1]), F32)
    parts = [None] * KCH
    for i in reversed(range(KCH)):
        run = run + sp[SUBLANES * i:SUBLANES * (i + 1), :]
        parts[i] = run
    incl = _suffix_incl_sublanes(run)
    base = (incl - run) + carry
    r_sum = jnp.concatenate([p + base for p in parts], axis=0)
    w = jnp.exp(s - r_sum)
    if valid is not None:
        w = jnp.where(valid, w, 0.0)
    acc = acc + jnp.dot(vt, w.astype(BF16), preferred_element_type=F32)
    return carry + incl[0:1, :], acc


def _sb_meta_block(s, vt, carry, acc):
    sp = _softplus(s)
    lo, hi = sp[0:SUBLANES, :], sp[SUBLANES:2 * SUBLANES, :]
    hi_incl = _suffix_incl_sublanes(hi)
    lo_incl = _suffix_incl_sublanes(lo) + hi_incl[0:1, :]
    r_sum = jnp.concatenate([lo_incl, hi_incl], axis=0) + carry
    w = jnp.exp(s - r_sum)
    return acc + jnp.dot(vt, w.astype(BF16), preferred_element_type=F32)


def _sb_kernel(q_ref, k_ref, vt_ref, mk_ref, mvt_ref, o_ref):
    qi = pl.program_id(2)
    qpair = q_ref[...]
    row = lax.broadcasted_iota(I32, (TK, TQ), 0)
    lane = lax.broadcasted_iota(I32, (TK, TQ), 1)
    key_off = (row % SUBLANES) * KCH + row // SUBLANES
    causal = key_off < lane
    qz = [_head_half(qpair, half) for half in range(2)]

    def scores(kb, half):
        start = pl.multiple_of(kb * TK, TK)
        return lax.dot_general(k_ref[pl.ds(start, TK), :], qz[half], _NT, preferred_element_type=F32)

    def values(kb, half):
        return vt_ref[kb, HEAD_DIM * half:HEAD_DIM * (half + 1), :]

    def alive(carry):
        return (jnp.min(carry) < SB_EXIT).astype(I32)

    has_prev = qi > 0
    kprev = jnp.maximum(qi - 1, 0)
    s_diag = [scores(qi, half) for half in range(2)]
    s_prev = [scores(kprev, half) for half in range(2)]
    state = []
    for half in range(2):
        carry = jnp.zeros((1, TQ), F32)
        acc = jnp.zeros((HEAD_DIM, TQ), F32)
        carry, acc = _sb_block(s_diag[half], values(qi, half), carry, acc, causal)
        carry2, acc2 = _sb_block(s_prev[half], values(kprev, half), carry, acc, None)
        state.append((jnp.where(has_prev, carry2, carry), jnp.where(has_prev, acc2, acc)))

    outs = []
    for half in range(2):
        carry, acc = state[half]

        def cond(st):
            return (st[0] >= 0) & (st[1] > 0)

        def body(st, half=half):
            kb, _, carry, acc = st
            carry, acc = _sb_block(scores(kb, half), values(kb, half), carry, acc, None)
            return kb - 1, alive(carry), carry, acc

        _, live, carry, acc = lax.while_loop(cond, body, (qi - 2, alive(carry), carry, acc))

        def meta(acc, half=half, carry=carry):
            sm = lax.dot_general(mk_ref[...], qz[half], _NT, preferred_element_type=F32)
            return _sb_meta_block(sm, mvt_ref[half], carry, acc)

        outs.append(lax.cond(live > 0, meta, lambda a: a, acc))
    o_ref[...] = jnp.concatenate(outs, axis=0).T.astype(BF16)


def _sb_attention(a_proj, k_perm, vt_perm, mproj, mvt, b, s):
    nq = s // TQ
    nkb = s // TK
    mkcol0 = D_SB // LANES
    return pl.pallas_call(
        _sb_kernel,
        grid=(b, H_SB // 2, nq),
        in_specs=[
            pl.BlockSpec((TQ, LANES), lambda bi, p, qi: (bi * nq + qi, p)),
            pl.BlockSpec((s, LANES), lambda bi, p, qi: (bi, p)),
            pl.BlockSpec((nkb, 2 * HEAD_DIM, TK), lambda bi, p, qi: (bi, p, 0)),
            pl.BlockSpec((N_META, LANES), lambda bi, p, qi: (0, mkcol0 + p)),
            pl.BlockSpec((2, HEAD_DIM, N_META), lambda bi, p, qi: (p, 0, 0)),
        ],
        out_specs=pl.BlockSpec((TQ, LANES), lambda bi, p, qi: (bi * nq + qi, p)),
        out_shape=jax.ShapeDtypeStruct((b * s, D_SB), BF16),
        compiler_params=_cparams(("parallel", "parallel", "arbitrary")),
        name="sb_attn",
    )(a_proj, k_perm, vt_perm, mproj, mvt)


def _with_ones_rows(vt):
    r = lax.broadcasted_iota(I32, (L_ROWS, vt.shape[1]), 0)
    ones = jnp.where(r == 0, 1.0, 0.0).astype(vt.dtype)
    return jnp.concatenate([vt, ones], axis=0)


def _df_update(s, vt_ext, st):
    m, acc = st
    m_new = jnp.maximum(m, jnp.max(s, axis=0, keepdims=True).astype(F32))
    alpha = jnp.exp(m - m_new)
    p = jnp.exp(s - m_new.astype(BF16))
    acc = alpha * acc + jnp.dot(vt_ext, p, preferred_element_type=F32)
    return m_new, acc


def _df_kernel(q_ref, k_ref, vt_ref, mk_ref, mvt_ref, bt_ref, mb_ref,
               lq1_ref, lk1_ref, lq2_ref, lk2_ref, gain_ref, o_ref):
    qi = pl.program_id(2)
    qpair = q_ref[...]
    qz = [_head_half(qpair, 0), _head_half(qpair, 1)]
    dv = 2 * HEAD_DIM

    def scores(d):
        kb = jnp.maximum(qi - d, 0)
        kblk = k_ref[pl.ds(pl.multiple_of(kb * TB, TB), TB), :]
        bias = bt_ref[jnp.minimum(d, N_BIAS_TILES - 1)]
        return tuple((lax.dot_general(kblk, qz[i], _NT, preferred_element_type=F32) + bias).astype(BF16)
                     for i in range(2))

    def init():
        return (jnp.full((1, TB), -jnp.inf, F32), jnp.zeros((dv + L_ROWS, TB), F32))

    def body(d, carry):
        st, s_cur = carry
        s_nxt = scores(d + 1)
        vt = _with_ones_rows(vt_ref[qi - d])
        return tuple(_df_update(s_cur[i], vt, st[i]) for i in range(2)), s_nxt

    st, _ = lax.fori_loop(0, qi + 1, body, ((init(), init()), scores(0)))
    mk = mk_ref[...]
    mvt = _with_ones_rows(mvt_ref[...])
    mb = mb_ref[...]
    st = tuple(
        _df_update((lax.dot_general(mk, qz[i], _NT, preferred_element_type=F32) + mb).astype(BF16),
                   mvt, st[i])
        for i in range(2))

    lam = (jnp.exp(jnp.sum(lq1_ref[...] * lk1_ref[...], axis=-1, keepdims=True))
           - jnp.exp(jnp.sum(lq2_ref[...] * lk2_ref[...], axis=-1, keepdims=True))
           + LAMBDA_INIT)
    acc1, acc2 = st[0][1], st[1][1]
    o = acc1[:dv] / acc1[dv:dv + 1] - lam * (acc2[:dv] / acc2[dv:dv + 1])
    ms = jnp.mean(o * o, axis=0, keepdims=True)
    y = o * lax.rsqrt(ms + SUBLN_EPS) * gain_ref[...] * (1.0 - LAMBDA_INIT)
    o_ref[...] = y.T.astype(BF16)


def _diff_attention(a_proj, vt, mproj, mvt, bias_tiles, meta_bias,
                    lq1, lk1, lq2, lk2, gain_col, b, s):
    nq = s // TB
    nkb = s // TB
    qcol0 = D_SB // LANES
    kcol0 = (D_SB + D_DIFF) // LANES
    mkcol0 = (3 * D_SB + D_DIFF) // LANES
    lam_spec = pl.BlockSpec((1, HEAD_DIM), lambda bi, h, qi: (0, 0))
    return pl.pallas_call(
        _df_kernel,
        grid=(b, H_DIFF, nq),
        in_specs=[
            pl.BlockSpec((TB, LANES), lambda bi, h, qi: (bi * nq + qi, qcol0 + h)),
            pl.BlockSpec((s, LANES), lambda bi, h, qi: (bi, kcol0 + h)),
            pl.BlockSpec((nkb, 2 * HEAD_DIM, TB), lambda bi, h, qi: (bi, h, 0)),
            pl.BlockSpec((N_META, LANES), lambda bi, h, qi: (0, mkcol0 + h)),
            pl.BlockSpec((None, 2 * HEAD_DIM, N_META), lambda bi, h, qi: (h, 0, 0)),
            pl.BlockSpec((None, N_BIAS_TILES, TB, TB), lambda bi, h, qi: (h, 0, 0, 0)),
            pl.BlockSpec((None, N_META, TB), lambda bi, h, qi: (h, 0, qi)),
            lam_spec, lam_spec, lam_spec, lam_spec,
            pl.BlockSpec((2 * HEAD_DIM, 1), lambda bi, h, qi: (0, 0)),
        ],
        out_specs=pl.BlockSpec((TB, LANES), lambda bi, h, qi: (bi * nq + qi, h)),
        out_shape=jax.ShapeDtypeStruct((b * s, D_DIFF), BF16),
        compiler_params=_cparams(("parallel", "parallel", "arbitrary")),
        name="diff_attn",
    )(a_proj, a_proj, vt, mproj, mvt, bias_tiles, meta_bias, lq1, lk1, lq2, lk2, gain_col)


def _outrouter_kernel(x_ref, osb_ref, odf_ref, wo_ref, g_ref, wrt_ref, br_ref, tri_ref,
                      h1_ref, xn_ref, idx_ref, gate_ref, rank_ref, cnt_ref, carry_ref):
    @pl.when(pl.program_id(0) == 0)
    def _():
        carry_ref[...] = jnp.zeros_like(carry_ref)

    tm = x_ref.shape[0]
    mix = jnp.concatenate([osb_ref[...], odf_ref[...]], axis=1)
    h1 = x_ref[...] + jnp.dot(mix, wo_ref[...], preferred_element_type=F32)
    h1_ref[...] = h1
    ms = jnp.mean(h1 * h1, axis=-1, keepdims=True)
    xn = h1 * lax.rsqrt(ms + NORM_EPS) * g_ref[...]
    for c in range(ROW_TILE):
        xn_ref[pl.ds(c, tm, stride=ROW_TILE), :] = xn[:, LANES * c:LANES * (c + 1)]

    logits = lax.dot_general(wrt_ref[...], xn.astype(BF16), _NT, preferred_element_type=F32) + br_ref[...]
    e_iota = lax.broadcasted_iota(I32, logits.shape, 0)
    work = logits
    vals, idxs = [], []
    for _ in range(TOP_K):
        m = jnp.max(work, axis=0, keepdims=True)
        ik = jnp.min(jnp.where(work == m, e_iota, N_EXPERTS), axis=0, keepdims=True)
        vals.append(m)
        idxs.append(ik)
        work = jnp.where(e_iota == ik, -jnp.inf, work)
    exps = [jnp.exp(v - vals[0]) for v in vals]
    den = exps[0] + exps[1] + exps[2] + exps[3]
    onehot = jnp.zeros(logits.shape, F32)
    for ik in idxs:
        onehot = onehot + jnp.where(e_iota == ik, 1.0, 0.0)
    prefix = jnp.dot(onehot.astype(BF16), tri_ref[...], preferred_element_type=F32)
    pos = prefix + carry_ref[:, 0:1]
    for k in range(TOP_K):
        idx_ref[k:k + 1, :] = idxs[k]
        gate_ref[k:k + 1, :] = exps[k] / den
        rank_ref[k:k + 1, :] = jnp.sum(jnp.where(e_iota == idxs[k], pos, 0.0), axis=0,
                                       keepdims=True).astype(I32)
    carry_ref[...] = carry_ref[...] + jnp.sum(onehot, axis=1, keepdims=True)
    cnt_ref[...] = carry_ref[...]


def _out_router(x2d, o_sb, o_df, wo_bf16, gain, wr_t, br_col, tri, tm):
    t = x2d.shape[0]
    const = lambda i: (0, 0)
    return pl.pallas_call(
        _outrouter_kernel,
        grid=(t // tm,),
        in_specs=[
            pl.BlockSpec((tm, D_MODEL), lambda i: (i, 0)),
            pl.BlockSpec((tm, D_SB), lambda i: (i, 0)),
            pl.BlockSpec((tm, D_DIFF), lambda i: (i, 0)),
            pl.BlockSpec((D_SB + D_DIFF, D_MODEL), const),
            pl.BlockSpec((1, D_MODEL), const),
            pl.BlockSpec((N_EXPERTS, D_MODEL), const),
            pl.BlockSpec((N_EXPERTS, 1), const),
            pl.BlockSpec((tm, tm), const),
        ],
        out_specs=[
            pl.BlockSpec((tm, D_MODEL), lambda i: (i, 0)),
            pl.BlockSpec((tm * ROW_TILE, LANES), lambda i: (i, 0)),
            pl.BlockSpec((TOP_K, tm), lambda i: (0, i)),
            pl.BlockSpec((TOP_K, tm), lambda i: (0, i)),
            pl.BlockSpec((TOP_K, tm), lambda i: (0, i)),
            pl.BlockSpec((N_EXPERTS, LANES), const),
        ],
        out_shape=[
            jax.ShapeDtypeStruct((t, D_MODEL), F32),
            jax.ShapeDtypeStruct((t * ROW_TILE, LANES), F32),
            jax.ShapeDtypeStruct((TOP_K, t), I32),
            jax.ShapeDtypeStruct((TOP_K, t), F32),
            jax.ShapeDtypeStruct((TOP_K, t), I32),
            jax.ShapeDtypeStruct((N_EXPERTS, LANES), F32),
        ],
        scratch_shapes=[pltpu.VMEM((N_EXPERTS, LANES), F32)],
        compiler_params=_cparams(("arbitrary",)),
        name="out_router",
    )(x2d, o_sb, o_df, wo_bf16, gain, wr_t, br_col, tri)


def _dest_kernel(ps_ref, idx_ref, rank_ref, dest_ref):
    idx = idx_ref[...]
    off = jnp.zeros(idx.shape, I32)
    for e in range(N_EXPERTS):
        off = jnp.where(idx == e, ps_ref[e], off)
    dest_ref[...] = rank_ref[...] + off


def _route_dest(pad_start, idx, rank):
    t = idx.shape[1]
    tt = min(t, 8192)
    grid_spec = pltpu.PrefetchScalarGridSpec(
        num_scalar_prefetch=1,
        grid=(t // tt,),
        in_specs=[pl.BlockSpec((TOP_K, tt), lambda i, ps: (0, i)),
                  pl.BlockSpec((TOP_K, tt), lambda i, ps: (0, i))],
        out_specs=pl.BlockSpec((TOP_K, tt), lambda i, ps: (0, i)),
    )
    return pl.pallas_call(
        _dest_kernel,
        grid_spec=grid_spec,
        out_shape=jax.ShapeDtypeStruct((TOP_K, t), I32),
        compiler_params=_cparams(("parallel",)),
        name="route_dest",
    )(pad_start, idx, rank)


def _zero_fill_padding(pe_ref, nu_ref, xs_hbm, zbuf, zsem, first_tail_block):
    zbuf[...] = jnp.zeros_like(zbuf)
    conds, copies = [], []
    for e in range(N_EXPERTS):
        prev_end = pe_ref[e - 1] if e > 0 else 0
        conds.append(pe_ref[e] > prev_end)
        start = jnp.maximum(pe_ref[e] - TM_FFN, 0)
        copies.append(pltpu.make_async_copy(zbuf, xs_hbm.at[pl.ds(start, TM_FFN)], zsem))
    for j in range(N_EXPERTS):
        blk = first_tail_block + j
        conds.append(blk >= nu_ref[0])
        copies.append(pltpu.make_async_copy(zbuf, xs_hbm.at[pl.ds(blk * TM_FFN, TM_FFN)], zsem))
    for cond, c in zip(conds, copies):
        pl.when(cond)(c.start)
    for cond, c in zip(conds, copies):
        pl.when(cond)(c.wait)


def _dispatch_kernel(pe_ref, nu_ref, dest_ref, xn_hbm, xs_hbm, zbuf, ring, lsem, sem, zsem):
    step = pl.program_id(0)
    n = pl.num_programs(0)
    first_tail_block = xs_hbm.shape[0] // TM_FFN - N_EXPERTS

    def load(tile, slot):
        return pltpu.make_async_copy(xn_hbm.at[pl.ds(tile * TD, TD)], ring.at[slot], lsem.at[slot])

    @pl.when(step == 0)
    def _():
        load(0, 0).start()

        @pl.when(n > 1)
        def _():
            load(1, 1).start()

        _zero_fill_padding(pe_ref, nu_ref, xs_hbm, zbuf, zsem, first_tail_block)

    def drain(slot):
        def body(_, carry):
            for _ in range(DRAIN_UNROLL):
                pltpu.make_async_copy(ring.at[0, 0], xs_hbm.at[0], sem.at[slot]).wait()
            return carry

        lax.fori_loop(0, TD * TOP_K // DRAIN_UNROLL, body, 0)

    for slot in range(RING):
        @pl.when(step % RING == slot)
        def _(slot=slot):
            prev = (slot + RING - 1) % RING
            load(step, slot).wait()

            def issue(r, carry):
                for k in range(TOP_K):
                    pltpu.make_async_copy(
                        ring.at[slot, r], xs_hbm.at[dest_ref[k, r]], sem.at[slot]).start(priority=k % 2)
                return carry

            lax.fori_loop(0, TD, issue, 0, unroll=8)

            @pl.when(step > 0)
            def _():
                drain(prev)

            @pl.when(step + 2 < n)
            def _():
                load(step + 2, prev).start()

            @pl.when(step == n - 1)
            def _():
                drain(slot)


def _dispatch(pad_end, n_used, dest, xn3, a_pad):
    t = dest.shape[1]
    grid_spec = pltpu.PrefetchScalarGridSpec(
        num_scalar_prefetch=2,
        grid=(t // TD,),
        in_specs=[
            pl.BlockSpec((TOP_K, TD), lambda i, pe, nu: (0, i), memory_space=pltpu.SMEM),
            pl.BlockSpec(memory_space=pl.ANY),
        ],
        out_specs=pl.BlockSpec(memory_space=pl.ANY),
        scratch_shapes=[
            pltpu.VMEM((TM_FFN, ROW_TILE, LANES), F32),
            pltpu.VMEM((RING, TD, ROW_TILE, LANES), F32),
            pltpu.SemaphoreType.DMA((RING,)),
            pltpu.SemaphoreType.DMA((RING,)),
            pltpu.SemaphoreType.DMA(()),
        ],
    )
    return pl.pallas_call(
        _dispatch_kernel,
        grid_spec=grid_spec,
        out_shape=jax.ShapeDtypeStruct((a_pad, ROW_TILE, LANES), F32),
        compiler_params=_cparams(("arbitrary",)),
        name="dispatch",
    )(pad_end, n_used, dest, xn3)


def _ffn_kernel(be_ref, new_ref, nu_ref, xs_ref, w1_ref, b1_ref, w2_ref, b2_ref, y_ref, w1b, w2b):
    i = pl.program_id(0)

    @pl.when(i >= nu_ref[0])
    def _():
        y_ref[...] = jnp.zeros_like(y_ref)

    @pl.when((i < nu_ref[0]) & (new_ref[i] > 0))
    def _():
        w1b[...] = w1_ref[...].astype(BF16)
        w2b[...] = w2_ref[...].astype(BF16)

    @pl.when(i < nu_ref[0])
    def _():
        x = jnp.concatenate(
            [xs_ref[pl.ds(c, TM_FFN, stride=ROW_TILE), :] for c in range(ROW_TILE)], axis=1).astype(BF16)
        hu = jnp.dot(x, w1b[...], preferred_element_type=F32) + b1_ref[...]
        gate = jnp.minimum(hu[:, :D_FF], SWIGLU_LIMIT)
        lin = jnp.clip(hu[:, D_FF:], -SWIGLU_LIMIT, SWIGLU_LIMIT)
        act = gate * jax.nn.sigmoid(SWIGLU_ALPHA * gate) * (lin + 1.0)
        y = jnp.dot(act.astype(BF16), w2b[...], preferred_element_type=F32) + b2_ref[...]
        for c in range(ROW_TILE):
            y_ref[pl.ds(c, TM_FFN, stride=ROW_TILE), :] = y[:, LANES * c:LANES * (c + 1)]


def _expert_ffn(blk_e, blk_new, n_used, xs2d, w1, b1, w2, b2, n_blocks):
    rows = TM_FFN * ROW_TILE

    def xmap(i, be, new, nu):
        return (jnp.minimum(i, nu[0] - 1), 0)

    def wmap(i, be, new, nu):
        return (be[i], 0, 0)

    grid_spec = pltpu.PrefetchScalarGridSpec(
        num_scalar_prefetch=3,
        grid=(n_blocks,),
        in_specs=[
            pl.BlockSpec((rows, LANES), xmap),
            pl.BlockSpec((None, D_MODEL, 2 * D_FF), wmap),
            pl.BlockSpec((None, 1, 2 * D_FF), wmap),
            pl.BlockSpec((None, D_FF, D_MODEL), wmap),
            pl.BlockSpec((None, 1, D_MODEL), wmap),
        ],
        out_specs=pl.BlockSpec((rows, LANES), lambda i, be, new, nu: (i, 0)),
        scratch_shapes=[pltpu.VMEM((D_MODEL, 2 * D_FF), BF16), pltpu.VMEM((D_FF, D_MODEL), BF16)],
    )
    return pl.pallas_call(
        _ffn_kernel,
        grid_spec=grid_spec,
        out_shape=jax.ShapeDtypeStruct(xs2d.shape, F32),
        compiler_params=_cparams(("arbitrary",)),
        name="expert_ffn",
    )(blk_e, blk_new, n_used, xs2d, w1, b1, w2, b2)


def _combine_gather_rows(dest_ref, y_hbm, buf, sem, slot, r0, nrows):
    for rr in range(nrows):
        r = r0 + rr
        for k in range(TOP_K):
            row0 = pl.multiple_of(dest_ref[k, r] * ROW_TILE, ROW_TILE)
            dst0 = pl.multiple_of(((slot * TOP_K + k) * TD + r) * ROW_TILE, ROW_TILE)
            pltpu.make_async_copy(
                y_hbm.at[pl.ds(row0, ROW_TILE)], buf.at[pl.ds(dst0, ROW_TILE)],
                sem.at[slot]).start(priority=k % 2)


def _combine_drain(y_hbm, buf, sem, slot):
    def drain(_, carry):
        for _ in range(DRAIN_UNROLL):
            pltpu.make_async_copy(
                y_hbm.at[pl.ds(0, ROW_TILE)], buf.at[pl.ds(0, ROW_TILE)], sem.at[slot]).wait()
        return carry

    lax.fori_loop(0, TD * TOP_K // DRAIN_UNROLL, drain, 0)


def _combine_kernel(dcur_ref, dnext_ref, y_hbm, gates_ref, h1_ref, g_ref, o_ref, buf, sem):
    i = pl.program_id(0)
    n = pl.num_programs(0)

    @pl.when(i == 0)
    def _():
        def first(gi, carry):
            _combine_gather_rows(dcur_ref, y_hbm, buf, sem, 0, gi * SUBLANES, SUBLANES)
            return carry

        lax.fori_loop(0, TD // SUBLANES, first, 0)

    for slot in range(2):
        @pl.when(i % 2 == slot)
        def _(slot=slot):
            _combine_drain(y_hbm, buf, sem, slot)

            def group(gi, carry):
                t0 = pl.multiple_of(gi * COMBINE_ROWS, COMBINE_ROWS)
                _combine_gather_rows(dnext_ref, y_hbm, buf, sem, 1 - slot, t0, COMBINE_ROWS)
                rows = pl.ds(t0, COMBINE_ROWS)
                gates = gates_ref[rows, :]
                acc = h1_ref[rows, :]
                for k in range(TOP_K):
                    base = (slot * TOP_K + k) * TD * ROW_TILE
                    yk = jnp.concatenate(
                        [buf[pl.ds(base + t0 * ROW_TILE + c, COMBINE_ROWS, stride=ROW_TILE), :]
                         for c in range(ROW_TILE)], axis=1)
                    acc = acc + yk * gates[:, k:k + 1]
                ms = jnp.mean(acc * acc, axis=-1, keepdims=True)
                o_ref[rows, :] = acc * lax.rsqrt(ms + NORM_EPS) * g_ref[...]
                return carry

            lax.fori_loop(0, TD // COMBINE_ROWS, group, 0)

            @pl.when(i == n - 1)
            def _():
                _combine_drain(y_hbm, buf, sem, 1 - slot)


def _combine(dest, y2d, gates_t, h1, gain):
    t = h1.shape[0]
    n = t // TD
    return pl.pallas_call(
        _combine_kernel,
        grid=(n,),
        in_specs=[
            pl.BlockSpec((TOP_K, TD), lambda i: (0, i), memory_space=pltpu.SMEM),
            pl.BlockSpec((TOP_K, TD), lambda i: (0, jnp.minimum(i + 1, n - 1)), memory_space=pltpu.SMEM),
            pl.BlockSpec(memory_space=pl.ANY),
            pl.BlockSpec((TD, TOP_K), lambda i: (i, 0)),
            pl.BlockSpec((TD, D_MODEL), lambda i: (i, 0)),
            pl.BlockSpec((1, D_MODEL), lambda i: (0, 0)),
        ],
        out_specs=pl.BlockSpec((TD, D_MODEL), lambda i: (i, 0)),
        out_shape=jax.ShapeDtypeStruct((t, D_MODEL), F32),
        scratch_shapes=[
            pltpu.VMEM((2 * TOP_K * TD * ROW_TILE, LANES), F32),
            pltpu.SemaphoreType.DMA((2,)),
        ],
        compiler_params=_cparams(("arbitrary",)),
        name="combine",
    )(dest, dest, y2d, gates_t, h1, gain)


def _block_plan(counts, n_blocks):
    padded = (counts + TM_FFN - 1) // TM_FFN * TM_FFN
    pad_end = jnp.cumsum(padded)
    pad_start = pad_end - padded
    blk_start = jnp.arange(n_blocks, dtype=I32) * TM_FFN
    blk_e = jnp.sum((pad_end[None, :] <= blk_start[:, None]).astype(I32), axis=1)
    blk_e = jnp.minimum(blk_e, N_EXPERTS - 1)
    blk_new = jnp.concatenate([jnp.ones((1,), I32), (blk_e[1:] != blk_e[:-1]).astype(I32)])
    n_used = (pad_end[-1:] // TM_FFN).astype(I32)
    return pad_start.astype(I32), pad_end.astype(I32), blk_e, blk_new, n_used


@jax.jit
def _forward(x, meta_tokens, rel_bias, attn_norm, w_in, w_out, lambda_q1, lambda_k1, lambda_q2,
             lambda_k2, subln_gain, ffn_norm, w_router, b_router, w1, b1, w2, b2, final_norm):
    b, s, _ = x.shape
    t = b * s
    assert TQ == TK and TM_PROJ % TK == 0 and TM_PROJ % TB == 0
    assert s % TQ == 0 and s % TB == 0 and t % TM_PROJ == 0 and t % TD == 0
    x2d = x.reshape(t, D_MODEL)

    scale = HEAD_DIM ** -0.5
    c_sbk, c_sbv, c_dfq, c_dfk, c_dfv = D_SB, 2 * D_SB, 3 * D_SB, 3 * D_SB + D_DIFF, 3 * D_SB + 2 * D_DIFF
    w_in_b = w_in[0].astype(BF16)
    g_attn = attn_norm[0][None, :]
    colscale = jnp.ones((D_IN,), F32).at[0:D_SB].set(scale).at[c_dfq:c_dfk].set(scale)[None, :]
    w_a = jnp.concatenate([w_in_b[:, :c_sbk], w_in_b[:, c_dfq:c_dfv]], axis=1)
    cs_a = jnp.concatenate([colscale[:, :c_sbk], colscale[:, c_dfq:c_dfv]], axis=1)
    a_proj, k_perm, vt_sb, vt_df = _in_proj_tokens(
        x2d, g_attn, cs_a, w_a, w_in_b[:, c_sbk:c_sbv], w_in_b[:, c_sbv:c_dfq].T, w_in_b[:, c_dfv:].T)
    mproj = _in_proj_meta(meta_tokens, g_attn, colscale, w_in_b)
    mvt_sb = mproj[:, c_sbv:c_dfq].reshape(N_META, H_SB, HEAD_DIM).transpose(1, 2, 0)
    mvt_df = mproj[:, c_dfv:].reshape(N_META, H_DIFF, 2 * HEAD_DIM).transpose(1, 2, 0)

    o_sb = _sb_attention(a_proj, k_perm, vt_sb, mproj, mvt_sb, b, s)

    bias_tiles, meta_bias = _rel_bias_tiles(rel_bias, s)
    o_df = _diff_attention(a_proj, vt_df, mproj, mvt_df, bias_tiles, meta_bias,
                           lambda_q1, lambda_k1, lambda_q2, lambda_k2,
                           subln_gain[0][:, None], b, s)

    tri = jnp.triu(jnp.ones((TM_PROJ, TM_PROJ), BF16), k=1)
    h1, xn2d, idx, gates, rank, cnt = _out_router(
        x2d, o_sb, o_df, w_out[0].astype(BF16), ffn_norm[0][None, :],
        w_router[0].T.astype(BF16), b_router[0][:, None], tri, TM_PROJ)

    a = t * TOP_K
    n_blocks = a // TM_FFN + N_EXPERTS
    a_pad = n_blocks * TM_FFN
    counts = cnt[:, 0].astype(I32)
    pad_start, pad_end, blk_e, blk_new, n_used = _block_plan(counts, n_blocks)
    dest = _route_dest(pad_start, idx, rank)

    xs = _dispatch(pad_end, n_used, dest, xn2d.reshape(t, ROW_TILE, LANES), a_pad)
    y2d = _expert_ffn(blk_e, blk_new, n_used, xs.reshape(a_pad * ROW_TILE, LANES),
                      w1[0], b1[0][:, None, :], w2[0], b2[0][:, None, :], n_blocks)
    out = _combine(dest, y2d, gates.T, h1, final_norm[None, :])
    return out.reshape(b, s, D_MODEL)


def kernel(x, meta_tokens, rel_bias, attn_norm, w_in, w_out, lambda_q1, lambda_k1, lambda_q2,
           lambda_k2, subln_gain, ffn_norm, w_router, b_router, w1, b1, w2, b2, final_norm):
    return _forward(x, meta_tokens, rel_bias, attn_norm, w_in, w_out, lambda_q1, lambda_k1,
                    lambda_q2, lambda_k2, subln_gain, ffn_norm, w_router, b_router, w1, b1, w2, b2,
                    final_norm)
```

```python
import functools
import math

import jax
import jax.numpy as jnp
from jax import lax
from jax.experimental import pallas as pl
from jax.experimental.pallas import tpu as pltpu

D_MODEL = 1024
N_META = 16
CHUNK = 64
HEAD_DIM = 64
H_SB = 8
H_DIFF = 4
D_SB = H_SB * HEAD_DIM
D_DIFF = H_DIFF * 2 * HEAD_DIM
D_IN = 3 * D_SB + 3 * D_DIFF
N_BUCKETS = 32
N_EXPERTS = 32
TOP_K = 4
D_FF = D_MODEL
SWIGLU_ALPHA = 1.702
SWIGLU_LIMIT = 7.0
NORM_EPS = 1e-6
SUBLN_EPS = 1e-5
NEG_BIG = -1e30
LAMBDA_INIT = 0.8 - 0.6 * math.exp(-0.3 * 0)

LANES = 128
SUBLANES = 8
ROW_TILE = D_MODEL // LANES
VMEM_LIMIT = 56 * 1024 * 1024

TM_PROJ = 512
TQ = 256
TK = 256
KCH = TK // SUBLANES
TB = 512
TBK = 512
KB_PER_Q = TB // TBK
L_ROWS = 16
TM_FFN = 512
TD = 256
DRAIN_UNROLL = 128
RING = 3
N_BIAS_TILES = KB_PER_Q + 2
SB_EXIT = 104.0

F32 = jnp.float32
BF16 = jnp.bfloat16
I32 = jnp.int32

_NT = (((1,), (1,)), ((), ()))


def _cparams(sem, vmem=VMEM_LIMIT):
    return pltpu.CompilerParams(dimension_semantics=sem, vmem_limit_bytes=vmem)


def _inproj_kernel(x_ref, g_ref, cs_ref, w_ref, o_ref):
    x = x_ref[...]
    ms = jnp.mean(x * x, axis=-1, keepdims=True)
    xn = (x * lax.rsqrt(ms + NORM_EPS) * g_ref[...]).astype(BF16)
    y = jnp.dot(xn, w_ref[...], preferred_element_type=F32)
    o_ref[...] = (y * cs_ref[...]).astype(BF16)


def _in_proj_meta(x2d, gain, colscale, w_bf16):
    t = x2d.shape[0]
    return pl.pallas_call(
        _inproj_kernel,
        grid=(1,),
        in_specs=[
            pl.BlockSpec((t, D_MODEL), lambda i: (0, 0)),
            pl.BlockSpec((1, D_MODEL), lambda i: (0, 0)),
            pl.BlockSpec((1, D_IN), lambda i: (0, 0)),
            pl.BlockSpec((D_MODEL, D_IN), lambda i: (0, 0)),
        ],
        out_specs=pl.BlockSpec((t, D_IN), lambda i: (0, 0)),
        out_shape=jax.ShapeDtypeStruct((t, D_IN), BF16),
        compiler_params=_cparams(("arbitrary",)),
        name="in_proj_meta",
    )(x2d, gain, colscale, w_bf16)


def _rms_bf16(x, g):
    ms = jnp.mean(x * x, axis=-1, keepdims=True)
    return (x * lax.rsqrt(ms + NORM_EPS) * g).astype(BF16)


def _inproj_tokens_kernel(x_ref, g_ref, cs_ref, perm_ref, wa_ref, wk_ref, wvs_ref, wvd_ref,
                          a_ref, kp_ref, vts_ref, vtd_ref):
    xn = _rms_bf16(x_ref[...], g_ref[...])
    xnp = jnp.concatenate(
        [jnp.dot(perm_ref[...], xn[blk * TK:(blk + 1) * TK], preferred_element_type=F32)
         for blk in range(TM_PROJ // TK)], axis=0).astype(BF16)
    a_ref[...] = (jnp.dot(xn, wa_ref[...], preferred_element_type=F32) * cs_ref[...]).astype(BF16)
    kp_ref[...] = jnp.dot(xnp, wk_ref[...], preferred_element_type=F32).astype(BF16)
    vts = lax.dot_general(wvs_ref[...], xnp, _NT, preferred_element_type=F32).astype(BF16)
    for blk in range(TM_PROJ // TK):
        vts_ref[blk] = vts[:, blk * TK:(blk + 1) * TK]
    vtd = lax.dot_general(wvd_ref[...], xn, _NT, preferred_element_type=F32).astype(BF16)
    for blk in range(TM_PROJ // TBK):
        vtd_ref[blk] = vtd[:, blk * TBK:(blk + 1) * TBK]


def _chunk_order_matrix():
    dst = jnp.arange(TK, dtype=I32)
    src = (dst % SUBLANES) * KCH + dst // SUBLANES
    return (src[:, None] == jnp.arange(TK, dtype=I32)[None, :]).astype(BF16)


def _in_proj_tokens(x2d, gain, cs_a, w_a, w_k, w_vs_t, w_vd_t):
    t = x2d.shape[0]
    tm = TM_PROJ
    n_a = w_a.shape[1]
    const = lambda i: (0, 0)
    return pl.pallas_call(
        _inproj_tokens_kernel,
        grid=(t // tm,),
        in_specs=[
            pl.BlockSpec((tm, D_MODEL), lambda i: (i, 0)),
            pl.BlockSpec((1, D_MODEL), const),
            pl.BlockSpec((1, n_a), const),
            pl.BlockSpec((TK, TK), const),
            pl.BlockSpec((D_MODEL, n_a), const),
            pl.BlockSpec((D_MODEL, D_SB), const),
            pl.BlockSpec((D_SB, D_MODEL), const),
            pl.BlockSpec((D_DIFF, D_MODEL), const),
        ],
        out_specs=[
            pl.BlockSpec((tm, n_a), lambda i: (i, 0)),
            pl.BlockSpec((tm, D_SB), lambda i: (i, 0)),
            pl.BlockSpec((tm // TK, D_SB, TK), lambda i: (i, 0, 0)),
            pl.BlockSpec((tm // TBK, D_DIFF, TBK), lambda i: (i, 0, 0)),
        ],
        out_shape=[
            jax.ShapeDtypeStruct((t, n_a), BF16),
            jax.ShapeDtypeStruct((t, D_SB), BF16),
            jax.ShapeDtypeStruct((t // TK, D_SB, TK), BF16),
            jax.ShapeDtypeStruct((t // TBK, D_DIFF, TBK), BF16),
        ],
        compiler_params=_cparams(("parallel",)),
        name="in_proj",
    )(x2d, gain, cs_a, _chunk_order_matrix(), w_a, w_k, w_vs_t, w_vd_t)


def _bias_lookup(rel, rb_ref, h):
    n = jnp.abs(rel)
    n2 = n * n
    large = jnp.full(rel.shape, 8, I32)
    for k in range(1, 8):
        large = large + jnp.where(n2 >= (64 << k), 1, 0)
    bucket = jnp.where(rel > 0, N_BUCKETS // 2, 0) + jnp.where(n < 8, n, large)
    out = jnp.zeros(rel.shape, F32)
    for b in range(N_BUCKETS):
        out = jnp.where(bucket == b, rb_ref[b, h], out)
    return out


def _relbias_kernel(rb_ref, bt_ref, mb_ref):
    h = pl.program_id(0)
    krow = lax.broadcasted_iota(I32, (TBK, TB), 0)
    qcol = lax.broadcasted_iota(I32, (TBK, TB), 1)
    for d in range(N_BIAS_TILES):
        key_off = (KB_PER_Q - 1 - d) * TBK + krow
        visible = (key_off // CHUNK) <= (qcol // CHUNK)
        bt_ref[0, d] = jnp.where(visible, _bias_lookup(key_off - qcol, rb_ref, h), NEG_BIG)
    s = mb_ref.shape[2]
    mrow = lax.broadcasted_iota(I32, (N_META, s), 0)
    qpos = lax.broadcasted_iota(I32, (N_META, s), 1) + N_META
    mb_ref[0] = _bias_lookup(mrow - qpos, rb_ref, h)


def _rel_bias_tiles(rel_bias, s):
    return pl.pallas_call(
        _relbias_kernel,
        grid=(H_DIFF,),
        in_specs=[pl.BlockSpec(memory_space=pltpu.SMEM)],
        out_specs=[
            pl.BlockSpec((1, N_BIAS_TILES, TBK, TB), lambda h: (h, 0, 0, 0)),
            pl.BlockSpec((1, N_META, s), lambda h: (h, 0, 0)),
        ],
        out_shape=[
            jax.ShapeDtypeStruct((H_DIFF, N_BIAS_TILES, TBK, TB), F32),
            jax.ShapeDtypeStruct((H_DIFF, N_META, s), F32),
        ],
        compiler_params=_cparams(("arbitrary",)),
        name="rel_bias",
    )(rel_bias)


def _suffix_incl_sublanes(x):
    r = lax.broadcasted_iota(I32, x.shape, 0)
    for d in (1, 2, 4):
        shifted = pltpu.roll(x, SUBLANES - d, axis=0)
        x = x + jnp.where(r + d < SUBLANES, shifted, 0.0)
    return x


def _softplus(s):
    return jnp.maximum(s, 0.0) + jnp.log(1.0 + jnp.exp(-jnp.abs(s)))


def _head_half(qpair, half):
    lane = lax.broadcasted_iota(I32, qpair.shape, 1)
    keep = (lane >= HEAD_DIM * half) & (lane < HEAD_DIM * (half + 1))
    return jnp.where(keep, qpair, jnp.zeros_like(qpair))


def _sb_block(s, vt, carry, acc, valid):
    sp = _softplus(s)
    if valid is not None:
        sp = jnp.where(valid, sp, 0.0)
    run = jnp.zeros((SUBLANES, s.shape[1]), F32)
    parts = [None] * KCH
    for i in reversed(range(KCH)):
        run = run + sp[SUBLANES * i:SUBLANES * (i + 1), :]
        parts[i] = run
    incl = _suffix_incl_sublanes(run)
    base = (incl - run) + carry
    r_sum = jnp.concatenate([p + base for p in parts], axis=0)
    w = jnp.exp(s - r_sum)
    if valid is not None:
        w = jnp.where(valid, w, 0.0)
    acc = acc + jnp.dot(vt, w.astype(BF16), preferred_element_type=F32)
    return carry + incl[0:1, :], acc


def _sb_meta_block(s, vt, carry, acc):
    sp = _softplus(s)
    lo, hi = sp[0:SUBLANES, :], sp[SUBLANES:2 * SUBLANES, :]
    hi_incl = _suffix_incl_sublanes(hi)
    lo_incl = _suffix_incl_sublanes(lo) + hi_incl[0:1, :]
    r_sum = jnp.concatenate([lo_incl, hi_incl], axis=0) + carry
    w = jnp.exp(s - r_sum)
    return acc + jnp.dot(vt, w.astype(BF16), preferred_element_type=F32)


def _sb_kernel(q_ref, k_ref, vt_ref, mk_ref, mvt_ref, o_ref):
    qi = pl.program_id(2)
    qpair = q_ref[...]
    row = lax.broadcasted_iota(I32, (TK, TQ), 0)
    lane = lax.broadcasted_iota(I32, (TK, TQ), 1)
    key_off = (row % SUBLANES) * KCH + row // SUBLANES
    causal = key_off < lane
    qz = [_head_half(qpair, half) for half in range(2)]

    def scores(kb, half):
        start = pl.multiple_of(kb * TK, TK)
        return lax.dot_general(k_ref[pl.ds(start, TK), :], qz[half], _NT, preferred_element_type=F32)

    def values(kb, half):
        return vt_ref[kb, HEAD_DIM * half:HEAD_DIM * (half + 1), :]

    def alive(carry):
        return (jnp.min(carry) < SB_EXIT).astype(I32)

    has_prev = qi > 0
    kprev = jnp.maximum(qi - 1, 0)
    s_diag = [scores(qi, half) for half in range(2)]
    s_prev = [scores(kprev, half) for half in range(2)]
    state = []
    for half in range(2):
        carry = jnp.zeros((1, TQ), F32)
        acc = jnp.zeros((HEAD_DIM, TQ), F32)
        carry, acc = _sb_block(s_diag[half], values(qi, half), carry, acc, causal)
        carry2, acc2 = _sb_block(s_prev[half], values(kprev, half), carry, acc, None)
        state.append((jnp.where(has_prev, carry2, carry), jnp.where(has_prev, acc2, acc)))

    outs = []
    for half in range(2):
        carry, acc = state[half]

        def cond(st):
            return (st[0] >= 0) & (st[1] > 0)

        def body(st, half=half):
            kb, _, carry, acc = st
            carry, acc = _sb_block(scores(kb, half), values(kb, half), carry, acc, None)
            return kb - 1, alive(carry), carry, acc

        _, live, carry, acc = lax.while_loop(cond, body, (qi - 2, alive(carry), carry, acc))

        def meta(acc, half=half, carry=carry):
            sm = lax.dot_general(mk_ref[...], qz[half], _NT, preferred_element_type=F32)
            return _sb_meta_block(sm, mvt_ref[half], carry, acc)

        outs.append(lax.cond(live > 0, meta, lambda a: a, acc))
    o_ref[...] = jnp.concatenate(outs, axis=0).T.astype(BF16)


def _sb_attention(a_proj, k_perm, vt_perm, mproj, mvt, b, s):
    nq = s // TQ
    nkb = s // TK
    mkcol0 = D_SB // LANES
    return pl.pallas_call(
        _sb_kernel,
        grid=(b, H_SB // 2, nq),
        in_specs=[
            pl.BlockSpec((TQ, LANES), lambda bi, p, qi: (bi * nq + qi, p)),
            pl.BlockSpec((s, LANES), lambda bi, p, qi: (bi, p)),
            pl.BlockSpec((nkb, 2 * HEAD_DIM, TK), lambda bi, p, qi: (bi, p, 0)),
            pl.BlockSpec((N_META, LANES), lambda bi, p, qi: (0, mkcol0 + p)),
            pl.BlockSpec((2, HEAD_DIM, N_META), lambda bi, p, qi: (p, 0, 0)),
        ],
        out_specs=pl.BlockSpec((TQ, LANES), lambda bi, p, qi: (bi * nq + qi, p)),
        out_shape=jax.ShapeDtypeStruct((b * s, D_SB), BF16),
        compiler_params=_cparams(("parallel", "parallel", "arbitrary")),
        name="sb_attn",
    )(a_proj, k_perm, vt_perm, mproj, mvt)


def _with_ones_rows(vt):
    r = lax.broadcasted_iota(I32, (L_ROWS, vt.shape[1]), 0)
    ones = jnp.where(r == 0, 1.0, 0.0).astype(vt.dtype)
    return jnp.concatenate([vt, ones], axis=0)


def _df_update(s, vt_ext, st):
    m, acc = st
    m_new = jnp.maximum(m, jnp.max(s, axis=0, keepdims=True).astype(F32))
    alpha = jnp.exp(m - m_new)
    p = jnp.exp(s - m_new.astype(BF16))
    acc = alpha * acc + jnp.dot(vt_ext, p, preferred_element_type=F32)
    return m_new, acc


def _df_kernel(q_ref, k_ref, vt_ref, mk_ref, mvt_ref, bt_ref, mb_ref,
               lq1_ref, lk1_ref, lq2_ref, lk2_ref, gain_ref, o_ref):
    qi = pl.program_id(2)
    qpair = q_ref[...]
    qz = [_head_half(qpair, 0), _head_half(qpair, 1)]
    dv = 2 * HEAD_DIM

    kb_last = KB_PER_Q * qi + KB_PER_Q - 1

    def scores(d):
        kb = jnp.maximum(kb_last - d, 0)
        kblk = k_ref[pl.ds(pl.multiple_of(kb * TBK, TBK), TBK), :]
        bias = bt_ref[jnp.minimum(d, N_BIAS_TILES - 1)]
        return tuple((lax.dot_general(kblk, qz[i], _NT, preferred_element_type=F32) + bias).astype(BF16)
                     for i in range(2))

    def init():
        return (jnp.full((1, TB), -jnp.inf, F32), jnp.zeros((dv + L_ROWS, TB), F32))

    def body(d, carry):
        st, s_cur = carry
        s_nxt = scores(d + 1)
        vt = _with_ones_rows(vt_ref[kb_last - d])
        return tuple(_df_update(s_cur[i], vt, st[i]) for i in range(2)), s_nxt

    s_first = scores(0)
    mk = mk_ref[...]
    mvt = _with_ones_rows(mvt_ref[...])
    mb = mb_ref[...]
    st = tuple(
        _df_update((lax.dot_general(mk, qz[i], _NT, preferred_element_type=F32) + mb).astype(BF16),
                   mvt, init())
        for i in range(2))
    st, _ = lax.fori_loop(0, kb_last + 1, body, (st, s_first))

    lam = (jnp.exp(jnp.sum(lq1_ref[...] * lk1_ref[...], axis=-1, keepdims=True))
           - jnp.exp(jnp.sum(lq2_ref[...] * lk2_ref[...], axis=-1, keepdims=True))
           + LAMBDA_INIT)
    acc1, acc2 = st[0][1], st[1][1]
    o = acc1[:dv] / acc1[dv:dv + 1] - lam * (acc2[:dv] / acc2[dv:dv + 1])
    ms = jnp.mean(o * o, axis=0, keepdims=True)
    y = o * lax.rsqrt(ms + SUBLN_EPS) * gain_ref[...] * (1.0 - LAMBDA_INIT)
    o_ref[...] = y.T.astype(BF16)


def _diff_attention(a_proj, vt, mproj, mvt, bias_tiles, meta_bias,
                    lq1, lk1, lq2, lk2, gain_col, b, s):
    nq = s // TB
    nkb = s // TBK
    qcol0 = D_SB // LANES
    kcol0 = (D_SB + D_DIFF) // LANES
    mkcol0 = (3 * D_SB + D_DIFF) // LANES
    lam_spec = pl.BlockSpec((1, HEAD_DIM), lambda bi, h, qi: (0, 0))
    return pl.pallas_call(
        _df_kernel,
        grid=(b, H_DIFF, nq),
        in_specs=[
            pl.BlockSpec((TB, LANES), lambda bi, h, qi: (bi * nq + qi, qcol0 + h)),
            pl.BlockSpec((s, LANES), lambda bi, h, qi: (bi, kcol0 + h)),
            pl.BlockSpec((nkb, 2 * HEAD_DIM, TBK), lambda bi, h, qi: (bi, h, 0)),
            pl.BlockSpec((N_META, LANES), lambda bi, h, qi: (0, mkcol0 + h)),
            pl.BlockSpec((None, 2 * HEAD_DIM, N_META), lambda bi, h, qi: (h, 0, 0)),
            pl.BlockSpec((None, N_BIAS_TILES, TBK, TB), lambda bi, h, qi: (h, 0, 0, 0)),
            pl.BlockSpec((None, N_META, TB), lambda bi, h, qi: (h, 0, qi)),
            lam_spec, lam_spec, lam_spec, lam_spec,
            pl.BlockSpec((2 * HEAD_DIM, 1), lambda bi, h, qi: (0, 0)),
        ],
        out_specs=pl.BlockSpec((TB, LANES), lambda bi, h, qi: (bi * nq + qi, h)),
        out_shape=jax.ShapeDtypeStruct((b * s, D_DIFF), BF16),
        compiler_params=_cparams(("parallel", "parallel", "arbitrary")),
        name="diff_attn",
    )(a_proj, a_proj, vt, mproj, mvt, bias_tiles, meta_bias, lq1, lk1, lq2, lk2, gain_col)


def _outrouter_kernel(x_ref, osb_ref, odf_ref, wo_ref, g_ref, wrt_ref, br_ref, tri_ref,
                      h1_ref, xn_ref, idx_ref, gate_ref, rank_ref, cnt_ref, carry_ref):
    @pl.when(pl.program_id(0) == 0)
    def _():
        carry_ref[...] = jnp.zeros_like(carry_ref)

    tm = x_ref.shape[0]
    mix = jnp.concatenate([osb_ref[...], odf_ref[...]], axis=1)
    h1 = x_ref[...] + jnp.dot(mix, wo_ref[...], preferred_element_type=F32)
    h1_ref[...] = h1
    ms = jnp.mean(h1 * h1, axis=-1, keepdims=True)
    xn = h1 * lax.rsqrt(ms + NORM_EPS) * g_ref[...]
    for c in range(ROW_TILE):
        xn_ref[pl.ds(c, tm, stride=ROW_TILE), :] = xn[:, LANES * c:LANES * (c + 1)]

    logits = lax.dot_general(wrt_ref[...], xn.astype(BF16), _NT, preferred_element_type=F32) + br_ref[...]
    e_iota = lax.broadcasted_iota(I32, logits.shape, 0)
    work = logits
    vals, idxs = [], []
    for _ in range(TOP_K):
        m = jnp.max(work, axis=0, keepdims=True)
        ik = jnp.min(jnp.where(work == m, e_iota, N_EXPERTS), axis=0, keepdims=True)
        vals.append(m)
        idxs.append(ik)
        work = jnp.where(e_iota == ik, -jnp.inf, work)
    exps = [jnp.exp(v - vals[0]) for v in vals]
    den = exps[0] + exps[1] + exps[2] + exps[3]
    onehot = jnp.zeros(logits.shape, F32)
    for ik in idxs:
        onehot = onehot + jnp.where(e_iota == ik, 1.0, 0.0)
    prefix = jnp.dot(onehot.astype(BF16), tri_ref[...], preferred_element_type=F32)
    pos = prefix + carry_ref[:, 0:1]
    for k in range(TOP_K):
        idx_ref[k:k + 1, :] = idxs[k]
        gate_ref[k:k + 1, :] = exps[k] / den
        rank_ref[k:k + 1, :] = jnp.sum(jnp.where(e_iota == idxs[k], pos, 0.0), axis=0,
                                       keepdims=True).astype(I32)
    carry_ref[...] = carry_ref[...] + jnp.sum(onehot, axis=1, keepdims=True)
    cnt_ref[...] = carry_ref[...]


def _out_router(x2d, o_sb, o_df, wo_bf16, gain, wr_t, br_col, tri, tm):
    t = x2d.shape[0]
    const = lambda i: (0, 0)
    return pl.pallas_call(
        _outrouter_kernel,
        grid=(t // tm,),
        in_specs=[
            pl.BlockSpec((tm, D_MODEL), lambda i: (i, 0)),
            pl.BlockSpec((tm, D_SB), lambda i: (i, 0)),
            pl.BlockSpec((tm, D_DIFF), lambda i: (i, 0)),
            pl.BlockSpec((D_SB + D_DIFF, D_MODEL), const),
            pl.BlockSpec((1, D_MODEL), const),
            pl.BlockSpec((N_EXPERTS, D_MODEL), const),
            pl.BlockSpec((N_EXPERTS, 1), const),
            pl.BlockSpec((tm, tm), const),
        ],
        out_specs=[
            pl.BlockSpec((tm, D_MODEL), lambda i: (i, 0)),
            pl.BlockSpec((tm * ROW_TILE, LANES), lambda i: (i, 0)),
            pl.BlockSpec((TOP_K, tm), lambda i: (0, i)),
            pl.BlockSpec((TOP_K, tm), lambda i: (0, i)),
            pl.BlockSpec((TOP_K, tm), lambda i: (0, i)),
            pl.BlockSpec((N_EXPERTS, LANES), const),
        ],
        out_shape=[
            jax.ShapeDtypeStruct((t, D_MODEL), F32),
            jax.ShapeDtypeStruct((t * ROW_TILE, LANES), F32),
            jax.ShapeDtypeStruct((TOP_K, t), I32),
            jax.ShapeDtypeStruct((TOP_K, t), F32),
            jax.ShapeDtypeStruct((TOP_K, t), I32),
            jax.ShapeDtypeStruct((N_EXPERTS, LANES), F32),
        ],
        scratch_shapes=[pltpu.VMEM((N_EXPERTS, LANES), F32)],
        compiler_params=_cparams(("arbitrary",)),
        name="out_router",
    )(x2d, o_sb, o_df, wo_bf16, gain, wr_t, br_col, tri)


def _dest_kernel(ps_ref, idx_ref, rank_ref, dest_ref):
    idx = idx_ref[...]
    off = jnp.zeros(idx.shape, I32)
    for e in range(N_EXPERTS):
        off = jnp.where(idx == e, ps_ref[e], off)
    dest_ref[...] = rank_ref[...] + off


def _route_dest(pad_start, idx, rank):
    t = idx.shape[1]
    tt = min(t, 8192)
    grid_spec = pltpu.PrefetchScalarGridSpec(
        num_scalar_prefetch=1,
        grid=(t // tt,),
        in_specs=[pl.BlockSpec((TOP_K, tt), lambda i, ps: (0, i)),
                  pl.BlockSpec((TOP_K, tt), lambda i, ps: (0, i))],
        out_specs=pl.BlockSpec((TOP_K, tt), lambda i, ps: (0, i)),
    )
    return pl.pallas_call(
        _dest_kernel,
        grid_spec=grid_spec,
        out_shape=jax.ShapeDtypeStruct((TOP_K, t), I32),
        compiler_params=_cparams(("parallel",)),
        name="route_dest",
    )(pad_start, idx, rank)


def _zero_fill_padding(pe_ref, nu_ref, xs_hbm, zbuf, zsem, first_tail_block):
    zbuf[...] = jnp.zeros_like(zbuf)
    conds, copies = [], []
    for e in range(N_EXPERTS):
        prev_end = pe_ref[e - 1] if e > 0 else 0
        conds.append(pe_ref[e] > prev_end)
        start = jnp.maximum(pe_ref[e] - TM_FFN, 0)
        copies.append(pltpu.make_async_copy(zbuf, xs_hbm.at[pl.ds(start, TM_FFN)], zsem))
    for j in range(N_EXPERTS):
        blk = first_tail_block + j
        conds.append(blk >= nu_ref[0])
        copies.append(pltpu.make_async_copy(zbuf, xs_hbm.at[pl.ds(blk * TM_FFN, TM_FFN)], zsem))
    for cond, c in zip(conds, copies):
        pl.when(cond)(c.start)
    for cond, c in zip(conds, copies):
        pl.when(cond)(c.wait)


def _dispatch_kernel(pe_ref, nu_ref, dest_ref, xn_hbm, xs_hbm, zbuf, ring, lsem, sem, zsem):
    step = pl.program_id(0)
    n = pl.num_programs(0)
    first_tail_block = xs_hbm.shape[0] // TM_FFN - N_EXPERTS

    def load(tile, slot):
        return pltpu.make_async_copy(xn_hbm.at[pl.ds(tile * TD, TD)], ring.at[slot], lsem.at[slot])

    @pl.when(step == 0)
    def _():
        load(0, 0).start()

        @pl.when(n > 1)
        def _():
            load(1, 1).start()

        _zero_fill_padding(pe_ref, nu_ref, xs_hbm, zbuf, zsem, first_tail_block)

    def drain(slot):
        def body(_, carry):
            for _ in range(DRAIN_UNROLL):
                pltpu.make_async_copy(ring.at[0, 0], xs_hbm.at[0], sem.at[slot]).wait()
            return carry

        lax.fori_loop(0, TD * TOP_K // DRAIN_UNROLL, body, 0)

    for slot in range(RING):
        @pl.when(step % RING == slot)
        def _(slot=slot):
            prev = (slot + RING - 1) % RING
            load(step, slot).wait()

            def issue(r, carry):
                for k in range(TOP_K):
                    pltpu.make_async_copy(
                        ring.at[slot, r], xs_hbm.at[dest_ref[k, r]], sem.at[slot]).start(priority=k % 2)
                return carry

            lax.fori_loop(0, TD, issue, 0, unroll=8)

            @pl.when(step > 0)
            def _():
                drain(prev)

            @pl.when(step + 2 < n)
            def _():
                load(step + 2, prev).start()

            @pl.when(step == n - 1)
            def _():
                drain(slot)


def _dispatch(pad_end, n_used, dest, xn3, a_pad):
    t = dest.shape[1]
    grid_spec = pltpu.PrefetchScalarGridSpec(
        num_scalar_prefetch=2,
        grid=(t // TD,),
        in_specs=[
            pl.BlockSpec((TOP_K, TD), lambda i, pe, nu: (0, i), memory_space=pltpu.SMEM),
            pl.BlockSpec(memory_space=pl.ANY),
        ],
        out_specs=pl.BlockSpec(memory_space=pl.ANY),
        scratch_shapes=[
            pltpu.VMEM((TM_FFN, ROW_TILE, LANES), F32),
            pltpu.VMEM((RING, TD, ROW_TILE, LANES), F32),
            pltpu.SemaphoreType.DMA((RING,)),
            pltpu.SemaphoreType.DMA((RING,)),
            pltpu.SemaphoreType.DMA(()),
        ],
    )
    return pl.pallas_call(
        _dispatch_kernel,
        grid_spec=grid_spec,
        out_shape=jax.ShapeDtypeStruct((a_pad, ROW_TILE, LANES), F32),
        compiler_params=_cparams(("arbitrary",)),
        name="dispatch",
    )(pad_end, n_used, dest, xn3)


def _ffn_kernel(be_ref, new_ref, nu_ref, xs_ref, w1_ref, b1_ref, w2_ref, b2_ref, y_ref, w1b, w2b):
    i = pl.program_id(0)

    @pl.when(i >= nu_ref[0])
    def _():
        y_ref[...] = jnp.zeros_like(y_ref)

    @pl.when((i < nu_ref[0]) & (new_ref[i] > 0))
    def _():
        w1b[...] = w1_ref[...].astype(BF16)
        w2b[...] = w2_ref[...].astype(BF16)

    @pl.when(i < nu_ref[0])
    def _():
        x = jnp.concatenate(
            [xs_ref[pl.ds(c, TM_FFN, stride=ROW_TILE), :] for c in range(ROW_TILE)], axis=1).astype(BF16)
        hu = jnp.dot(x, w1b[...], preferred_element_type=F32) + b1_ref[...]
        gate = jnp.minimum(hu[:, :D_FF], SWIGLU_LIMIT)
        lin = jnp.clip(hu[:, D_FF:], -SWIGLU_LIMIT, SWIGLU_LIMIT)
        act = gate * jax.nn.sigmoid(SWIGLU_ALPHA * gate) * (lin + 1.0)
        y = jnp.dot(act.astype(BF16), w2b[...], preferred_element_type=F32) + b2_ref[...]
        for c in range(ROW_TILE):
            y_ref[pl.ds(c, TM_FFN, stride=ROW_TILE), :] = y[:, LANES * c:LANES * (c + 1)]


def _expert_ffn(blk_e, blk_new, n_used, xs2d, w1, b1, w2, b2, n_blocks):
    rows = TM_FFN * ROW_TILE

    def xmap(i, be, new, nu):
        return (jnp.minimum(i, nu[0] - 1), 0)

    def wmap(i, be, new, nu):
        return (be[i], 0, 0)

    grid_spec = pltpu.PrefetchScalarGridSpec(
        num_scalar_prefetch=3,
        grid=(n_blocks,),
        in_specs=[
            pl.BlockSpec((rows, LANES), xmap),
            pl.BlockSpec((None, D_MODEL, 2 * D_FF), wmap),
            pl.BlockSpec((None, 1, 2 * D_FF), wmap),
            pl.BlockSpec((None, D_FF, D_MODEL), wmap),
            pl.BlockSpec((None, 1, D_MODEL), wmap),
        ],
        out_specs=pl.BlockSpec((rows, LANES), lambda i, be, new, nu: (i, 0)),
        scratch_shapes=[pltpu.VMEM((D_MODEL, 2 * D_FF), BF16), pltpu.VMEM((D_FF, D_MODEL), BF16)],
    )
    return pl.pallas_call(
        _ffn_kernel,
        grid_spec=grid_spec,
        out_shape=jax.ShapeDtypeStruct(xs2d.shape, F32),
        compiler_params=_cparams(("arbitrary",)),
        name="expert_ffn",
    )(blk_e, blk_new, n_used, xs2d, w1, b1, w2, b2)


def _combine_gather(dest_ref, y_hbm, buf, sem, slot):
    def issue(r, carry):
        for k in range(TOP_K):
            row0 = pl.multiple_of(dest_ref[k, r] * ROW_TILE, ROW_TILE)
            dst0 = pl.multiple_of(((slot * TOP_K + k) * TD + r) * ROW_TILE, ROW_TILE)
            pltpu.make_async_copy(
                y_hbm.at[pl.ds(row0, ROW_TILE)], buf.at[pl.ds(dst0, ROW_TILE)],
                sem.at[slot]).start(priority=k % 2)
        return carry

    lax.fori_loop(0, TD, issue, 0, unroll=8)


def _combine_drain(y_hbm, buf, sem, slot):
    def drain(_, carry):
        for _ in range(DRAIN_UNROLL):
            pltpu.make_async_copy(
                y_hbm.at[pl.ds(0, ROW_TILE)], buf.at[pl.ds(0, ROW_TILE)], sem.at[slot]).wait()
        return carry

    lax.fori_loop(0, TD * TOP_K // DRAIN_UNROLL, drain, 0)


def _combine_kernel(dcur_ref, dnext_ref, y_hbm, gates_ref, h1_ref, g_ref, o_ref, buf, sem):
    i = pl.program_id(0)
    n = pl.num_programs(0)

    @pl.when(i == 0)
    def _():
        _combine_gather(dcur_ref, y_hbm, buf, sem, 0)

    for slot in range(2):
        @pl.when((i % 2 == slot) & (i + 1 < n))
        def _(slot=slot):
            _combine_gather(dnext_ref, y_hbm, buf, sem, 1 - slot)

    for slot in range(2):
        @pl.when(i % 2 == slot)
        def _(slot=slot):
            _combine_drain(y_hbm, buf, sem, slot)
            acc = h1_ref[...]
            gates = gates_ref[...]
            for k in range(TOP_K):
                base = (slot * TOP_K + k) * TD * ROW_TILE
                yk = jnp.concatenate(
                    [buf[pl.ds(base + c, TD, stride=ROW_TILE), :] for c in range(ROW_TILE)], axis=1)
                acc = acc + yk * gates[:, k:k + 1]
            ms = jnp.mean(acc * acc, axis=-1, keepdims=True)
            o_ref[...] = acc * lax.rsqrt(ms + NORM_EPS) * g_ref[...]


def _combine(dest, y2d, gates_t, h1, gain):
    t = h1.shape[0]
    n = t // TD
    return pl.pallas_call(
        _combine_kernel,
        grid=(n,),
        in_specs=[
            pl.BlockSpec((TOP_K, TD), lambda i: (0, i), memory_space=pltpu.SMEM),
            pl.BlockSpec((TOP_K, TD), lambda i: (0, jnp.minimum(i + 1, n - 1)), memory_space=pltpu.SMEM),
            pl.BlockSpec(memory_space=pl.ANY),
            pl.BlockSpec((TD, TOP_K), lambda i: (i, 0)),
            pl.BlockSpec((TD, D_MODEL), lambda i: (i, 0)),
            pl.BlockSpec((1, D_MODEL), lambda i: (0, 0)),
        ],
        out_specs=pl.BlockSpec((TD, D_MODEL), lambda i: (i, 0)),
        out_shape=jax.ShapeDtypeStruct((t, D_MODEL), F32),
        scratch_shapes=[
            pltpu.VMEM((2 * TOP_K * TD * ROW_TILE, LANES), F32),
            pltpu.SemaphoreType.DMA((2,)),
        ],
        compiler_params=_cparams(("arbitrary",)),
        name="combine",
    )(dest, dest, y2d, gates_t, h1, gain)


def _block_plan(counts, n_blocks):
    padded = (counts + TM_FFN - 1) // TM_FFN * TM_FFN
    pad_end = jnp.cumsum(padded)
    pad_start = pad_end - padded
    blk_start = jnp.arange(n_blocks, dtype=I32) * TM_FFN
    blk_e = jnp.sum((pad_end[None, :] <= blk_start[:, None]).astype(I32), axis=1)
    blk_e = jnp.minimum(blk_e, N_EXPERTS - 1)
    blk_new = jnp.concatenate([jnp.ones((1,), I32), (blk_e[1:] != blk_e[:-1]).astype(I32)])
    n_used = (pad_end[-1:] // TM_FFN).astype(I32)
    return pad_start.astype(I32), pad_end.astype(I32), blk_e, blk_new, n_used


@jax.jit
def _forward(x, meta_tokens, rel_bias, attn_norm, w_in, w_out, lambda_q1, lambda_k1, lambda_q2,
             lambda_k2, subln_gain, ffn_norm, w_router, b_router, w1, b1, w2, b2, final_norm):
    b, s, _ = x.shape
    t = b * s
    assert TQ == TK and TM_PROJ % TK == 0 and TM_PROJ % TBK == 0 and TB % TBK == 0
    assert s % TQ == 0 and s % TB == 0 and t % TM_PROJ == 0 and t % TD == 0
    x2d = x.reshape(t, D_MODEL)

    scale = HEAD_DIM ** -0.5
    c_sbk, c_sbv, c_dfq, c_dfk, c_dfv = D_SB, 2 * D_SB, 3 * D_SB, 3 * D_SB + D_DIFF, 3 * D_SB + 2 * D_DIFF
    w_in_b = w_in[0].astype(BF16)
    g_attn = attn_norm[0][None, :]
    colscale = jnp.ones((D_IN,), F32).at[0:D_SB].set(scale).at[c_dfq:c_dfk].set(scale)[None, :]
    w_a = jnp.concatenate([w_in_b[:, :c_sbk], w_in_b[:, c_dfq:c_dfv]], axis=1)
    cs_a = jnp.concatenate([colscale[:, :c_sbk], colscale[:, c_dfq:c_dfv]], axis=1)
    a_proj, k_perm, vt_sb, vt_df = _in_proj_tokens(
        x2d, g_attn, cs_a, w_a, w_in_b[:, c_sbk:c_sbv], w_in_b[:, c_sbv:c_dfq].T, w_in_b[:, c_dfv:].T)
    mproj = _in_proj_meta(meta_tokens, g_attn, colscale, w_in_b)
    mvt_sb = mproj[:, c_sbv:c_dfq].reshape(N_META, H_SB, HEAD_DIM).transpose(1, 2, 0)
    mvt_df = mproj[:, c_dfv:].reshape(N_META, H_DIFF, 2 * HEAD_DIM).transpose(1, 2, 0)

    o_sb = _sb_attention(a_proj, k_perm, vt_sb, mproj, mvt_sb, b, s)

    bias_tiles, meta_bias = _rel_bias_tiles(rel_bias, s)
    o_df = _diff_attention(a_proj, vt_df, mproj, mvt_df, bias_tiles, meta_bias,
                           lambda_q1, lambda_k1, lambda_q2, lambda_k2,
                           subln_gain[0][:, None], b, s)

    tri = jnp.triu(jnp.ones((TM_PROJ, TM_PROJ), BF16), k=1)
    h1, xn2d, idx, gates, rank, cnt = _out_router(
        x2d, o_sb, o_df, w_out[0].astype(BF16), ffn_norm[0][None, :],
        w_router[0].T.astype(BF16), b_router[0][:, None], tri, TM_PROJ)

    a = t * TOP_K
    n_blocks = a // TM_FFN + N_EXPERTS
    a_pad = n_blocks * TM_FFN
    counts = cnt[:, 0].astype(I32)
    pad_start, pad_end, blk_e, blk_new, n_used = _block_plan(counts, n_blocks)
    dest = _route_dest(pad_start, idx, rank)

    xs = _dispatch(pad_end, n_used, dest, xn2d.reshape(t, ROW_TILE, LANES), a_pad)
    y2d = _expert_ffn(blk_e, blk_new, n_used, xs.reshape(a_pad * ROW_TILE, LANES),
                      w1[0], b1[0][:, None, :], w2[0], b2[0][:, None, :], n_blocks)
    out = _combine(dest, y2d, gates.T, h1, final_norm[None, :])
    return out.reshape(b, s, D_MODEL)


def kernel(x, meta_tokens, rel_bias, attn_norm, w_in, w_out, lambda_q1, lambda_k1, lambda_q2,
           lambda_k2, subln_gain, ffn_norm, w_router, b_router, w1, b1, w2, b2, final_norm):
    return _forward(x, meta_tokens, rel_bias, attn_norm, w_in, w_out, lambda_q1, lambda_k1,
                    lambda_q2, lambda_k2, subln_gain, ffn_norm, w_router, b_router, w1, b1, w2, b2,
                    final_norm)
```

```python
import functools
import math

import jax
import jax.numpy as jnp
from jax import lax
from jax.experimental import pallas as pl
from jax.experimental.pallas import tpu as pltpu

D_MODEL = 1024
N_META = 16
CHUNK = 64
HEAD_DIM = 64
H_SB = 8
H_DIFF = 4
D_SB = H_SB * HEAD_DIM
D_DIFF = H_DIFF * 2 * HEAD_DIM
D_IN = 3 * D_SB + 3 * D_DIFF
N_BUCKETS = 32
N_EXPERTS = 32
TOP_K = 4
D_FF = D_MODEL
SWIGLU_ALPHA = 1.702
SWIGLU_LIMIT = 7.0
NORM_EPS = 1e-6
SUBLN_EPS = 1e-5
NEG_BIG = -1e30
LAMBDA_INIT = 0.8 - 0.6 * math.exp(-0.3 * 0)

LANES = 128
SUBLANES = 8
ROW_TILE = D_MODEL // LANES
VMEM_LIMIT = 56 * 1024 * 1024

TM_PROJ = 512
TQ = 256
TK = 256
KCH = TK // SUBLANES
SB_PAIRS = 2
SB_QBLOCKS = 2
TB = 512
TBK = 512
KB_PER_Q = TB // TBK
L_ROWS = 16
TM_FFN = 512
TD = 256
DRAIN_UNROLL = 128
RING = 3
N_BIAS_TILES = KB_PER_Q + 2
SB_EXIT = 104.0

F32 = jnp.float32
BF16 = jnp.bfloat16
I32 = jnp.int32

_NT = (((1,), (1,)), ((), ()))


def _cparams(sem, vmem=VMEM_LIMIT):
    return pltpu.CompilerParams(dimension_semantics=sem, vmem_limit_bytes=vmem)


def _inproj_kernel(x_ref, g_ref, cs_ref, w_ref, o_ref):
    x = x_ref[...]
    ms = jnp.mean(x * x, axis=-1, keepdims=True)
    xn = (x * lax.rsqrt(ms + NORM_EPS) * g_ref[...]).astype(BF16)
    y = jnp.dot(xn, w_ref[...], preferred_element_type=F32)
    o_ref[...] = (y * cs_ref[...]).astype(BF16)


def _in_proj_meta(x2d, gain, colscale, w_bf16):
    t = x2d.shape[0]
    return pl.pallas_call(
        _inproj_kernel,
        grid=(1,),
        in_specs=[
            pl.BlockSpec((t, D_MODEL), lambda i: (0, 0)),
            pl.BlockSpec((1, D_MODEL), lambda i: (0, 0)),
            pl.BlockSpec((1, D_IN), lambda i: (0, 0)),
            pl.BlockSpec((D_MODEL, D_IN), lambda i: (0, 0)),
        ],
        out_specs=pl.BlockSpec((t, D_IN), lambda i: (0, 0)),
        out_shape=jax.ShapeDtypeStruct((t, D_IN), BF16),
        compiler_params=_cparams(("arbitrary",)),
        name="in_proj_meta",
    )(x2d, gain, colscale, w_bf16)


def _rms_bf16(x, g):
    ms = jnp.mean(x * x, axis=-1, keepdims=True)
    return (x * lax.rsqrt(ms + NORM_EPS) * g).astype(BF16)


def _inproj_tokens_kernel(x_ref, g_ref, cs_ref, perm_ref, wa_ref, wk_ref, wvs_ref, wvd_ref,
                          a_ref, kp_ref, vts_ref, vtd_ref):
    xn = _rms_bf16(x_ref[...], g_ref[...])
    xnp = jnp.concatenate(
        [jnp.dot(perm_ref[...], xn[blk * TK:(blk + 1) * TK], preferred_element_type=F32)
         for blk in range(TM_PROJ // TK)], axis=0).astype(BF16)
    a_ref[...] = (jnp.dot(xn, wa_ref[...], preferred_element_type=F32) * cs_ref[...]).astype(BF16)
    kp_ref[...] = jnp.dot(xnp, wk_ref[...], preferred_element_type=F32).astype(BF16)
    vts = lax.dot_general(wvs_ref[...], xnp, _NT, preferred_element_type=F32).astype(BF16)
    for blk in range(TM_PROJ // TK):
        vts_ref[blk] = vts[:, blk * TK:(blk + 1) * TK]
    vtd = lax.dot_general(wvd_ref[...], xn, _NT, preferred_element_type=F32).astype(BF16)
    for blk in range(TM_PROJ // TBK):
        vtd_ref[blk] = vtd[:, blk * TBK:(blk + 1) * TBK]


def _chunk_order_matrix():
    dst = jnp.arange(TK, dtype=I32)
    src = (dst % SUBLANES) * KCH + dst // SUBLANES
    return (src[:, None] == jnp.arange(TK, dtype=I32)[None, :]).astype(BF16)


def _in_proj_tokens(x2d, gain, cs_a, w_a, w_k, w_vs_t, w_vd_t):
    t = x2d.shape[0]
    tm = TM_PROJ
    n_a = w_a.shape[1]
    const = lambda i: (0, 0)
    return pl.pallas_call(
        _inproj_tokens_kernel,
        grid=(t // tm,),
        in_specs=[
            pl.BlockSpec((tm, D_MODEL), lambda i: (i, 0)),
            pl.BlockSpec((1, D_MODEL), const),
            pl.BlockSpec((1, n_a), const),
            pl.BlockSpec((TK, TK), const),
            pl.BlockSpec((D_MODEL, n_a), const),
            pl.BlockSpec((D_MODEL, D_SB), const),
            pl.BlockSpec((D_SB, D_MODEL), const),
            pl.BlockSpec((D_DIFF, D_MODEL), const),
        ],
        out_specs=[
            pl.BlockSpec((tm, n_a), lambda i: (i, 0)),
            pl.BlockSpec((tm, D_SB), lambda i: (i, 0)),
            pl.BlockSpec((tm // TK, D_SB, TK), lambda i: (i, 0, 0)),
            pl.BlockSpec((tm // TBK, D_DIFF, TBK), lambda i: (i, 0, 0)),
        ],
        out_shape=[
            jax.ShapeDtypeStruct((t, n_a), BF16),
            jax.ShapeDtypeStruct((t, D_SB), BF16),
            jax.ShapeDtypeStruct((t // TK, D_SB, TK), BF16),
            jax.ShapeDtypeStruct((t // TBK, D_DIFF, TBK), BF16),
        ],
        compiler_params=_cparams(("parallel",)),
        name="in_proj",
    )(x2d, gain, cs_a, _chunk_order_matrix(), w_a, w_k, w_vs_t, w_vd_t)


def _bias_lookup(rel, rb_ref, h):
    n = jnp.abs(rel)
    n2 = n * n
    large = jnp.full(rel.shape, 8, I32)
    for k in range(1, 8):
        large = large + jnp.where(n2 >= (64 << k), 1, 0)
    bucket = jnp.where(rel > 0, N_BUCKETS // 2, 0) + jnp.where(n < 8, n, large)
    out = jnp.zeros(rel.shape, F32)
    for b in range(N_BUCKETS):
        out = jnp.where(bucket == b, rb_ref[b, h], out)
    return out


def _relbias_kernel(rb_ref, bt_ref, mb_ref):
    h = pl.program_id(0)
    krow = lax.broadcasted_iota(I32, (TBK, TB), 0)
    qcol = lax.broadcasted_iota(I32, (TBK, TB), 1)
    for d in range(N_BIAS_TILES):
        key_off = (KB_PER_Q - 1 - d) * TBK + krow
        visible = (key_off // CHUNK) <= (qcol // CHUNK)
        bt_ref[0, d] = jnp.where(visible, _bias_lookup(key_off - qcol, rb_ref, h), NEG_BIG)
    s = mb_ref.shape[2]
    mrow = lax.broadcasted_iota(I32, (N_META, s), 0)
    qpos = lax.broadcasted_iota(I32, (N_META, s), 1) + N_META
    mb_ref[0] = _bias_lookup(mrow - qpos, rb_ref, h)


def _rel_bias_tiles(rel_bias, s):
    return pl.pallas_call(
        _relbias_kernel,
        grid=(H_DIFF,),
        in_specs=[pl.BlockSpec(memory_space=pltpu.SMEM)],
        out_specs=[
            pl.BlockSpec((1, N_BIAS_TILES, TBK, TB), lambda h: (h, 0, 0, 0)),
            pl.BlockSpec((1, N_META, s), lambda h: (h, 0, 0)),
        ],
        out_shape=[
            jax.ShapeDtypeStruct((H_DIFF, N_BIAS_TILES, TBK, TB), F32),
            jax.ShapeDtypeStruct((H_DIFF, N_META, s), F32),
        ],
        compiler_params=_cparams(("arbitrary",)),
        name="rel_bias",
    )(rel_bias)


def _suffix_incl_sublanes(x):
    r = lax.broadcasted_iota(I32, x.shape, 0)
    for d in (1, 2, 4):
        shifted = pltpu.roll(x, SUBLANES - d, axis=0)
        x = x + jnp.where(r + d < SUBLANES, shifted, 0.0)
    return x


def _softplus(s):
    return jnp.maximum(s, 0.0) + jnp.log(1.0 + jnp.exp(-jnp.abs(s)))


def _head_half(qpair, half):
    lane = lax.broadcasted_iota(I32, qpair.shape, 1)
    keep = (lane >= HEAD_DIM * half) & (lane < HEAD_DIM * (half + 1))
    return jnp.where(keep, qpair, jnp.zeros_like(qpair))


def _sb_block(s, vt, carry, acc, valid):
    sp = _softplus(s)
    if valid is not None:
        sp = jnp.where(valid, sp, 0.0)
    run = jnp.zeros((SUBLANES, s.shape[1]), F32)
    parts = [None] * KCH
    for i in reversed(range(KCH)):
        run = run + sp[SUBLANES * i:SUBLANES * (i + 1), :]
        parts[i] = run
    incl = _suffix_incl_sublanes(run)
    base = (incl - run) + carry
    r_sum = jnp.concatenate([p + base for p in parts], axis=0)
    w = jnp.exp(s - r_sum)
    if valid is not None:
        w = jnp.where(valid, w, 0.0)
    acc = acc + jnp.dot(vt, w.astype(BF16), preferred_element_type=F32)
    return carry + incl[0:1, :], acc


def _sb_meta_block(s, vt, carry, acc):
    sp = _softplus(s)
    lo, hi = sp[0:SUBLANES, :], sp[SUBLANES:2 * SUBLANES, :]
    hi_incl = _suffix_incl_sublanes(hi)
    lo_incl = _suffix_incl_sublanes(lo) + hi_incl[0:1, :]
    r_sum = jnp.concatenate([lo_incl, hi_incl], axis=0) + carry
    w = jnp.exp(s - r_sum)
    return acc + jnp.dot(vt, w.astype(BF16), preferred_element_type=F32)


def _sb_kernel(q_ref, k_ref, vt_ref, mk_ref, mvt_ref, o_ref):
    step = pl.program_id(2)
    row = lax.broadcasted_iota(I32, (TK, TQ), 0)
    lane = lax.broadcasted_iota(I32, (TK, TQ), 1)
    key_off = (row % SUBLANES) * KCH + row // SUBLANES
    causal = key_off < lane

    units = []
    for pp in range(SB_PAIRS):
        for j in range(SB_QBLOCKS):
            qpair = q_ref[TQ * j:TQ * (j + 1), LANES * pp:LANES * (pp + 1)]
            for half in range(2):
                units.append((pp, j, half, step * SB_QBLOCKS + j, _head_half(qpair, half)))

    def scores(u, kb):
        pp, _, _, _, qz = u
        start = pl.multiple_of(kb * TK, TK)
        kblk = k_ref[pl.ds(start, TK), LANES * pp:LANES * (pp + 1)]
        return lax.dot_general(kblk, qz, _NT, preferred_element_type=F32)

    def values(u, kb):
        pp, _, half, _, _ = u
        r0 = 2 * HEAD_DIM * pp + HEAD_DIM * half
        return vt_ref[kb, r0:r0 + HEAD_DIM, :]

    def alive(carry):
        return (jnp.min(carry) < SB_EXIT).astype(I32)

    s_diag = [scores(u, u[3]) for u in units]
    s_prev = [scores(u, jnp.maximum(u[3] - 1, 0)) for u in units]
    carries, accs = [], []
    for u, sd, sp in zip(units, s_diag, s_prev):
        qi = u[3]
        carry = jnp.zeros((1, TQ), F32)
        acc = jnp.zeros((HEAD_DIM, TQ), F32)
        carry, acc = _sb_block(sd, values(u, qi), carry, acc, causal)
        carry2, acc2 = _sb_block(sp, values(u, jnp.maximum(qi - 1, 0)), carry, acc, None)
        carries.append(jnp.where(qi > 0, carry2, carry))
        accs.append(jnp.where(qi > 0, acc2, acc))

    cmin = carries[0]
    for c in carries[1:]:
        cmin = jnp.minimum(cmin, c)

    def slow(accs):
        out = []
        for u, carry, acc in zip(units, carries, accs):
            def cond(st):
                return (st[0] >= 0) & (st[1] > 0)

            def body(st, u=u):
                kb, _, carry, acc = st
                carry, acc = _sb_block(scores(u, kb), values(u, kb), carry, acc, None)
                return kb - 1, alive(carry), carry, acc

            _, live, carry, acc = lax.while_loop(cond, body, (u[3] - 2, alive(carry), carry, acc))

            def meta(acc, u=u, carry=carry):
                pp, _, half, _, qz = u
                sm = lax.dot_general(mk_ref[:, LANES * pp:LANES * (pp + 1)], qz, _NT,
                                     preferred_element_type=F32)
                return _sb_meta_block(sm, mvt_ref[2 * pp + half], carry, acc)

            out.append(lax.cond(live > 0, meta, lambda a: a, acc))
        return tuple(out)

    accs = lax.cond(alive(cmin) > 0, slow, lambda a: a, tuple(accs))
    for idx in range(0, len(units), 2):
        pp, j = units[idx][0], units[idx][1]
        pair_out = jnp.concatenate([accs[idx], accs[idx + 1]], axis=0).T.astype(BF16)
        o_ref[TQ * j:TQ * (j + 1), LANES * pp:LANES * (pp + 1)] = pair_out


def _sb_attention(a_proj, k_perm, vt_perm, mproj, mvt, b, s):
    nkb = s // TK
    nsteps = s // (TQ * SB_QBLOCKS)
    wl = LANES * SB_PAIRS
    mkcol0 = D_SB // wl
    return pl.pallas_call(
        _sb_kernel,
        grid=(b, H_SB // 2 // SB_PAIRS, nsteps),
        in_specs=[
            pl.BlockSpec((TQ * SB_QBLOCKS, wl), lambda bi, p, st: (bi * nsteps + st, p)),
            pl.BlockSpec((s, wl), lambda bi, p, st: (bi, p)),
            pl.BlockSpec((nkb, 2 * HEAD_DIM * SB_PAIRS, TK), lambda bi, p, st: (bi, p, 0)),
            pl.BlockSpec((N_META, wl), lambda bi, p, st: (0, mkcol0 + p)),
            pl.BlockSpec((2 * SB_PAIRS, HEAD_DIM, N_META), lambda bi, p, st: (p, 0, 0)),
        ],
        out_specs=pl.BlockSpec((TQ * SB_QBLOCKS, wl), lambda bi, p, st: (bi * nsteps + st, p)),
        out_shape=jax.ShapeDtypeStruct((b * s, D_SB), BF16),
        compiler_params=_cparams(("parallel", "parallel", "arbitrary")),
        name="sb_attn",
    )(a_proj, k_perm, vt_perm, mproj, mvt)


def _with_ones_rows(vt):
    r = lax.broadcasted_iota(I32, (L_ROWS, vt.shape[1]), 0)
    ones = jnp.where(r == 0, 1.0, 0.0).astype(vt.dtype)
    return jnp.concatenate([vt, ones], axis=0)


def _df_update(s, vt_ext, st):
    m, acc = st
    m_new = jnp.maximum(m, jnp.max(s, axis=0, keepdims=True).astype(F32))
    alpha = jnp.exp(m - m_new)
    p = jnp.exp(s - m_new.astype(BF16))
    acc = alpha * acc + jnp.dot(vt_ext, p, preferred_element_type=F32)
    return m_new, acc


def _df_kernel(q_ref, k_ref, vt_ref, mk_ref, mvt_ref, bt_ref, mb_ref,
               lq1_ref, lk1_ref, lq2_ref, lk2_ref, gain_ref, o_ref):
    qi = pl.program_id(2)
    qpair = q_ref[...]
    qz = [_head_half(qpair, 0), _head_half(qpair, 1)]
    dv = 2 * HEAD_DIM

    kb_last = KB_PER_Q * qi + KB_PER_Q - 1

    def scores(d):
        kb = jnp.maximum(kb_last - d, 0)
        kblk = k_ref[pl.ds(pl.multiple_of(kb * TBK, TBK), TBK), :]
        bias = bt_ref[jnp.minimum(d, N_BIAS_TILES - 1)]
        return tuple((lax.dot_general(kblk, qz[i], _NT, preferred_element_type=F32) + bias).astype(BF16)
                     for i in range(2))

    def init():
        return (jnp.full((1, TB), -jnp.inf, F32), jnp.zeros((dv + L_ROWS, TB), F32))

    def body(d, carry):
        st, s_cur = carry
        s_nxt = scores(d + 1)
        vt = _with_ones_rows(vt_ref[kb_last - d])
        return tuple(_df_update(s_cur[i], vt, st[i]) for i in range(2)), s_nxt

    s_first = scores(0)
    mk = mk_ref[...]
    mvt = _with_ones_rows(mvt_ref[...])
    mb = mb_ref[...]
    st = tuple(
        _df_update((lax.dot_general(mk, qz[i], _NT, preferred_element_type=F32) + mb).astype(BF16),
                   mvt, init())
        for i in range(2))
    st, _ = lax.fori_loop(0, kb_last + 1, body, (st, s_first))

    lam = (jnp.exp(jnp.sum(lq1_ref[...] * lk1_ref[...], axis=-1, keepdims=True))
           - jnp.exp(jnp.sum(lq2_ref[...] * lk2_ref[...], axis=-1, keepdims=True))
           + LAMBDA_INIT)
    acc1, acc2 = st[0][1], st[1][1]
    o = acc1[:dv] / acc1[dv:dv + 1] - lam * (acc2[:dv] / acc2[dv:dv + 1])
    ms = jnp.mean(o * o, axis=0, keepdims=True)
    y = o * lax.rsqrt(ms + SUBLN_EPS) * gain_ref[...] * (1.0 - LAMBDA_INIT)
    o_ref[...] = y.T.astype(BF16)


def _diff_attention(a_proj, vt, mproj, mvt, bias_tiles, meta_bias,
                    lq1, lk1, lq2, lk2, gain_col, b, s):
    nq = s // TB
    nkb = s // TBK
    qcol0 = D_SB // LANES
    kcol0 = (D_SB + D_DIFF) // LANES
    mkcol0 = (3 * D_SB + D_DIFF) // LANES
    lam_spec = pl.BlockSpec((1, HEAD_DIM), lambda bi, h, qi: (0, 0))
    return pl.pallas_call(
        _df_kernel,
        grid=(b, H_DIFF, nq),
        in_specs=[
            pl.BlockSpec((TB, LANES), lambda bi, h, qi: (bi * nq + qi, qcol0 + h)),
            pl.BlockSpec((s, LANES), lambda bi, h, qi: (bi, kcol0 + h)),
            pl.BlockSpec((nkb, 2 * HEAD_DIM, TBK), lambda bi, h, qi: (bi, h, 0)),
            pl.BlockSpec((N_META, LANES), lambda bi, h, qi: (0, mkcol0 + h)),
            pl.BlockSpec((None, 2 * HEAD_DIM, N_META), lambda bi, h, qi: (h, 0, 0)),
            pl.BlockSpec((None, N_BIAS_TILES, TBK, TB), lambda bi, h, qi: (h, 0, 0, 0)),
            pl.BlockSpec((None, N_META, TB), lambda bi, h, qi: (h, 0, qi)),
            lam_spec, lam_spec, lam_spec, lam_spec,
            pl.BlockSpec((2 * HEAD_DIM, 1), lambda bi, h, qi: (0, 0)),
        ],
        out_specs=pl.BlockSpec((TB, LANES), lambda bi, h, qi: (bi * nq + qi, h)),
        out_shape=jax.ShapeDtypeStruct((b * s, D_DIFF), BF16),
        compiler_params=_cparams(("parallel", "parallel", "arbitrary")),
        name="diff_attn",
    )(a_proj, a_proj, vt, mproj, mvt, bias_tiles, meta_bias, lq1, lk1, lq2, lk2, gain_col)


def _outrouter_kernel(x_ref, osb_ref, odf_ref, wo_ref, g_ref, wrt_ref, br_ref, tri_ref,
                      h1_ref, xn_ref, idx_ref, gate_ref, rank_ref, cnt_ref, carry_ref):
    @pl.when(pl.program_id(0) == 0)
    def _():
        carry_ref[...] = jnp.zeros_like(carry_ref)

    tm = x_ref.shape[0]
    mix = jnp.concatenate([osb_ref[...], odf_ref[...]], axis=1)
    h1 = x_ref[...] + jnp.dot(mix, wo_ref[...], preferred_element_type=F32)
    h1_ref[...] = h1
    ms = jnp.mean(h1 * h1, axis=-1, keepdims=True)
    xn = h1 * lax.rsqrt(ms + NORM_EPS) * g_ref[...]
    for c in range(ROW_TILE):
        xn_ref[pl.ds(c, tm, stride=ROW_TILE), :] = xn[:, LANES * c:LANES * (c + 1)]

    logits = lax.dot_general(wrt_ref[...], xn.astype(BF16), _NT, preferred_element_type=F32) + br_ref[...]
    e_iota = lax.broadcasted_iota(I32, logits.shape, 0)
    work = logits
    vals, idxs = [], []
    for _ in range(TOP_K):
        m = jnp.max(work, axis=0, keepdims=True)
        ik = jnp.min(jnp.where(work == m, e_iota, N_EXPERTS), axis=0, keepdims=True)
        vals.append(m)
        idxs.append(ik)
        work = jnp.where(e_iota == ik, -jnp.inf, work)
    exps = [jnp.exp(v - vals[0]) for v in vals]
    den = exps[0] + exps[1] + exps[2] + exps[3]
    onehot = jnp.zeros(logits.shape, F32)
    for ik in idxs:
        onehot = onehot + jnp.where(e_iota == ik, 1.0, 0.0)
    prefix = jnp.dot(onehot.astype(BF16), tri_ref[...], preferred_element_type=F32)
    pos = prefix + carry_ref[:, 0:1]
    for k in range(TOP_K):
        idx_ref[k:k + 1, :] = idxs[k]
        gate_ref[k:k + 1, :] = exps[k] / den
        rank_ref[k:k + 1, :] = jnp.sum(jnp.where(e_iota == idxs[k], pos, 0.0), axis=0,
                                       keepdims=True).astype(I32)
    carry_ref[...] = carry_ref[...] + jnp.sum(onehot, axis=1, keepdims=True)
    cnt_ref[...] = carry_ref[...]


def _out_router(x2d, o_sb, o_df, wo_bf16, gain, wr_t, br_col, tri, tm):
    t = x2d.shape[0]
    const = lambda i: (0, 0)
    return pl.pallas_call(
        _outrouter_kernel,
        grid=(t // tm,),
        in_specs=[
            pl.BlockSpec((tm, D_MODEL), lambda i: (i, 0)),
            pl.BlockSpec((tm, D_SB), lambda i: (i, 0)),
            pl.BlockSpec((tm, D_DIFF), lambda i: (i, 0)),
            pl.BlockSpec((D_SB + D_DIFF, D_MODEL), const),
            pl.BlockSpec((1, D_MODEL), const),
            pl.BlockSpec((N_EXPERTS, D_MODEL), const),
            pl.BlockSpec((N_EXPERTS, 1), const),
            pl.BlockSpec((tm, tm), const),
        ],
        out_specs=[
            pl.BlockSpec((tm, D_MODEL), lambda i: (i, 0)),
            pl.BlockSpec((tm * ROW_TILE, LANES), lambda i: (i, 0)),
            pl.BlockSpec((TOP_K, tm), lambda i: (0, i)),
            pl.BlockSpec((TOP_K, tm), lambda i: (0, i)),
            pl.BlockSpec((TOP_K, tm), lambda i: (0, i)),
            pl.BlockSpec((N_EXPERTS, LANES), const),
        ],
        out_shape=[
            jax.ShapeDtypeStruct((t, D_MODEL), F32),
            jax.ShapeDtypeStruct((t * ROW_TILE, LANES), F32),
            jax.ShapeDtypeStruct((TOP_K, t), I32),
            jax.ShapeDtypeStruct((TOP_K, t), F32),
            jax.ShapeDtypeStruct((TOP_K, t), I32),
            jax.ShapeDtypeStruct((N_EXPERTS, LANES), F32),
        ],
        scratch_shapes=[pltpu.VMEM((N_EXPERTS, LANES), F32)],
        compiler_params=_cparams(("arbitrary",)),
        name="out_router",
    )(x2d, o_sb, o_df, wo_bf16, gain, wr_t, br_col, tri)


def _dest_kernel(ps_ref, idx_ref, rank_ref, dest_ref):
    idx = idx_ref[...]
    off = jnp.zeros(idx.shape, I32)
    for e in range(N_EXPERTS):
        off = jnp.where(idx == e, ps_ref[e], off)
    dest_ref[...] = rank_ref[...] + off


def _route_dest(pad_start, idx, rank):
    t = idx.shape[1]
    tt = min(t, 8192)
    grid_spec = pltpu.PrefetchScalarGridSpec(
        num_scalar_prefetch=1,
        grid=(t // tt,),
        in_specs=[pl.BlockSpec((TOP_K, tt), lambda i, ps: (0, i)),
                  pl.BlockSpec((TOP_K, tt), lambda i, ps: (0, i))],
        out_specs=pl.BlockSpec((TOP_K, tt), lambda i, ps: (0, i)),
    )
    return pl.pallas_call(
        _dest_kernel,
        grid_spec=grid_spec,
        out_shape=jax.ShapeDtypeStruct((TOP_K, t), I32),
        compiler_params=_cparams(("parallel",)),
        name="route_dest",
    )(pad_start, idx, rank)


def _zero_fill_padding(pe_ref, nu_ref, xs_hbm, zbuf, zsem, first_tail_block):
    zbuf[...] = jnp.zeros_like(zbuf)
    conds, copies = [], []
    for e in range(N_EXPERTS):
        prev_end = pe_ref[e - 1] if e > 0 else 0
        conds.append(pe_ref[e] > prev_end)
        start = jnp.maximum(pe_ref[e] - TM_FFN, 0)
        copies.append(pltpu.make_async_copy(zbuf, xs_hbm.at[pl.ds(start, TM_FFN)], zsem))
    for j in range(N_EXPERTS):
        blk = first_tail_block + j
        conds.append(blk >= nu_ref[0])
        copies.append(pltpu.make_async_copy(zbuf, xs_hbm.at[pl.ds(blk * TM_FFN, TM_FFN)], zsem))
    for cond, c in zip(conds, copies):
        pl.when(cond)(c.start)
    for cond, c in zip(conds, copies):
        pl.when(cond)(c.wait)


def _dispatch_kernel(pe_ref, nu_ref, dest_ref, xn_hbm, xs_hbm, zbuf, ring, lsem, sem, zsem):
    step = pl.program_id(0)
    n = pl.num_programs(0)
    first_tail_block = xs_hbm.shape[0] // TM_FFN - N_EXPERTS

    def load(tile, slot):
        return pltpu.make_async_copy(xn_hbm.at[pl.ds(tile * TD, TD)], ring.at[slot], lsem.at[slot])

    @pl.when(step == 0)
    def _():
        load(0, 0).start()

        @pl.when(n > 1)
        def _():
            load(1, 1).start()

        _zero_fill_padding(pe_ref, nu_ref, xs_hbm, zbuf, zsem, first_tail_block)

    def drain(slot):
        def body(_, carry):
            for _ in range(DRAIN_UNROLL):
                pltpu.make_async_copy(ring.at[0, 0], xs_hbm.at[0], sem.at[slot]).wait()
            return carry

        lax.fori_loop(0, TD * TOP_K // DRAIN_UNROLL, body, 0)

    for slot in range(RING):
        @pl.when(step % RING == slot)
        def _(slot=slot):
            prev = (slot + RING - 1) % RING
            load(step, slot).wait()

            def issue(r, carry):
                for k in range(TOP_K):
                    pltpu.make_async_copy(
                        ring.at[slot, r], xs_hbm.at[dest_ref[k, r]], sem.at[slot]).start(priority=k % 2)
                return carry

            lax.fori_loop(0, TD, issue, 0, unroll=8)

            @pl.when(step > 0)
            def _():
                drain(prev)

            @pl.when(step + 2 < n)
            def _():
                load(step + 2, prev).start()

            @pl.when(step == n - 1)
            def _():
                drain(slot)


def _dispatch(pad_end, n_used, dest, xn3, a_pad):
    t = dest.shape[1]
    grid_spec = pltpu.PrefetchScalarGridSpec(
        num_scalar_prefetch=2,
        grid=(t // TD,),
        in_specs=[
            pl.BlockSpec((TOP_K, TD), lambda i, pe, nu: (0, i), memory_space=pltpu.SMEM),
            pl.BlockSpec(memory_space=pl.ANY),
        ],
        out_specs=pl.BlockSpec(memory_space=pl.ANY),
        scratch_shapes=[
            pltpu.VMEM((TM_FFN, ROW_TILE, LANES), F32),
            pltpu.VMEM((RING, TD, ROW_TILE, LANES), F32),
            pltpu.SemaphoreType.DMA((RING,)),
            pltpu.SemaphoreType.DMA((RING,)),
            pltpu.SemaphoreType.DMA(()),
        ],
    )
    return pl.pallas_call(
        _dispatch_kernel,
        grid_spec=grid_spec,
        out_shape=jax.ShapeDtypeStruct((a_pad, ROW_TILE, LANES), F32),
        compiler_params=_cparams(("arbitrary",)),
        name="dispatch",
    )(pad_end, n_used, dest, xn3)


def _ffn_kernel(be_ref, new_ref, nu_ref, xs_ref, w1_ref, b1_ref, w2_ref, b2_ref, y_ref, w1b, w2b):
    i = pl.program_id(0)

    @pl.when(i >= nu_ref[0])
    def _():
        y_ref[...] = jnp.zeros_like(y_ref)

    @pl.when((i < nu_ref[0]) & (new_ref[i] > 0))
    def _():
        w1b[...] = w1_ref[...].astype(BF16)
        w2b[...] = w2_ref[...].astype(BF16)

    @pl.when(i < nu_ref[0])
    def _():
        x = jnp.concatenate(
            [xs_ref[pl.ds(c, TM_FFN, stride=ROW_TILE), :] for c in range(ROW_TILE)], axis=1).astype(BF16)
        hu = jnp.dot(x, w1b[...], preferred_element_type=F32) + b1_ref[...]
        gate = jnp.minimum(hu[:, :D_FF], SWIGLU_LIMIT)
        lin = jnp.clip(hu[:, D_FF:], -SWIGLU_LIMIT, SWIGLU_LIMIT)
        act = gate * jax.nn.sigmoid(SWIGLU_ALPHA * gate) * (lin + 1.0)
        y = jnp.dot(act.astype(BF16), w2b[...], preferred_element_type=F32) + b2_ref[...]
        for c in range(ROW_TILE):
            y_ref[pl.ds(c, TM_FFN, stride=ROW_TILE), :] = y[:, LANES * c:LANES * (c + 1)]


def _expert_ffn(blk_e, blk_new, n_used, xs2d, w1, b1, w2, b2, n_blocks):
    rows = TM_FFN * ROW_TILE

    def xmap(i, be, new, nu):
        return (jnp.minimum(i, nu[0] - 1), 0)

    def wmap(i, be, new, nu):
        return (be[i], 0, 0)

    grid_spec = pltpu.PrefetchScalarGridSpec(
        num_scalar_prefetch=3,
        grid=(n_blocks,),
        in_specs=[
            pl.BlockSpec((rows, LANES), xmap),
            pl.BlockSpec((None, D_MODEL, 2 * D_FF), wmap),
            pl.BlockSpec((None, 1, 2 * D_FF), wmap),
            pl.BlockSpec((None, D_FF, D_MODEL), wmap),
            pl.BlockSpec((None, 1, D_MODEL), wmap),
        ],
        out_specs=pl.BlockSpec((rows, LANES), lambda i, be, new, nu: (i, 0)),
        scratch_shapes=[pltpu.VMEM((D_MODEL, 2 * D_FF), BF16), pltpu.VMEM((D_FF, D_MODEL), BF16)],
    )
    return pl.pallas_call(
        _ffn_kernel,
        grid_spec=grid_spec,
        out_shape=jax.ShapeDtypeStruct(xs2d.shape, F32),
        compiler_params=_cparams(("arbitrary",)),
        name="expert_ffn",
    )(blk_e, blk_new, n_used, xs2d, w1, b1, w2, b2)


def _combine_gather(dest_ref, y_hbm, buf, sem, slot):
    def issue(r, carry):
        for k in range(TOP_K):
            row0 = pl.multiple_of(dest_ref[k, r] * ROW_TILE, ROW_TILE)
            dst0 = pl.multiple_of(((slot * TOP_K + k) * TD + r) * ROW_TILE, ROW_TILE)
            pltpu.make_async_copy(
                y_hbm.at[pl.ds(row0, ROW_TILE)], buf.at[pl.ds(dst0, ROW_TILE)],
                sem.at[slot]).start(priority=k % 2)
        return carry

    lax.fori_loop(0, TD, issue, 0, unroll=8)


def _combine_drain(y_hbm, buf, sem, slot):
    def drain(_, carry):
        for _ in range(DRAIN_UNROLL):
            pltpu.make_async_copy(
                y_hbm.at[pl.ds(0, ROW_TILE)], buf.at[pl.ds(0, ROW_TILE)], sem.at[slot]).wait()
        return carry

    lax.fori_loop(0, TD * TOP_K // DRAIN_UNROLL, drain, 0)


def _combine_kernel(dcur_ref, dnext_ref, y_hbm, gates_ref, h1_ref, g_ref, o_ref, buf, sem):
    i = pl.program_id(0)
    n = pl.num_programs(0)

    @pl.when(i == 0)
    def _():
        _combine_gather(dcur_ref, y_hbm, buf, sem, 0)

    for slot in range(2):
        @pl.when((i % 2 == slot) & (i + 1 < n))
        def _(slot=slot):
            _combine_gather(dnext_ref, y_hbm, buf, sem, 1 - slot)

    for slot in range(2):
        @pl.when(i % 2 == slot)
        def _(slot=slot):
            _combine_drain(y_hbm, buf, sem, slot)
            acc = h1_ref[...]
            gates = gates_ref[...]
            for k in range(TOP_K):
                base = (slot * TOP_K + k) * TD * ROW_TILE
                yk = jnp.concatenate(
                    [buf[pl.ds(base + c, TD, stride=ROW_TILE), :] for c in range(ROW_TILE)], axis=1)
                acc = acc + yk * gates[:, k:k + 1]
            ms = jnp.mean(acc * acc, axis=-1, keepdims=True)
            o_ref[...] = acc * lax.rsqrt(ms + NORM_EPS) * g_ref[...]


def _combine(dest, y2d, gates_t, h1, gain):
    t = h1.shape[0]
    n = t // TD
    return pl.pallas_call(
        _combine_kernel,
        grid=(n,),
        in_specs=[
            pl.BlockSpec((TOP_K, TD), lambda i: (0, i), memory_space=pltpu.SMEM),
            pl.BlockSpec((TOP_K, TD), lambda i: (0, jnp.minimum(i + 1, n - 1)), memory_space=pltpu.SMEM),
            pl.BlockSpec(memory_space=pl.ANY),
            pl.BlockSpec((TD, TOP_K), lambda i: (i, 0)),
            pl.BlockSpec((TD, D_MODEL), lambda i: (i, 0)),
            pl.BlockSpec((1, D_MODEL), lambda i: (0, 0)),
        ],
        out_specs=pl.BlockSpec((TD, D_MODEL), lambda i: (i, 0)),
        out_shape=jax.ShapeDtypeStruct((t, D_MODEL), F32),
        scratch_shapes=[
            pltpu.VMEM((2 * TOP_K * TD * ROW_TILE, LANES), F32),
            pltpu.SemaphoreType.DMA((2,)),
        ],
        compiler_params=_cparams(("arbitrary",)),
        name="combine",
    )(dest, dest, y2d, gates_t, h1, gain)


def _block_plan(counts, n_blocks):
    padded = (counts + TM_FFN - 1) // TM_FFN * TM_FFN
    pad_end = jnp.cumsum(padded)
    pad_start = pad_end - padded
    blk_start = jnp.arange(n_blocks, dtype=I32) * TM_FFN
    blk_e = jnp.sum((pad_end[None, :] <= blk_start[:, None]).astype(I32), axis=1)
    blk_e = jnp.minimum(blk_e, N_EXPERTS - 1)
    blk_new = jnp.concatenate([jnp.ones((1,), I32), (blk_e[1:] != blk_e[:-1]).astype(I32)])
    n_used = (pad_end[-1:] // TM_FFN).astype(I32)
    return pad_start.astype(I32), pad_end.astype(I32), blk_e, blk_new, n_used


@jax.jit
def _forward(x, meta_tokens, rel_bias, attn_norm, w_in, w_out, lambda_q1, lambda_k1, lambda_q2,
             lambda_k2, subln_gain, ffn_norm, w_router, b_router, w1, b1, w2, b2, final_norm):
    b, s, _ = x.shape
    t = b * s
    assert TQ == TK and TM_PROJ % TK == 0 and TM_PROJ % TBK == 0 and TB % TBK == 0
    assert s % (TQ * SB_QBLOCKS) == 0 and s % TB == 0 and t % TM_PROJ == 0 and t % TD == 0
    x2d = x.reshape(t, D_MODEL)

    scale = HEAD_DIM ** -0.5
    c_sbk, c_sbv, c_dfq, c_dfk, c_dfv = D_SB, 2 * D_SB, 3 * D_SB, 3 * D_SB + D_DIFF, 3 * D_SB + 2 * D_DIFF
    w_in_b = w_in[0].astype(BF16)
    g_attn = attn_norm[0][None, :]
    colscale = jnp.ones((D_IN,), F32).at[0:D_SB].set(scale).at[c_dfq:c_dfk].set(scale)[None, :]
    w_a = jnp.concatenate([w_in_b[:, :c_sbk], w_in_b[:, c_dfq:c_dfv]], axis=1)
    cs_a = jnp.concatenate([colscale[:, :c_sbk], colscale[:, c_dfq:c_dfv]], axis=1)
    a_proj, k_perm, vt_sb, vt_df = _in_proj_tokens(
        x2d, g_attn, cs_a, w_a, w_in_b[:, c_sbk:c_sbv], w_in_b[:, c_sbv:c_dfq].T, w_in_b[:, c_dfv:].T)
    mproj = _in_proj_meta(meta_tokens, g_attn, colscale, w_in_b)
    mvt_sb = mproj[:, c_sbv:c_dfq].reshape(N_META, H_SB, HEAD_DIM).transpose(1, 2, 0)
    mvt_df = mproj[:, c_dfv:].reshape(N_META, H_DIFF, 2 * HEAD_DIM).transpose(1, 2, 0)

    o_sb = _sb_attention(a_proj, k_perm, vt_sb, mproj, mvt_sb, b, s)

    bias_tiles, meta_bias = _rel_bias_tiles(rel_bias, s)
    o_df = _diff_attention(a_proj, vt_df, mproj, mvt_df, bias_tiles, meta_bias,
                           lambda_q1, lambda_k1, lambda_q2, lambda_k2,
                           subln_gain[0][:, None], b, s)

    tri = jnp.triu(jnp.ones((TM_PROJ, TM_PROJ), BF16), k=1)
    h1, xn2d, idx, gates, rank, cnt = _out_router(
        x2d, o_sb, o_df, w_out[0].astype(BF16), ffn_norm[0][None, :],
        w_router[0].T.astype(BF16), b_router[0][:, None], tri, TM_PROJ)

    a = t * TOP_K
    n_blocks = a // TM_FFN + N_EXPERTS
    a_pad = n_blocks * TM_FFN
    counts = cnt[:, 0].astype(I32)
    pad_start, pad_end, blk_e, blk_new, n_used = _block_plan(counts, n_blocks)
    dest = _route_dest(pad_start, idx, rank)

    xs = _dispatch(pad_end, n_used, dest, xn2d.reshape(t, ROW_TILE, LANES), a_pad)
    y2d = _expert_ffn(blk_e, blk_new, n_used, xs.reshape(a_pad * ROW_TILE, LANES),
                      w1[0], b1[0][:, None, :], w2[0], b2[0][:, None, :], n_blocks)
    out = _combine(dest, y2d, gates.T, h1, final_norm[None, :])
    return out.reshape(b, s, D_MODEL)


def kernel(x, meta_tokens, rel_bias, attn_norm, w_in, w_out, lambda_q1, lambda_k1, lambda_q2,
           lambda_k2, subln_gain, ffn_norm, w_router, b_router, w1, b1, w2, b2, final_norm):
    return _forward(x, meta_tokens, rel_bias, attn_norm, w_in, w_out, lambda_q1, lambda_k1,
                    lambda_q2, lambda_k2, subln_gain, ffn_norm, w_router, b_router, w1, b1, w2, b2,
                    final_norm)
```

```python
import functools
import math

import jax
import jax.numpy as jnp
from jax import lax
from jax.experimental import pallas as pl
from jax.experimental.pallas import tpu as pltpu

D_MODEL = 1024
N_META = 16
CHUNK = 64
HEAD_DIM = 64
H_SB = 8
H_DIFF = 4
D_SB = H_SB * HEAD_DIM
D_DIFF = H_DIFF * 2 * HEAD_DIM
D_IN = 3 * D_SB + 3 * D_DIFF
N_BUCKETS = 32
N_EXPERTS = 32
TOP_K = 4
D_FF = D_MODEL
SWIGLU_ALPHA = 1.702
SWIGLU_LIMIT = 7.0
NORM_EPS = 1e-6
SUBLN_EPS = 1e-5
NEG_BIG = -1e30
LAMBDA_INIT = 0.8 - 0.6 * math.exp(-0.3 * 0)

LANES = 128
SUBLANES = 8
ROW_TILE = D_MODEL // LANES
VMEM_LIMIT = 56 * 1024 * 1024

TM_PROJ = 512
TQ = 256
TK = 256
KCH = TK // SUBLANES
SB_PAIRS = 4
SB_QBLOCKS = 2
TB = 512
TBK = 512
KB_PER_Q = TB // TBK
DF_HEADS = 2
L_ROWS = 16
TM_FFN = 512
TD = 256
DRAIN_UNROLL = 128
RING = 3
N_BIAS_TILES = KB_PER_Q + 2
SB_EXIT = 104.0

F32 = jnp.float32
BF16 = jnp.bfloat16
I32 = jnp.int32

_NT = (((1,), (1,)), ((), ()))


def _cparams(sem, vmem=VMEM_LIMIT):
    return pltpu.CompilerParams(dimension_semantics=sem, vmem_limit_bytes=vmem)


def _inproj_kernel(x_ref, g_ref, cs_ref, w_ref, o_ref):
    x = x_ref[...]
    ms = jnp.mean(x * x, axis=-1, keepdims=True)
    xn = (x * lax.rsqrt(ms + NORM_EPS) * g_ref[...]).astype(BF16)
    y = jnp.dot(xn, w_ref[...], preferred_element_type=F32)
    o_ref[...] = (y * cs_ref[...]).astype(BF16)


def _in_proj_meta(x2d, gain, colscale, w_bf16):
    t = x2d.shape[0]
    return pl.pallas_call(
        _inproj_kernel,
        grid=(1,),
        in_specs=[
            pl.BlockSpec((t, D_MODEL), lambda i: (0, 0)),
            pl.BlockSpec((1, D_MODEL), lambda i: (0, 0)),
            pl.BlockSpec((1, D_IN), lambda i: (0, 0)),
            pl.BlockSpec((D_MODEL, D_IN), lambda i: (0, 0)),
        ],
        out_specs=pl.BlockSpec((t, D_IN), lambda i: (0, 0)),
        out_shape=jax.ShapeDtypeStruct((t, D_IN), BF16),
        compiler_params=_cparams(("arbitrary",)),
        name="in_proj_meta",
    )(x2d, gain, colscale, w_bf16)


def _rms_bf16(x, g):
    ms = jnp.mean(x * x, axis=-1, keepdims=True)
    return (x * lax.rsqrt(ms + NORM_EPS) * g).astype(BF16)


def _inproj_tokens_kernel(x_ref, g_ref, cs_ref, perm_ref, wa_ref, wk_ref, wvs_ref, wvd_ref,
                          a_ref, kp_ref, vts_ref, vtd_ref):
    xn = _rms_bf16(x_ref[...], g_ref[...])
    xnp = jnp.concatenate(
        [jnp.dot(perm_ref[...], xn[blk * TK:(blk + 1) * TK], preferred_element_type=F32)
         for blk in range(TM_PROJ // TK)], axis=0).astype(BF16)
    a_ref[...] = (jnp.dot(xn, wa_ref[...], preferred_element_type=F32) * cs_ref[...]).astype(BF16)
    kp_ref[...] = jnp.dot(xnp, wk_ref[...], preferred_element_type=F32).astype(BF16)
    vts = lax.dot_general(wvs_ref[...], xnp, _NT, preferred_element_type=F32).astype(BF16)
    for blk in range(TM_PROJ // TK):
        vts_ref[blk] = vts[:, blk * TK:(blk + 1) * TK]
    vtd = lax.dot_general(wvd_ref[...], xn, _NT, preferred_element_type=F32).astype(BF16)
    for blk in range(TM_PROJ // TBK):
        vtd_ref[blk] = vtd[:, blk * TBK:(blk + 1) * TBK]


def _chunk_order_matrix():
    dst = jnp.arange(TK, dtype=I32)
    src = (dst % SUBLANES) * KCH + dst // SUBLANES
    return (src[:, None] == jnp.arange(TK, dtype=I32)[None, :]).astype(BF16)


def _in_proj_tokens(x2d, gain, cs_a, w_a, w_k, w_vs_t, w_vd_t):
    t = x2d.shape[0]
    tm = TM_PROJ
    n_a = w_a.shape[1]
    const = lambda i: (0, 0)
    return pl.pallas_call(
        _inproj_tokens_kernel,
        grid=(t // tm,),
        in_specs=[
            pl.BlockSpec((tm, D_MODEL), lambda i: (i, 0)),
            pl.BlockSpec((1, D_MODEL), const),
            pl.BlockSpec((1, n_a), const),
            pl.BlockSpec((TK, TK), const),
            pl.BlockSpec((D_MODEL, n_a), const),
            pl.BlockSpec((D_MODEL, D_SB), const),
            pl.BlockSpec((D_SB, D_MODEL), const),
            pl.BlockSpec((D_DIFF, D_MODEL), const),
        ],
        out_specs=[
            pl.BlockSpec((tm, n_a), lambda i: (i, 0)),
            pl.BlockSpec((tm, D_SB), lambda i: (i, 0)),
            pl.BlockSpec((tm // TK, D_SB, TK), lambda i: (i, 0, 0)),
            pl.BlockSpec((tm // TBK, D_DIFF, TBK), lambda i: (i, 0, 0)),
        ],
        out_shape=[
            jax.ShapeDtypeStruct((t, n_a), BF16),
            jax.ShapeDtypeStruct((t, D_SB), BF16),
            jax.ShapeDtypeStruct((t // TK, D_SB, TK), BF16),
            jax.ShapeDtypeStruct((t // TBK, D_DIFF, TBK), BF16),
        ],
        compiler_params=_cparams(("parallel",)),
        name="in_proj",
    )(x2d, gain, cs_a, _chunk_order_matrix(), w_a, w_k, w_vs_t, w_vd_t)


def _bias_lookup(rel, rb_ref, h):
    n = jnp.abs(rel)
    n2 = n * n
    large = jnp.full(rel.shape, 8, I32)
    for k in range(1, 8):
        large = large + jnp.where(n2 >= (64 << k), 1, 0)
    bucket = jnp.where(rel > 0, N_BUCKETS // 2, 0) + jnp.where(n < 8, n, large)
    out = jnp.zeros(rel.shape, F32)
    for b in range(N_BUCKETS):
        out = jnp.where(bucket == b, rb_ref[b, h], out)
    return out


def _relbias_kernel(rb_ref, bt_ref, mb_ref):
    h = pl.program_id(0)
    krow = lax.broadcasted_iota(I32, (TBK, TB), 0)
    qcol = lax.broadcasted_iota(I32, (TBK, TB), 1)
    for d in range(N_BIAS_TILES):
        key_off = (KB_PER_Q - 1 - d) * TBK + krow
        visible = (key_off // CHUNK) <= (qcol // CHUNK)
        bt_ref[0, d] = jnp.where(visible, _bias_lookup(key_off - qcol, rb_ref, h), NEG_BIG)
    s = mb_ref.shape[2]
    mrow = lax.broadcasted_iota(I32, (N_META, s), 0)
    qpos = lax.broadcasted_iota(I32, (N_META, s), 1) + N_META
    mb_ref[0] = _bias_lookup(mrow - qpos, rb_ref, h)


def _rel_bias_tiles(rel_bias, s):
    return pl.pallas_call(
        _relbias_kernel,
        grid=(H_DIFF,),
        in_specs=[pl.BlockSpec(memory_space=pltpu.SMEM)],
        out_specs=[
            pl.BlockSpec((1, N_BIAS_TILES, TBK, TB), lambda h: (h, 0, 0, 0)),
            pl.BlockSpec((1, N_META, s), lambda h: (h, 0, 0)),
        ],
        out_shape=[
            jax.ShapeDtypeStruct((H_DIFF, N_BIAS_TILES, TBK, TB), F32),
            jax.ShapeDtypeStruct((H_DIFF, N_META, s), F32),
        ],
        compiler_params=_cparams(("arbitrary",)),
        name="rel_bias",
    )(rel_bias)


def _suffix_incl_sublanes(x):
    r = lax.broadcasted_iota(I32, x.shape, 0)
    for d in (1, 2, 4):
        shifted = pltpu.roll(x, SUBLANES - d, axis=0)
        x = x + jnp.where(r + d < SUBLANES, shifted, 0.0)
    return x


def _softplus(s):
    return jnp.maximum(s, 0.0) + jnp.log(1.0 + jnp.exp(-jnp.abs(s)))


def _head_half(qpair, half):
    lane = lax.broadcasted_iota(I32, qpair.shape, 1)
    keep = (lane >= HEAD_DIM * half) & (lane < HEAD_DIM * (half + 1))
    return jnp.where(keep, qpair, jnp.zeros_like(qpair))


def _sb_block(s, vt, carry, acc, valid):
    sp = _softplus(s)
    if valid is not None:
        sp = jnp.where(valid, sp, 0.0)
    run = jnp.zeros((SUBLANES, s.shape[1]), F32)
    parts = [None] * KCH
    for i in reversed(range(KCH)):
        run = run + sp[SUBLANES * i:SUBLANES * (i + 1), :]
        parts[i] = run
    incl = _suffix_incl_sublanes(run)
    base = (incl - run) + carry
    r_sum = jnp.concatenate([p + base for p in parts], axis=0)
    w = jnp.exp(s - r_sum)
    if valid is not None:
        w = jnp.where(valid, w, 0.0)
    acc = acc + jnp.dot(vt, w.astype(BF16), preferred_element_type=F32)
    return carry + incl[0:1, :], acc


def _sb_meta_block(s, vt, carry, acc):
    sp = _softplus(s)
    lo, hi = sp[0:SUBLANES, :], sp[SUBLANES:2 * SUBLANES, :]
    hi_incl = _suffix_incl_sublanes(hi)
    lo_incl = _suffix_incl_sublanes(lo) + hi_incl[0:1, :]
    r_sum = jnp.concatenate([lo_incl, hi_incl], axis=0) + carry
    w = jnp.exp(s - r_sum)
    return acc + jnp.dot(vt, w.astype(BF16), preferred_element_type=F32)


def _sb_kernel(q_ref, k_ref, vt_ref, mk_ref, mvt_ref, o_ref):
    step = pl.program_id(2)
    row = lax.broadcasted_iota(I32, (TK, TQ), 0)
    lane = lax.broadcasted_iota(I32, (TK, TQ), 1)
    key_off = (row % SUBLANES) * KCH + row // SUBLANES
    causal = key_off < lane

    units = []
    for pp in range(SB_PAIRS):
        for j in range(SB_QBLOCKS):
            qpair = q_ref[TQ * j:TQ * (j + 1), LANES * pp:LANES * (pp + 1)]
            for half in range(2):
                units.append((pp, j, half, step * SB_QBLOCKS + j, _head_half(qpair, half)))

    def scores(u, kb):
        pp, _, _, _, qz = u
        start = pl.multiple_of(kb * TK, TK)
        kblk = k_ref[pl.ds(start, TK), LANES * pp:LANES * (pp + 1)]
        return lax.dot_general(kblk, qz, _NT, preferred_element_type=F32)

    def values(u, kb):
        pp, _, half, _, _ = u
        r0 = 2 * HEAD_DIM * pp + HEAD_DIM * half
        return vt_ref[kb, r0:r0 + HEAD_DIM, :]

    def alive(carry):
        return (jnp.min(carry) < SB_EXIT).astype(I32)

    s_diag = [scores(u, u[3]) for u in units]
    s_prev = [scores(u, jnp.maximum(u[3] - 1, 0)) for u in units]
    carries, accs = [], []
    for u, sd, sp in zip(units, s_diag, s_prev):
        qi = u[3]
        carry = jnp.zeros((1, TQ), F32)
        acc = jnp.zeros((HEAD_DIM, TQ), F32)
        carry, acc = _sb_block(sd, values(u, qi), carry, acc, causal)
        carry2, acc2 = _sb_block(sp, values(u, jnp.maximum(qi - 1, 0)), carry, acc, None)
        carries.append(jnp.where(qi > 0, carry2, carry))
        accs.append(jnp.where(qi > 0, acc2, acc))

    cmin = carries[0]
    for c in carries[1:]:
        cmin = jnp.minimum(cmin, c)

    def slow(accs):
        out = []
        for u, carry, acc in zip(units, carries, accs):
            def cond(st):
                return (st[0] >= 0) & (st[1] > 0)

            def body(st, u=u):
                kb, _, carry, acc = st
                carry, acc = _sb_block(scores(u, kb), values(u, kb), carry, acc, None)
                return kb - 1, alive(carry), carry, acc

            _, live, carry, acc = lax.while_loop(cond, body, (u[3] - 2, alive(carry), carry, acc))

            def meta(acc, u=u, carry=carry):
                pp, _, half, _, qz = u
                sm = lax.dot_general(mk_ref[:, LANES * pp:LANES * (pp + 1)], qz, _NT,
                                     preferred_element_type=F32)
                return _sb_meta_block(sm, mvt_ref[2 * pp + half], carry, acc)

            out.append(lax.cond(live > 0, meta, lambda a: a, acc))
        return tuple(out)

    accs = lax.cond(alive(cmin) > 0, slow, lambda a: a, tuple(accs))
    for idx in range(0, len(units), 2):
        pp, j = units[idx][0], units[idx][1]
        pair_out = jnp.concatenate([accs[idx], accs[idx + 1]], axis=0).T.astype(BF16)
        o_ref[TQ * j:TQ * (j + 1), LANES * pp:LANES * (pp + 1)] = pair_out


def _sb_attention(a_proj, k_perm, vt_perm, mproj, mvt, b, s):
    nkb = s // TK
    nsteps = s // (TQ * SB_QBLOCKS)
    wl = LANES * SB_PAIRS
    mkcol0 = D_SB // wl
    return pl.pallas_call(
        _sb_kernel,
        grid=(b, H_SB // 2 // SB_PAIRS, nsteps),
        in_specs=[
            pl.BlockSpec((TQ * SB_QBLOCKS, wl), lambda bi, p, st: (bi * nsteps + st, p)),
            pl.BlockSpec((s, wl), lambda bi, p, st: (bi, p)),
            pl.BlockSpec((nkb, 2 * HEAD_DIM * SB_PAIRS, TK), lambda bi, p, st: (bi, p, 0)),
            pl.BlockSpec((N_META, wl), lambda bi, p, st: (0, mkcol0 + p)),
            pl.BlockSpec((2 * SB_PAIRS, HEAD_DIM, N_META), lambda bi, p, st: (p, 0, 0)),
        ],
        out_specs=pl.BlockSpec((TQ * SB_QBLOCKS, wl), lambda bi, p, st: (bi * nsteps + st, p)),
        out_shape=jax.ShapeDtypeStruct((b * s, D_SB), BF16),
        compiler_params=_cparams(("parallel", "parallel", "arbitrary")),
        name="sb_attn",
    )(a_proj, k_perm, vt_perm, mproj, mvt)


def _with_ones_rows(vt):
    r = lax.broadcasted_iota(I32, (L_ROWS, vt.shape[1]), 0)
    ones = jnp.where(r == 0, 1.0, 0.0).astype(vt.dtype)
    return jnp.concatenate([vt, ones], axis=0)


def _df_update(s, vt_ext, st):
    m, acc = st
    m_new = jnp.maximum(m, jnp.max(s, axis=0, keepdims=True).astype(F32))
    alpha = jnp.exp(m - m_new)
    p = jnp.exp(s - m_new.astype(BF16))
    acc = alpha * acc + jnp.dot(vt_ext, p, preferred_element_type=F32)
    return m_new, acc


def _df_kernel(q_ref, k_ref, vt_ref, mk_ref, mvt_ref, bt_ref, mb_ref,
               lq1_ref, lk1_ref, lq2_ref, lk2_ref, gain_ref, o_ref):
    qi = pl.program_id(2)
    dv = 2 * HEAD_DIM
    chains = []
    for hh in range(DF_HEADS):
        qpair = q_ref[:, LANES * hh:LANES * (hh + 1)]
        chains += [(hh, _head_half(qpair, 0)), (hh, _head_half(qpair, 1))]

    kb_last = KB_PER_Q * qi + KB_PER_Q - 1

    def scores(d):
        kb = jnp.maximum(kb_last - d, 0)
        rows = pl.ds(pl.multiple_of(kb * TBK, TBK), TBK)
        tile = jnp.minimum(d, N_BIAS_TILES - 1)
        return tuple(
            (lax.dot_general(k_ref[rows, LANES * hh:LANES * (hh + 1)], qz, _NT,
                             preferred_element_type=F32) + bt_ref[hh, tile]).astype(BF16)
            for hh, qz in chains)

    def values(hh, kb):
        return _with_ones_rows(vt_ref[kb, dv * hh:dv * (hh + 1), :])

    def init():
        return (jnp.full((1, TB), -jnp.inf, F32), jnp.zeros((dv + L_ROWS, TB), F32))

    def body(d, carry):
        st, s_cur = carry
        s_nxt = scores(d + 1)
        vts = [values(hh, kb_last - d) for hh in range(DF_HEADS)]
        return tuple(_df_update(s_cur[c], vts[hh], st[c]) for c, (hh, _) in enumerate(chains)), s_nxt

    s_first = scores(0)
    st = tuple(
        _df_update((lax.dot_general(mk_ref[:, LANES * hh:LANES * (hh + 1)], qz, _NT,
                                    preferred_element_type=F32) + mb_ref[hh]).astype(BF16),
                   _with_ones_rows(mvt_ref[hh]), init())
        for hh, qz in chains)
    st, _ = lax.fori_loop(0, kb_last + 1, body, (st, s_first))

    lam = (jnp.exp(jnp.sum(lq1_ref[...] * lk1_ref[...], axis=-1, keepdims=True))
           - jnp.exp(jnp.sum(lq2_ref[...] * lk2_ref[...], axis=-1, keepdims=True))
           + LAMBDA_INIT)
    for hh in range(DF_HEADS):
        acc1, acc2 = st[2 * hh][1], st[2 * hh + 1][1]
        o = acc1[:dv] / acc1[dv:dv + 1] - lam * (acc2[:dv] / acc2[dv:dv + 1])
        ms = jnp.mean(o * o, axis=0, keepdims=True)
        y = o * lax.rsqrt(ms + SUBLN_EPS) * gain_ref[...] * (1.0 - LAMBDA_INIT)
        o_ref[:, LANES * hh:LANES * (hh + 1)] = y.T.astype(BF16)


def _diff_attention(a_proj, vt, mproj, mvt, bias_tiles, meta_bias,
                    lq1, lk1, lq2, lk2, gain_col, b, s):
    nq = s // TB
    nkb = s // TBK
    qcol0 = D_SB // LANES
    kcol0 = (D_SB + D_DIFF) // LANES
    mkcol0 = (3 * D_SB + D_DIFF) // LANES
    lam_spec = pl.BlockSpec((1, HEAD_DIM), lambda bi, h, qi: (0, 0))
    wl = LANES * DF_HEADS
    qcol0, kcol0, mkcol0 = qcol0 // DF_HEADS, kcol0 // DF_HEADS, mkcol0 // DF_HEADS
    return pl.pallas_call(
        _df_kernel,
        grid=(b, H_DIFF // DF_HEADS, nq),
        in_specs=[
            pl.BlockSpec((TB, wl), lambda bi, h, qi: (bi * nq + qi, qcol0 + h)),
            pl.BlockSpec((s, wl), lambda bi, h, qi: (bi, kcol0 + h)),
            pl.BlockSpec((nkb, 2 * HEAD_DIM * DF_HEADS, TBK), lambda bi, h, qi: (bi, h, 0)),
            pl.BlockSpec((N_META, wl), lambda bi, h, qi: (0, mkcol0 + h)),
            pl.BlockSpec((DF_HEADS, 2 * HEAD_DIM, N_META), lambda bi, h, qi: (h, 0, 0)),
            pl.BlockSpec((DF_HEADS, N_BIAS_TILES, TBK, TB), lambda bi, h, qi: (h, 0, 0, 0)),
            pl.BlockSpec((DF_HEADS, N_META, TB), lambda bi, h, qi: (h, 0, qi)),
            lam_spec, lam_spec, lam_spec, lam_spec,
            pl.BlockSpec((2 * HEAD_DIM, 1), lambda bi, h, qi: (0, 0)),
        ],
        out_specs=pl.BlockSpec((TB, wl), lambda bi, h, qi: (bi * nq + qi, h)),
        out_shape=jax.ShapeDtypeStruct((b * s, D_DIFF), BF16),
        compiler_params=_cparams(("parallel", "parallel", "arbitrary")),
        name="diff_attn",
    )(a_proj, a_proj, vt, mproj, mvt, bias_tiles, meta_bias, lq1, lk1, lq2, lk2, gain_col)


def _outrouter_kernel(x_ref, osb_ref, odf_ref, wo_ref, g_ref, wrt_ref, br_ref, tri_ref,
                      h1_ref, xn_ref, idx_ref, gate_ref, rank_ref, cnt_ref, carry_ref):
    @pl.when(pl.program_id(0) == 0)
    def _():
        carry_ref[...] = jnp.zeros_like(carry_ref)

    tm = x_ref.shape[0]
    mix = jnp.concatenate([osb_ref[...], odf_ref[...]], axis=1)
    h1 = x_ref[...] + jnp.dot(mix, wo_ref[...], preferred_element_type=F32)
    h1_ref[...] = h1
    ms = jnp.mean(h1 * h1, axis=-1, keepdims=True)
    xn = h1 * lax.rsqrt(ms + NORM_EPS) * g_ref[...]
    for c in range(ROW_TILE):
        xn_ref[pl.ds(c, tm, stride=ROW_TILE), :] = xn[:, LANES * c:LANES * (c + 1)]

    logits = lax.dot_general(wrt_ref[...], xn.astype(BF16), _NT, preferred_element_type=F32) + br_ref[...]
    e_iota = lax.broadcasted_iota(I32, logits.shape, 0)
    work = logits
    vals, idxs = [], []
    for _ in range(TOP_K):
        m = jnp.max(work, axis=0, keepdims=True)
        ik = jnp.min(jnp.where(work == m, e_iota, N_EXPERTS), axis=0, keepdims=True)
        vals.append(m)
        idxs.append(ik)
        work = jnp.where(e_iota == ik, -jnp.inf, work)
    exps = [jnp.exp(v - vals[0]) for v in vals]
    den = exps[0] + exps[1] + exps[2] + exps[3]
    onehot = jnp.zeros(logits.shape, F32)
    for ik in idxs:
        onehot = onehot + jnp.where(e_iota == ik, 1.0, 0.0)
    prefix = jnp.dot(onehot.astype(BF16), tri_ref[...], preferred_element_type=F32)
    pos = prefix + carry_ref[:, 0:1]
    for k in range(TOP_K):
        idx_ref[k:k + 1, :] = idxs[k]
        gate_ref[k:k + 1, :] = exps[k] / den
        rank_ref[k:k + 1, :] = jnp.sum(jnp.where(e_iota == idxs[k], pos, 0.0), axis=0,
                                       keepdims=True).astype(I32)
    carry_ref[...] = carry_ref[...] + jnp.sum(onehot, axis=1, keepdims=True)
    cnt_ref[...] = carry_ref[...]


def _out_router(x2d, o_sb, o_df, wo_bf16, gain, wr_t, br_col, tri, tm):
    t = x2d.shape[0]
    const = lambda i: (0, 0)
    return pl.pallas_call(
        _outrouter_kernel,
        grid=(t // tm,),
        in_specs=[
            pl.BlockSpec((tm, D_MODEL), lambda i: (i, 0)),
            pl.BlockSpec((tm, D_SB), lambda i: (i, 0)),
            pl.BlockSpec((tm, D_DIFF), lambda i: (i, 0)),
            pl.BlockSpec((D_SB + D_DIFF, D_MODEL), const),
            pl.BlockSpec((1, D_MODEL), const),
            pl.BlockSpec((N_EXPERTS, D_MODEL), const),
            pl.BlockSpec((N_EXPERTS, 1), const),
            pl.BlockSpec((tm, tm), const),
        ],
        out_specs=[
            pl.BlockSpec((tm, D_MODEL), lambda i: (i, 0)),
            pl.BlockSpec((tm * ROW_TILE, LANES), lambda i: (i, 0)),
            pl.BlockSpec((TOP_K, tm), lambda i: (0, i)),
            pl.BlockSpec((TOP_K, tm), lambda i: (0, i)),
            pl.BlockSpec((TOP_K, tm), lambda i: (0, i)),
            pl.BlockSpec((N_EXPERTS, LANES), const),
        ],
        out_shape=[
            jax.ShapeDtypeStruct((t, D_MODEL), F32),
            jax.ShapeDtypeStruct((t * ROW_TILE, LANES), F32),
            jax.ShapeDtypeStruct((TOP_K, t), I32),
            jax.ShapeDtypeStruct((TOP_K, t), F32),
            jax.ShapeDtypeStruct((TOP_K, t), I32),
            jax.ShapeDtypeStruct((N_EXPERTS, LANES), F32),
        ],
        scratch_shapes=[pltpu.VMEM((N_EXPERTS, LANES), F32)],
        compiler_params=_cparams(("arbitrary",)),
        name="out_router",
    )(x2d, o_sb, o_df, wo_bf16, gain, wr_t, br_col, tri)


def _dest_kernel(ps_ref, idx_ref, rank_ref, dest_ref):
    idx = idx_ref[...]
    off = jnp.zeros(idx.shape, I32)
    for e in range(N_EXPERTS):
        off = jnp.where(idx == e, ps_ref[e], off)
    dest_ref[...] = rank_ref[...] + off


def _route_dest(pad_start, idx, rank):
    t = idx.shape[1]
    tt = min(t, 8192)
    grid_spec = pltpu.PrefetchScalarGridSpec(
        num_scalar_prefetch=1,
        grid=(t // tt,),
        in_specs=[pl.BlockSpec((TOP_K, tt), lambda i, ps: (0, i)),
                  pl.BlockSpec((TOP_K, tt), lambda i, ps: (0, i))],
        out_specs=pl.BlockSpec((TOP_K, tt), lambda i, ps: (0, i)),
    )
    return pl.pallas_call(
        _dest_kernel,
        grid_spec=grid_spec,
        out_shape=jax.ShapeDtypeStruct((TOP_K, t), I32),
        compiler_params=_cparams(("parallel",)),
        name="route_dest",
    )(pad_start, idx, rank)


def _zero_fill_padding(pe_ref, nu_ref, xs_hbm, zbuf, zsem, first_tail_block):
    zbuf[...] = jnp.zeros_like(zbuf)
    conds, copies = [], []
    for e in range(N_EXPERTS):
        prev_end = pe_ref[e - 1] if e > 0 else 0
        conds.append(pe_ref[e] > prev_end)
        start = jnp.maximum(pe_ref[e] - TM_FFN, 0)
        copies.append(pltpu.make_async_copy(zbuf, xs_hbm.at[pl.ds(start, TM_FFN)], zsem))
    for j in range(N_EXPERTS):
        blk = first_tail_block + j
        conds.append(blk >= nu_ref[0])
        copies.append(pltpu.make_async_copy(zbuf, xs_hbm.at[pl.ds(blk * TM_FFN, TM_FFN)], zsem))
    for cond, c in zip(conds, copies):
        pl.when(cond)(c.start)
    for cond, c in zip(conds, copies):
        pl.when(cond)(c.wait)


def _dispatch_kernel(pe_ref, nu_ref, dest_ref, xn_hbm, xs_hbm, zbuf, ring, lsem, sem, zsem):
    step = pl.program_id(0)
    n = pl.num_programs(0)
    first_tail_block = xs_hbm.shape[0] // TM_FFN - N_EXPERTS

    def load(tile, slot):
        return pltpu.make_async_copy(xn_hbm.at[pl.ds(tile * TD, TD)], ring.at[slot], lsem.at[slot])

    @pl.when(step == 0)
    def _():
        load(0, 0).start()

        @pl.when(n > 1)
        def _():
            load(1, 1).start()

        _zero_fill_padding(pe_ref, nu_ref, xs_hbm, zbuf, zsem, first_tail_block)

    def drain(slot):
        def body(_, carry):
            for _ in range(DRAIN_UNROLL):
                pltpu.make_async_copy(ring.at[0, 0], xs_hbm.at[0], sem.at[slot]).wait()
            return carry

        lax.fori_loop(0, TD * TOP_K // DRAIN_UNROLL, body, 0)

    for slot in range(RING):
        @pl.when(step % RING == slot)
        def _(slot=slot):
            prev = (slot + RING - 1) % RING
            load(step, slot).wait()

            def issue(r, carry):
                for k in range(TOP_K):
                    pltpu.make_async_copy(
                        ring.at[slot, r], xs_hbm.at[dest_ref[k, r]], sem.at[slot]).start(priority=k % 2)
                return carry

            lax.fori_loop(0, TD, issue, 0, unroll=8)

            @pl.when(step > 0)
            def _():
                drain(prev)

            @pl.when(step + 2 < n)
            def _():
                load(step + 2, prev).start()

            @pl.when(step == n - 1)
            def _():
                drain(slot)


def _dispatch(pad_end, n_used, dest, xn3, a_pad):
    t = dest.shape[1]
    grid_spec = pltpu.PrefetchScalarGridSpec(
        num_scalar_prefetch=2,
        grid=(t // TD,),
        in_specs=[
            pl.BlockSpec((TOP_K, TD), lambda i, pe, nu: (0, i), memory_space=pltpu.SMEM),
            pl.BlockSpec(memory_space=pl.ANY),
        ],
        out_specs=pl.BlockSpec(memory_space=pl.ANY),
        scratch_shapes=[
            pltpu.VMEM((TM_FFN, ROW_TILE, LANES), F32),
            pltpu.VMEM((RING, TD, ROW_TILE, LANES), F32),
            pltpu.SemaphoreType.DMA((RING,)),
            pltpu.SemaphoreType.DMA((RING,)),
            pltpu.SemaphoreType.DMA(()),
        ],
    )
    return pl.pallas_call(
        _dispatch_kernel,
        grid_spec=grid_spec,
        out_shape=jax.ShapeDtypeStruct((a_pad, ROW_TILE, LANES), F32),
        compiler_params=_cparams(("arbitrary",)),
        name="dispatch",
    )(pad_end, n_used, dest, xn3)


def _ffn_kernel(be_ref, new_ref, nu_ref, xs_ref, w1_ref, b1_ref, w2_ref, b2_ref, y_ref, w1b, w2b):
    i = pl.program_id(0)

    @pl.when(i >= nu_ref[0])
    def _():
        y_ref[...] = jnp.zeros_like(y_ref)

    @pl.when((i < nu_ref[0]) & (new_ref[i] > 0))
    def _():
        w1b[...] = w1_ref[...].astype(BF16)
        w2b[...] = w2_ref[...].astype(BF16)

    @pl.when(i < nu_ref[0])
    def _():
        x = jnp.concatenate(
            [xs_ref[pl.ds(c, TM_FFN, stride=ROW_TILE), :] for c in range(ROW_TILE)], axis=1).astype(BF16)
        hu = jnp.dot(x, w1b[...], preferred_element_type=F32) + b1_ref[...]
        gate = jnp.minimum(hu[:, :D_FF], SWIGLU_LIMIT)
        lin = jnp.clip(hu[:, D_FF:], -SWIGLU_LIMIT, SWIGLU_LIMIT)
        act = gate * jax.nn.sigmoid(SWIGLU_ALPHA * gate) * (lin + 1.0)
        y = jnp.dot(act.astype(BF16), w2b[...], preferred_element_type=F32) + b2_ref[...]
        for c in range(ROW_TILE):
            y_ref[pl.ds(c, TM_FFN, stride=ROW_TILE), :] = y[:, LANES * c:LANES * (c + 1)]


def _expert_ffn(blk_e, blk_new, n_used, xs2d, w1, b1, w2, b2, n_blocks):
    rows = TM_FFN * ROW_TILE

    def xmap(i, be, new, nu):
        return (jnp.minimum(i, nu[0] - 1), 0)

    def wmap(i, be, new, nu):
        return (be[i], 0, 0)

    grid_spec = pltpu.PrefetchScalarGridSpec(
        num_scalar_prefetch=3,
        grid=(n_blocks,),
        in_specs=[
            pl.BlockSpec((rows, LANES), xmap),
            pl.BlockSpec((None, D_MODEL, 2 * D_FF), wmap),
            pl.BlockSpec((None, 1, 2 * D_FF), wmap),
            pl.BlockSpec((None, D_FF, D_MODEL), wmap),
            pl.BlockSpec((None, 1, D_MODEL), wmap),
        ],
        out_specs=pl.BlockSpec((rows, LANES), lambda i, be, new, nu: (i, 0)),
        scratch_shapes=[pltpu.VMEM((D_MODEL, 2 * D_FF), BF16), pltpu.VMEM((D_FF, D_MODEL), BF16)],
    )
    return pl.pallas_call(
        _ffn_kernel,
        grid_spec=grid_spec,
        out_shape=jax.ShapeDtypeStruct(xs2d.shape, F32),
        compiler_params=_cparams(("arbitrary",)),
        name="expert_ffn",
    )(blk_e, blk_new, n_used, xs2d, w1, b1, w2, b2)


def _combine_gather(dest_ref, y_hbm, buf, sem, slot):
    def issue(r, carry):
        for k in range(TOP_K):
            row0 = pl.multiple_of(dest_ref[k, r] * ROW_TILE, ROW_TILE)
            dst0 = pl.multiple_of(((slot * TOP_K + k) * TD + r) * ROW_TILE, ROW_TILE)
            pltpu.make_async_copy(
                y_hbm.at[pl.ds(row0, ROW_TILE)], buf.at[pl.ds(dst0, ROW_TILE)],
                sem.at[slot]).start(priority=k % 2)
        return carry

    lax.fori_loop(0, TD, issue, 0, unroll=8)


def _combine_drain(y_hbm, buf, sem, slot):
    def drain(_, carry):
        for _ in range(DRAIN_UNROLL):
            pltpu.make_async_copy(
                y_hbm.at[pl.ds(0, ROW_TILE)], buf.at[pl.ds(0, ROW_TILE)], sem.at[slot]).wait()
        return carry

    lax.fori_loop(0, TD * TOP_K // DRAIN_UNROLL, drain, 0)


def _combine_kernel(dcur_ref, dnext_ref, y_hbm, gates_ref, h1_ref, g_ref, o_ref, buf, sem):
    i = pl.program_id(0)
    n = pl.num_programs(0)

    @pl.when(i == 0)
    def _():
        _combine_gather(dcur_ref, y_hbm, buf, sem, 0)

    for slot in range(2):
        @pl.when((i % 2 == slot) & (i + 1 < n))
        def _(slot=slot):
            _combine_gather(dnext_ref, y_hbm, buf, sem, 1 - slot)

    for slot in range(2):
        @pl.when(i % 2 == slot)
        def _(slot=slot):
            _combine_drain(y_hbm, buf, sem, slot)
            acc = h1_ref[...]
            gates = gates_ref[...]
            for k in range(TOP_K):
                base = (slot * TOP_K + k) * TD * ROW_TILE
                yk = jnp.concatenate(
                    [buf[pl.ds(base + c, TD, stride=ROW_TILE), :] for c in range(ROW_TILE)], axis=1)
                acc = acc + yk * gates[:, k:k + 1]
            ms = jnp.mean(acc * acc, axis=-1, keepdims=True)
            o_ref[...] = acc * lax.rsqrt(ms + NORM_EPS) * g_ref[...]


def _combine(dest, y2d, gates_t, h1, gain):
    t = h1.shape[0]
    n = t // TD
    return pl.pallas_call(
        _combine_kernel,
        grid=(n,),
        in_specs=[
            pl.BlockSpec((TOP_K, TD), lambda i: (0, i), memory_space=pltpu.SMEM),
            pl.BlockSpec((TOP_K, TD), lambda i: (0, jnp.minimum(i + 1, n - 1)), memory_space=pltpu.SMEM),
            pl.BlockSpec(memory_space=pl.ANY),
            pl.BlockSpec((TD, TOP_K), lambda i: (i, 0)),
            pl.BlockSpec((TD, D_MODEL), lambda i: (i, 0)),
            pl.BlockSpec((1, D_MODEL), lambda i: (0, 0)),
        ],
        out_specs=pl.BlockSpec((TD, D_MODEL), lambda i: (i, 0)),
        out_shape=jax.ShapeDtypeStruct((t, D_MODEL), F32),
        scratch_shapes=[
            pltpu.VMEM((2 * TOP_K * TD * ROW_TILE, LANES), F32),
            pltpu.SemaphoreType.DMA((2,)),
        ],
        compiler_params=_cparams(("arbitrary",)),
        name="combine",
    )(dest, dest, y2d, gates_t, h1, gain)


def _block_plan(counts, n_blocks):
    padded = (counts + TM_FFN - 1) // TM_FFN * TM_FFN
    pad_end = jnp.cumsum(padded)
    pad_start = pad_end - padded
    blk_start = jnp.arange(n_blocks, dtype=I32) * TM_FFN
    blk_e = jnp.sum((pad_end[None, :] <= blk_start[:, None]).astype(I32), axis=1)
    blk_e = jnp.minimum(blk_e, N_EXPERTS - 1)
    blk_new = jnp.concatenate([jnp.ones((1,), I32), (blk_e[1:] != blk_e[:-1]).astype(I32)])
    n_used = (pad_end[-1:] // TM_FFN).astype(I32)
    return pad_start.astype(I32), pad_end.astype(I32), blk_e, blk_new, n_used


@jax.jit
def _forward(x, meta_tokens, rel_bias, attn_norm, w_in, w_out, lambda_q1, lambda_k1, lambda_q2,
             lambda_k2, subln_gain, ffn_norm, w_router, b_router, w1, b1, w2, b2, final_norm):
    b, s, _ = x.shape
    t = b * s
    assert TQ == TK and TM_PROJ % TK == 0 and TM_PROJ % TBK == 0 and TB % TBK == 0
    assert s % (TQ * SB_QBLOCKS) == 0 and s % TB == 0 and t % TM_PROJ == 0 and t % TD == 0
    x2d = x.reshape(t, D_MODEL)

    scale = HEAD_DIM ** -0.5
    c_sbk, c_sbv, c_dfq, c_dfk, c_dfv = D_SB, 2 * D_SB, 3 * D_SB, 3 * D_SB + D_DIFF, 3 * D_SB + 2 * D_DIFF
    w_in_b = w_in[0].astype(BF16)
    g_attn = attn_norm[0][None, :]
    colscale = jnp.ones((D_IN,), F32).at[0:D_SB].set(scale).at[c_dfq:c_dfk].set(scale)[None, :]
    w_a = jnp.concatenate([w_in_b[:, :c_sbk], w_in_b[:, c_dfq:c_dfv]], axis=1)
    cs_a = jnp.concatenate([colscale[:, :c_sbk], colscale[:, c_dfq:c_dfv]], axis=1)
    a_proj, k_perm, vt_sb, vt_df = _in_proj_tokens(
        x2d, g_attn, cs_a, w_a, w_in_b[:, c_sbk:c_sbv], w_in_b[:, c_sbv:c_dfq].T, w_in_b[:, c_dfv:].T)
    mproj = _in_proj_meta(meta_tokens, g_attn, colscale, w_in_b)
    mvt_sb = mproj[:, c_sbv:c_dfq].reshape(N_META, H_SB, HEAD_DIM).transpose(1, 2, 0)
    mvt_df = mproj[:, c_dfv:].reshape(N_META, H_DIFF, 2 * HEAD_DIM).transpose(1, 2, 0)

    o_sb = _sb_attention(a_proj, k_perm, vt_sb, mproj, mvt_sb, b, s)

    bias_tiles, meta_bias = _rel_bias_tiles(rel_bias, s)
    o_df = _diff_attention(a_proj, vt_df, mproj, mvt_df, bias_tiles, meta_bias,
                           lambda_q1, lambda_k1, lambda_q2, lambda_k2,
                           subln_gain[0][:, None], b, s)

    tri = jnp.triu(jnp.ones((TM_PROJ, TM_PROJ), BF16), k=1)
    h1, xn2d, idx, gates, rank, cnt = _out_router(
        x2d, o_sb, o_df, w_out[0].astype(BF16), ffn_norm[0][None, :],
        w_router[0].T.astype(BF16), b_router[0][:, None], tri, TM_PROJ)

    a = t * TOP_K
    n_blocks = a // TM_FFN + N_EXPERTS
    a_pad = n_blocks * TM_FFN
    counts = cnt[:, 0].astype(I32)
    pad_start, pad_end, blk_e, blk_new, n_used = _block_plan(counts, n_blocks)
    dest = _route_dest(pad_start, idx, rank)

    xs = _dispatch(pad_end, n_used, dest, xn2d.reshape(t, ROW_TILE, LANES), a_pad)
    y2d = _expert_ffn(blk_e, blk_new, n_used, xs.reshape(a_pad * ROW_TILE, LANES),
                      w1[0], b1[0][:, None, :], w2[0], b2[0][:, None, :], n_blocks)
    out = _combine(dest, y2d, gates.T, h1, final_norm[None, :])
    return out.reshape(b, s, D_MODEL)


def kernel(x, meta_tokens, rel_bias, attn_norm, w_in, w_out, lambda_q1, lambda_k1, lambda_q2,
           lambda_k2, subln_gain, ffn_norm, w_router, b_router, w1, b1, w2, b2, final_norm):
    return _forward(x, meta_tokens, rel_bias, attn_norm, w_in, w_out, lambda_q1, lambda_k1,
                    lambda_q2, lambda_k2, subln_gain, ffn_norm, w_router, b_router, w1, b1, w2, b2,
                    final_norm)
```

```python
import functools
import math

import jax
import jax.numpy as jnp
from jax import lax
from jax.experimental import pallas as pl
from jax.experimental.pallas import tpu as pltpu

D_MODEL = 1024
N_META = 16
CHUNK = 64
HEAD_DIM = 64
H_SB = 8
H_DIFF = 4
D_SB = H_SB * HEAD_DIM
D_DIFF = H_DIFF * 2 * HEAD_DIM
D_IN = 3 * D_SB + 3 * D_DIFF
N_BUCKETS = 32
N_EXPERTS = 32
TOP_K = 4
D_FF = D_MODEL
SWIGLU_ALPHA = 1.702
SWIGLU_LIMIT = 7.0
NORM_EPS = 1e-6
SUBLN_EPS = 1e-5
NEG_BIG = -1e30
LAMBDA_INIT = 0.8 - 0.6 * math.exp(-0.3 * 0)

LANES = 128
SUBLANES = 8
ROW_TILE = D_MODEL // LANES
VMEM_LIMIT = 56 * 1024 * 1024

TM_PROJ = 512
TQ = 256
TK = 256
KCH = TK // SUBLANES
SB_PAIRS = 4
SB_QBLOCKS = 2
TB = 512
TBK = 512
KB_PER_Q = TB // TBK
DF_HEADS = 4
L_ROWS = 16
TM_FFN = 512
TD = 256
DRAIN_UNROLL = 128
RING = 3
N_BIAS_TILES = KB_PER_Q + 2
SB_EXIT = 104.0

F32 = jnp.float32
BF16 = jnp.bfloat16
I32 = jnp.int32

_NT = (((1,), (1,)), ((), ()))


def _cparams(sem, vmem=VMEM_LIMIT):
    return pltpu.CompilerParams(dimension_semantics=sem, vmem_limit_bytes=vmem)


def _inproj_kernel(x_ref, g_ref, cs_ref, w_ref, o_ref):
    x = x_ref[...]
    ms = jnp.mean(x * x, axis=-1, keepdims=True)
    xn = (x * lax.rsqrt(ms + NORM_EPS) * g_ref[...]).astype(BF16)
    y = jnp.dot(xn, w_ref[...], preferred_element_type=F32)
    o_ref[...] = (y * cs_ref[...]).astype(BF16)


def _in_proj_meta(x2d, gain, colscale, w_bf16):
    t = x2d.shape[0]
    return pl.pallas_call(
        _inproj_kernel,
        grid=(1,),
        in_specs=[
            pl.BlockSpec((t, D_MODEL), lambda i: (0, 0)),
            pl.BlockSpec((1, D_MODEL), lambda i: (0, 0)),
            pl.BlockSpec((1, D_IN), lambda i: (0, 0)),
            pl.BlockSpec((D_MODEL, D_IN), lambda i: (0, 0)),
        ],
        out_specs=pl.BlockSpec((t, D_IN), lambda i: (0, 0)),
        out_shape=jax.ShapeDtypeStruct((t, D_IN), BF16),
        compiler_params=_cparams(("arbitrary",)),
        name="in_proj_meta",
    )(x2d, gain, colscale, w_bf16)


def _rms_bf16(x, g):
    ms = jnp.mean(x * x, axis=-1, keepdims=True)
    return (x * lax.rsqrt(ms + NORM_EPS) * g).astype(BF16)


def _inproj_tokens_kernel(x_ref, g_ref, cs_ref, perm_ref, wa_ref, wk_ref, wvs_ref, wvd_ref,
                          a_ref, kp_ref, vts_ref, vtd_ref):
    xn = _rms_bf16(x_ref[...], g_ref[...])
    xnp = jnp.concatenate(
        [jnp.dot(perm_ref[...], xn[blk * TK:(blk + 1) * TK], preferred_element_type=F32)
         for blk in range(TM_PROJ // TK)], axis=0).astype(BF16)
    a_ref[...] = (jnp.dot(xn, wa_ref[...], preferred_element_type=F32) * cs_ref[...]).astype(BF16)
    kp_ref[...] = jnp.dot(xnp, wk_ref[...], preferred_element_type=F32).astype(BF16)
    vts = lax.dot_general(wvs_ref[...], xnp, _NT, preferred_element_type=F32).astype(BF16)
    for blk in range(TM_PROJ // TK):
        vts_ref[blk] = vts[:, blk * TK:(blk + 1) * TK]
    vtd = lax.dot_general(wvd_ref[...], xn, _NT, preferred_element_type=F32).astype(BF16)
    for blk in range(TM_PROJ // TBK):
        vtd_ref[blk] = vtd[:, blk * TBK:(blk + 1) * TBK]


def _chunk_order_matrix():
    dst = jnp.arange(TK, dtype=I32)
    src = (dst % SUBLANES) * KCH + dst // SUBLANES
    return (src[:, None] == jnp.arange(TK, dtype=I32)[None, :]).astype(BF16)


def _in_proj_tokens(x2d, gain, cs_a, w_a, w_k, w_vs_t, w_vd_t):
    t = x2d.shape[0]
    tm = TM_PROJ
    n_a = w_a.shape[1]
    const = lambda i: (0, 0)
    return pl.pallas_call(
        _inproj_tokens_kernel,
        grid=(t // tm,),
        in_specs=[
            pl.BlockSpec((tm, D_MODEL), lambda i: (i, 0)),
            pl.BlockSpec((1, D_MODEL), const),
            pl.BlockSpec((1, n_a), const),
            pl.BlockSpec((TK, TK), const),
            pl.BlockSpec((D_MODEL, n_a), const),
            pl.BlockSpec((D_MODEL, D_SB), const),
            pl.BlockSpec((D_SB, D_MODEL), const),
            pl.BlockSpec((D_DIFF, D_MODEL), const),
        ],
        out_specs=[
            pl.BlockSpec((tm, n_a), lambda i: (i, 0)),
            pl.BlockSpec((tm, D_SB), lambda i: (i, 0)),
            pl.BlockSpec((tm // TK, D_SB, TK), lambda i: (i, 0, 0)),
            pl.BlockSpec((tm // TBK, D_DIFF, TBK), lambda i: (i, 0, 0)),
        ],
        out_shape=[
            jax.ShapeDtypeStruct((t, n_a), BF16),
            jax.ShapeDtypeStruct((t, D_SB), BF16),
            jax.ShapeDtypeStruct((t // TK, D_SB, TK), BF16),
            jax.ShapeDtypeStruct((t // TBK, D_DIFF, TBK), BF16),
        ],
        compiler_params=_cparams(("parallel",)),
        name="in_proj",
    )(x2d, gain, cs_a, _chunk_order_matrix(), w_a, w_k, w_vs_t, w_vd_t)


def _bias_lookup(rel, rb_ref, h):
    n = jnp.abs(rel)
    n2 = n * n
    large = jnp.full(rel.shape, 8, I32)
    for k in range(1, 8):
        large = large + jnp.where(n2 >= (64 << k), 1, 0)
    bucket = jnp.where(rel > 0, N_BUCKETS // 2, 0) + jnp.where(n < 8, n, large)
    out = jnp.zeros(rel.shape, F32)
    for b in range(N_BUCKETS):
        out = jnp.where(bucket == b, rb_ref[b, h], out)
    return out


def _relbias_kernel(rb_ref, bt_ref, mb_ref):
    h = pl.program_id(0)
    krow = lax.broadcasted_iota(I32, (TBK, TB), 0)
    qcol = lax.broadcasted_iota(I32, (TBK, TB), 1)
    for d in range(N_BIAS_TILES):
        key_off = (KB_PER_Q - 1 - d) * TBK + krow
        visible = (key_off // CHUNK) <= (qcol // CHUNK)
        bt_ref[0, d] = jnp.where(visible, _bias_lookup(key_off - qcol, rb_ref, h), NEG_BIG)
    s = mb_ref.shape[2]
    mrow = lax.broadcasted_iota(I32, (N_META, s), 0)
    qpos = lax.broadcasted_iota(I32, (N_META, s), 1) + N_META
    mb_ref[0] = _bias_lookup(mrow - qpos, rb_ref, h)


def _rel_bias_tiles(rel_bias, s):
    return pl.pallas_call(
        _relbias_kernel,
        grid=(H_DIFF,),
        in_specs=[pl.BlockSpec(memory_space=pltpu.SMEM)],
        out_specs=[
            pl.BlockSpec((1, N_BIAS_TILES, TBK, TB), lambda h: (h, 0, 0, 0)),
            pl.BlockSpec((1, N_META, s), lambda h: (h, 0, 0)),
        ],
        out_shape=[
            jax.ShapeDtypeStruct((H_DIFF, N_BIAS_TILES, TBK, TB), F32),
            jax.ShapeDtypeStruct((H_DIFF, N_META, s), F32),
        ],
        compiler_params=_cparams(("arbitrary",)),
        name="rel_bias",
    )(rel_bias)


def _suffix_incl_sublanes(x):
    r = lax.broadcasted_iota(I32, x.shape, 0)
    for d in (1, 2, 4):
        shifted = pltpu.roll(x, SUBLANES - d, axis=0)
        x = x + jnp.where(r + d < SUBLANES, shifted, 0.0)
    return x


def _softplus(s):
    return jnp.maximum(s, 0.0) + jnp.log(1.0 + jnp.exp(-jnp.abs(s)))


def _head_half(qpair, half):
    lane = lax.broadcasted_iota(I32, qpair.shape, 1)
    keep = (lane >= HEAD_DIM * half) & (lane < HEAD_DIM * (half + 1))
    return jnp.where(keep, qpair, jnp.zeros_like(qpair))


def _sb_block(s, vt, carry, acc, valid):
    sp = _softplus(s)
    if valid is not None:
        sp = jnp.where(valid, sp, 0.0)
    run = jnp.zeros((SUBLANES, s.shape[1]), F32)
    parts = [None] * KCH
    for i in reversed(range(KCH)):
        run = run + sp[SUBLANES * i:SUBLANES * (i + 1), :]
        parts[i] = run
    incl = _suffix_incl_sublanes(run)
    base = (incl - run) + carry
    r_sum = jnp.concatenate([p + base for p in parts], axis=0)
    w = jnp.exp(s - r_sum)
    if valid is not None:
        w = jnp.where(valid, w, 0.0)
    acc = acc + jnp.dot(vt, w.astype(BF16), preferred_element_type=F32)
    return carry + incl[0:1, :], acc


def _sb_meta_block(s, vt, carry, acc):
    sp = _softplus(s)
    lo, hi = sp[0:SUBLANES, :], sp[SUBLANES:2 * SUBLANES, :]
    hi_incl = _suffix_incl_sublanes(hi)
    lo_incl = _suffix_incl_sublanes(lo) + hi_incl[0:1, :]
    r_sum = jnp.concatenate([lo_incl, hi_incl], axis=0) + carry
    w = jnp.exp(s - r_sum)
    return acc + jnp.dot(vt, w.astype(BF16), preferred_element_type=F32)


def _sb_kernel(q_ref, k_ref, vt_ref, mk_ref, mvt_ref, o_ref):
    step = pl.program_id(2)
    row = lax.broadcasted_iota(I32, (TK, TQ), 0)
    lane = lax.broadcasted_iota(I32, (TK, TQ), 1)
    key_off = (row % SUBLANES) * KCH + row // SUBLANES
    causal = key_off < lane

    units = []
    for pp in range(SB_PAIRS):
        for j in range(SB_QBLOCKS):
            qpair = q_ref[TQ * j:TQ * (j + 1), LANES * pp:LANES * (pp + 1)]
            for half in range(2):
                units.append((pp, j, half, step * SB_QBLOCKS + j, _head_half(qpair, half)))

    def scores(u, kb):
        pp, _, _, _, qz = u
        start = pl.multiple_of(kb * TK, TK)
        kblk = k_ref[pl.ds(start, TK), LANES * pp:LANES * (pp + 1)]
        return lax.dot_general(kblk, qz, _NT, preferred_element_type=F32)

    def values(u, kb):
        pp, _, half, _, _ = u
        r0 = 2 * HEAD_DIM * pp + HEAD_DIM * half
        return vt_ref[kb, r0:r0 + HEAD_DIM, :]

    def alive(carry):
        return (jnp.min(carry) < SB_EXIT).astype(I32)

    s_diag = [scores(u, u[3]) for u in units]
    s_prev = [scores(u, jnp.maximum(u[3] - 1, 0)) for u in units]
    carries, accs = [], []
    for u, sd, sp in zip(units, s_diag, s_prev):
        qi = u[3]
        carry = jnp.zeros((1, TQ), F32)
        acc = jnp.zeros((HEAD_DIM, TQ), F32)
        carry, acc = _sb_block(sd, values(u, qi), carry, acc, causal)
        carry2, acc2 = _sb_block(sp, values(u, jnp.maximum(qi - 1, 0)), carry, acc, None)
        carries.append(jnp.where(qi > 0, carry2, carry))
        accs.append(jnp.where(qi > 0, acc2, acc))

    cmin = carries[0]
    for c in carries[1:]:
        cmin = jnp.minimum(cmin, c)

    def slow(accs):
        out = []
        for u, carry, acc in zip(units, carries, accs):
            def cond(st):
                return (st[0] >= 0) & (st[1] > 0)

            def body(st, u=u):
                kb, _, carry, acc = st
                carry, acc = _sb_block(scores(u, kb), values(u, kb), carry, acc, None)
                return kb - 1, alive(carry), carry, acc

            _, live, carry, acc = lax.while_loop(cond, body, (u[3] - 2, alive(carry), carry, acc))

            def meta(acc, u=u, carry=carry):
                pp, _, half, _, qz = u
                sm = lax.dot_general(mk_ref[:, LANES * pp:LANES * (pp + 1)], qz, _NT,
                                     preferred_element_type=F32)
                return _sb_meta_block(sm, mvt_ref[2 * pp + half], carry, acc)

            out.append(lax.cond(live > 0, meta, lambda a: a, acc))
        return tuple(out)

    accs = lax.cond(alive(cmin) > 0, slow, lambda a: a, tuple(accs))
    for idx in range(0, len(units), 2):
        pp, j = units[idx][0], units[idx][1]
        pair_out = jnp.concatenate([accs[idx], accs[idx + 1]], axis=0).T.astype(BF16)
        o_ref[TQ * j:TQ * (j + 1), LANES * pp:LANES * (pp + 1)] = pair_out


def _sb_attention(a_proj, k_perm, vt_perm, mproj, mvt, b, s):
    nkb = s // TK
    nsteps = s // (TQ * SB_QBLOCKS)
    wl = LANES * SB_PAIRS
    mkcol0 = D_SB // wl
    return pl.pallas_call(
        _sb_kernel,
        grid=(b, H_SB // 2 // SB_PAIRS, nsteps),
        in_specs=[
            pl.BlockSpec((TQ * SB_QBLOCKS, wl), lambda bi, p, st: (bi * nsteps + st, p)),
            pl.BlockSpec((s, wl), lambda bi, p, st: (bi, p)),
            pl.BlockSpec((nkb, 2 * HEAD_DIM * SB_PAIRS, TK), lambda bi, p, st: (bi, p, 0)),
            pl.BlockSpec((N_META, wl), lambda bi, p, st: (0, mkcol0 + p)),
            pl.BlockSpec((2 * SB_PAIRS, HEAD_DIM, N_META), lambda bi, p, st: (p, 0, 0)),
        ],
        out_specs=pl.BlockSpec((TQ * SB_QBLOCKS, wl), lambda bi, p, st: (bi * nsteps + st, p)),
        out_shape=jax.ShapeDtypeStruct((b * s, D_SB), BF16),
        compiler_params=_cparams(("parallel", "parallel", "arbitrary")),
        name="sb_attn",
    )(a_proj, k_perm, vt_perm, mproj, mvt)


def _with_ones_rows(vt):
    r = lax.broadcasted_iota(I32, (L_ROWS, vt.shape[1]), 0)
    ones = jnp.where(r == 0, 1.0, 0.0).astype(vt.dtype)
    return jnp.concatenate([vt, ones], axis=0)


def _df_update(s, vt_ext, st):
    m, acc = st
    m_new = jnp.maximum(m, jnp.max(s, axis=0, keepdims=True).astype(F32))
    alpha = jnp.exp(m - m_new)
    p = jnp.exp(s - m_new.astype(BF16))
    acc = alpha * acc + jnp.dot(vt_ext, p, preferred_element_type=F32)
    return m_new, acc


def _df_kernel(q_ref, k_ref, vt_ref, mk_ref, mvt_ref, bt_ref, mb_ref,
               lq1_ref, lk1_ref, lq2_ref, lk2_ref, gain_ref, o_ref):
    qi = pl.program_id(2)
    dv = 2 * HEAD_DIM
    chains = []
    for hh in range(DF_HEADS):
        qpair = q_ref[:, LANES * hh:LANES * (hh + 1)]
        chains += [(hh, _head_half(qpair, 0)), (hh, _head_half(qpair, 1))]

    kb_last = KB_PER_Q * qi + KB_PER_Q - 1

    def scores(d):
        kb = jnp.maximum(kb_last - d, 0)
        rows = pl.ds(pl.multiple_of(kb * TBK, TBK), TBK)
        tile = jnp.minimum(d, N_BIAS_TILES - 1)
        return tuple(
            (lax.dot_general(k_ref[rows, LANES * hh:LANES * (hh + 1)], qz, _NT,
                             preferred_element_type=F32) + bt_ref[hh, tile]).astype(BF16)
            for hh, qz in chains)

    def values(hh, kb):
        return _with_ones_rows(vt_ref[kb, dv * hh:dv * (hh + 1), :])

    def init():
        return (jnp.full((1, TB), -jnp.inf, F32), jnp.zeros((dv + L_ROWS, TB), F32))

    def body(d, carry):
        st, s_cur = carry
        s_nxt = scores(d + 1)
        vts = [values(hh, kb_last - d) for hh in range(DF_HEADS)]
        return tuple(_df_update(s_cur[c], vts[hh], st[c]) for c, (hh, _) in enumerate(chains)), s_nxt

    s_first = scores(0)
    st = tuple(
        _df_update((lax.dot_general(mk_ref[:, LANES * hh:LANES * (hh + 1)], qz, _NT,
                                    preferred_element_type=F32) + mb_ref[hh]).astype(BF16),
                   _with_ones_rows(mvt_ref[hh]), init())
        for hh, qz in chains)
    st, _ = lax.fori_loop(0, kb_last + 1, body, (st, s_first))

    lam = (jnp.exp(jnp.sum(lq1_ref[...] * lk1_ref[...], axis=-1, keepdims=True))
           - jnp.exp(jnp.sum(lq2_ref[...] * lk2_ref[...], axis=-1, keepdims=True))
           + LAMBDA_INIT)
    for hh in range(DF_HEADS):
        acc1, acc2 = st[2 * hh][1], st[2 * hh + 1][1]
        o = acc1[:dv] / acc1[dv:dv + 1] - lam * (acc2[:dv] / acc2[dv:dv + 1])
        ms = jnp.mean(o * o, axis=0, keepdims=True)
        y = o * lax.rsqrt(ms + SUBLN_EPS) * gain_ref[...] * (1.0 - LAMBDA_INIT)
        o_ref[:, LANES * hh:LANES * (hh + 1)] = y.T.astype(BF16)


def _diff_attention(a_proj, vt, mproj, mvt, bias_tiles, meta_bias,
                    lq1, lk1, lq2, lk2, gain_col, b, s):
    nq = s // TB
    nkb = s // TBK
    qcol0 = D_SB // LANES
    kcol0 = (D_SB + D_DIFF) // LANES
    mkcol0 = (3 * D_SB + D_DIFF) // LANES
    lam_spec = pl.BlockSpec((1, HEAD_DIM), lambda bi, h, qi: (0, 0))
    wl = LANES * DF_HEADS
    qcol0, kcol0, mkcol0 = qcol0 // DF_HEADS, kcol0 // DF_HEADS, mkcol0 // DF_HEADS
    return pl.pallas_call(
        _df_kernel,
        grid=(b, H_DIFF // DF_HEADS, nq),
        in_specs=[
            pl.BlockSpec((TB, wl), lambda bi, h, qi: (bi * nq + qi, qcol0 + h)),
            pl.BlockSpec((s, wl), lambda bi, h, qi: (bi, kcol0 + h)),
            pl.BlockSpec((nkb, 2 * HEAD_DIM * DF_HEADS, TBK), lambda bi, h, qi: (bi, h, 0)),
            pl.BlockSpec((N_META, wl), lambda bi, h, qi: (0, mkcol0 + h)),
            pl.BlockSpec((DF_HEADS, 2 * HEAD_DIM, N_META), lambda bi, h, qi: (h, 0, 0)),
            pl.BlockSpec((DF_HEADS, N_BIAS_TILES, TBK, TB), lambda bi, h, qi: (h, 0, 0, 0)),
            pl.BlockSpec((DF_HEADS, N_META, TB), lambda bi, h, qi: (h, 0, qi)),
            lam_spec, lam_spec, lam_spec, lam_spec,
            pl.BlockSpec((2 * HEAD_DIM, 1), lambda bi, h, qi: (0, 0)),
        ],
        out_specs=pl.BlockSpec((TB, wl), lambda bi, h, qi: (bi * nq + qi, h)),
        out_shape=jax.ShapeDtypeStruct((b * s, D_DIFF), BF16),
        compiler_params=_cparams(("parallel", "parallel", "arbitrary")),
        name="diff_attn",
    )(a_proj, a_proj, vt, mproj, mvt, bias_tiles, meta_bias, lq1, lk1, lq2, lk2, gain_col)


def _outrouter_kernel(x_ref, osb_ref, odf_ref, wo_ref, g_ref, wrt_ref, br_ref, tri_ref,
                      h1_ref, xn_ref, idx_ref, gate_ref, rank_ref, cnt_ref, carry_ref):
    @pl.when(pl.program_id(0) == 0)
    def _():
        carry_ref[...] = jnp.zeros_like(carry_ref)

    tm = x_ref.shape[0]
    mix = jnp.concatenate([osb_ref[...], odf_ref[...]], axis=1)
    h1 = x_ref[...] + jnp.dot(mix, wo_ref[...], preferred_element_type=F32)
    h1_ref[...] = h1
    ms = jnp.mean(h1 * h1, axis=-1, keepdims=True)
    xn = h1 * lax.rsqrt(ms + NORM_EPS) * g_ref[...]
    for c in range(ROW_TILE):
        xn_ref[pl.ds(c, tm, stride=ROW_TILE), :] = xn[:, LANES * c:LANES * (c + 1)]

    logits = lax.dot_general(wrt_ref[...], xn.astype(BF16), _NT, preferred_element_type=F32) + br_ref[...]
    e_iota = lax.broadcasted_iota(I32, logits.shape, 0)
    work = logits
    vals, idxs = [], []
    for _ in range(TOP_K):
        m = jnp.max(work, axis=0, keepdims=True)
        ik = jnp.min(jnp.where(work == m, e_iota, N_EXPERTS), axis=0, keepdims=True)
        vals.append(m)
        idxs.append(ik)
        work = jnp.where(e_iota == ik, -jnp.inf, work)
    exps = [jnp.exp(v - vals[0]) for v in vals]
    den = exps[0] + exps[1] + exps[2] + exps[3]
    onehot = jnp.zeros(logits.shape, F32)
    for ik in idxs:
        onehot = onehot + jnp.where(e_iota == ik, 1.0, 0.0)
    prefix = jnp.dot(onehot.astype(BF16), tri_ref[...], preferred_element_type=F32)
    pos = prefix + carry_ref[:, 0:1]
    for k in range(TOP_K):
        idx_ref[k:k + 1, :] = idxs[k]
        gate_ref[k:k + 1, :] = exps[k] / den
        rank_ref[k:k + 1, :] = jnp.sum(jnp.where(e_iota == idxs[k], pos, 0.0), axis=0,
                                       keepdims=True).astype(I32)
    carry_ref[...] = carry_ref[...] + jnp.sum(onehot, axis=1, keepdims=True)
    cnt_ref[...] = carry_ref[...]


def _out_router(x2d, o_sb, o_df, wo_bf16, gain, wr_t, br_col, tri, tm):
    t = x2d.shape[0]
    const = lambda i: (0, 0)
    return pl.pallas_call(
        _outrouter_kernel,
        grid=(t // tm,),
        in_specs=[
            pl.BlockSpec((tm, D_MODEL), lambda i: (i, 0)),
            pl.BlockSpec((tm, D_SB), lambda i: (i, 0)),
            pl.BlockSpec((tm, D_DIFF), lambda i: (i, 0)),
            pl.BlockSpec((D_SB + D_DIFF, D_MODEL), const),
            pl.BlockSpec((1, D_MODEL), const),
            pl.BlockSpec((N_EXPERTS, D_MODEL), const),
            pl.BlockSpec((N_EXPERTS, 1), const),
            pl.BlockSpec((tm, tm), const),
        ],
        out_specs=[
            pl.BlockSpec((tm, D_MODEL), lambda i: (i, 0)),
            pl.BlockSpec((tm * ROW_TILE, LANES), lambda i: (i, 0)),
            pl.BlockSpec((TOP_K, tm), lambda i: (0, i)),
            pl.BlockSpec((TOP_K, tm), lambda i: (0, i)),
            pl.BlockSpec((TOP_K, tm), lambda i: (0, i)),
            pl.BlockSpec((N_EXPERTS, LANES), const),
        ],
        out_shape=[
            jax.ShapeDtypeStruct((t, D_MODEL), F32),
            jax.ShapeDtypeStruct((t * ROW_TILE, LANES), F32),
            jax.ShapeDtypeStruct((TOP_K, t), I32),
            jax.ShapeDtypeStruct((TOP_K, t), F32),
            jax.ShapeDtypeStruct((TOP_K, t), I32),
            jax.ShapeDtypeStruct((N_EXPERTS, LANES), F32),
        ],
        scratch_shapes=[pltpu.VMEM((N_EXPERTS, LANES), F32)],
        compiler_params=_cparams(("arbitrary",)),
        name="out_router",
    )(x2d, o_sb, o_df, wo_bf16, gain, wr_t, br_col, tri)


def _dest_kernel(ps_ref, idx_ref, rank_ref, dest_ref):
    idx = idx_ref[...]
    off = jnp.zeros(idx.shape, I32)
    for e in range(N_EXPERTS):
        off = jnp.where(idx == e, ps_ref[e], off)
    dest_ref[...] = rank_ref[...] + off


def _route_dest(pad_start, idx, rank):
    t = idx.shape[1]
    tt = min(t, 8192)
    grid_spec = pltpu.PrefetchScalarGridSpec(
        num_scalar_prefetch=1,
        grid=(t // tt,),
        in_specs=[pl.BlockSpec((TOP_K, tt), lambda i, ps: (0, i)),
                  pl.BlockSpec((TOP_K, tt), lambda i, ps: (0, i))],
        out_specs=pl.BlockSpec((TOP_K, tt), lambda i, ps: (0, i)),
    )
    return pl.pallas_call(
        _dest_kernel,
        grid_spec=grid_spec,
        out_shape=jax.ShapeDtypeStruct((TOP_K, t), I32),
        compiler_params=_cparams(("parallel",)),
        name="route_dest",
    )(pad_start, idx, rank)


def _zero_fill_padding(pe_ref, nu_ref, xs_hbm, zbuf, zsem, first_tail_block):
    zbuf[...] = jnp.zeros_like(zbuf)
    conds, copies = [], []
    for e in range(N_EXPERTS):
        prev_end = pe_ref[e - 1] if e > 0 else 0
        conds.append(pe_ref[e] > prev_end)
        start = jnp.maximum(pe_ref[e] - TM_FFN, 0)
        copies.append(pltpu.make_async_copy(zbuf, xs_hbm.at[pl.ds(start, TM_FFN)], zsem))
    for j in range(N_EXPERTS):
        blk = first_tail_block + j
        conds.append(blk >= nu_ref[0])
        copies.append(pltpu.make_async_copy(zbuf, xs_hbm.at[pl.ds(blk * TM_FFN, TM_FFN)], zsem))
    for cond, c in zip(conds, copies):
        pl.when(cond)(c.start)
    for cond, c in zip(conds, copies):
        pl.when(cond)(c.wait)


def _dispatch_kernel(pe_ref, nu_ref, dest_ref, xn_hbm, xs_hbm, zbuf, ring, lsem, sem, zsem):
    step = pl.program_id(0)
    n = pl.num_programs(0)
    first_tail_block = xs_hbm.shape[0] // TM_FFN - N_EXPERTS

    def load(tile, slot):
        return pltpu.make_async_copy(xn_hbm.at[pl.ds(tile * TD, TD)], ring.at[slot], lsem.at[slot])

    @pl.when(step == 0)
    def _():
        load(0, 0).start()

        @pl.when(n > 1)
        def _():
            load(1, 1).start()

        _zero_fill_padding(pe_ref, nu_ref, xs_hbm, zbuf, zsem, first_tail_block)

    def drain(slot):
        def body(_, carry):
            for _ in range(DRAIN_UNROLL):
                pltpu.make_async_copy(ring.at[0, 0], xs_hbm.at[0], sem.at[slot]).wait()
            return carry

        lax.fori_loop(0, TD * TOP_K // DRAIN_UNROLL, body, 0)

    for slot in range(RING):
        @pl.when(step % RING == slot)
        def _(slot=slot):
            prev = (slot + RING - 1) % RING
            load(step, slot).wait()

            def issue(r, carry):
                for k in range(TOP_K):
                    pltpu.make_async_copy(
                        ring.at[slot, r], xs_hbm.at[dest_ref[k, r]], sem.at[slot]).start(priority=k % 2)
                return carry

            lax.fori_loop(0, TD, issue, 0, unroll=8)

            @pl.when(step > 0)
            def _():
                drain(prev)

            @pl.when(step + 2 < n)
            def _():
                load(step + 2, prev).start()

            @pl.when(step == n - 1)
            def _():
                drain(slot)


def _dispatch(pad_end, n_used, dest, xn3, a_pad):
    t = dest.shape[1]
    grid_spec = pltpu.PrefetchScalarGridSpec(
        num_scalar_prefetch=2,
        grid=(t // TD,),
        in_specs=[
            pl.BlockSpec((TOP_K, TD), lambda i, pe, nu: (0, i), memory_space=pltpu.SMEM),
            pl.BlockSpec(memory_space=pl.ANY),
        ],
        out_specs=pl.BlockSpec(memory_space=pl.ANY),
        scratch_shapes=[
            pltpu.VMEM((TM_FFN, ROW_TILE, LANES), F32),
            pltpu.VMEM((RING, TD, ROW_TILE, LANES), F32),
            pltpu.SemaphoreType.DMA((RING,)),
            pltpu.SemaphoreType.DMA((RING,)),
            pltpu.SemaphoreType.DMA(()),
        ],
    )
    return pl.pallas_call(
        _dispatch_kernel,
        grid_spec=grid_spec,
        out_shape=jax.ShapeDtypeStruct((a_pad, ROW_TILE, LANES), F32),
        compiler_params=_cparams(("arbitrary",)),
        name="dispatch",
    )(pad_end, n_used, dest, xn3)


def _ffn_kernel(be_ref, new_ref, nu_ref, xs_ref, w1_ref, b1_ref, w2_ref, b2_ref, y_ref, w1b, w2b):
    i = pl.program_id(0)

    @pl.when(i >= nu_ref[0])
    def _():
        y_ref[...] = jnp.zeros_like(y_ref)

    @pl.when((i < nu_ref[0]) & (new_ref[i] > 0))
    def _():
        w1b[...] = w1_ref[...].astype(BF16)
        w2b[...] = w2_ref[...].astype(BF16)

    @pl.when(i < nu_ref[0])
    def _():
        x = jnp.concatenate(
            [xs_ref[pl.ds(c, TM_FFN, stride=ROW_TILE), :] for c in range(ROW_TILE)], axis=1).astype(BF16)
        hu = jnp.dot(x, w1b[...], preferred_element_type=F32) + b1_ref[...]
        gate = jnp.minimum(hu[:, :D_FF], SWIGLU_LIMIT)
        lin = jnp.clip(hu[:, D_FF:], -SWIGLU_LIMIT, SWIGLU_LIMIT)
        act = gate * jax.nn.sigmoid(SWIGLU_ALPHA * gate) * (lin + 1.0)
        y = jnp.dot(act.astype(BF16), w2b[...], preferred_element_type=F32) + b2_ref[...]
        for c in range(ROW_TILE):
            y_ref[pl.ds(c, TM_FFN, stride=ROW_TILE), :] = y[:, LANES * c:LANES * (c + 1)]


def _expert_ffn(blk_e, blk_new, n_used, xs2d, w1, b1, w2, b2, n_blocks):
    rows = TM_FFN * ROW_TILE

    def xmap(i, be, new, nu):
        return (jnp.minimum(i, nu[0] - 1), 0)

    def wmap(i, be, new, nu):
        return (be[i], 0, 0)

    grid_spec = pltpu.PrefetchScalarGridSpec(
        num_scalar_prefetch=3,
        grid=(n_blocks,),
        in_specs=[
            pl.BlockSpec((rows, LANES), xmap),
            pl.BlockSpec((None, D_MODEL, 2 * D_FF), wmap),
            pl.BlockSpec((None, 1, 2 * D_FF), wmap),
            pl.BlockSpec((None, D_FF, D_MODEL), wmap),
            pl.BlockSpec((None, 1, D_MODEL), wmap),
        ],
        out_specs=pl.BlockSpec((rows, LANES), lambda i, be, new, nu: (i, 0)),
        scratch_shapes=[pltpu.VMEM((D_MODEL, 2 * D_FF), BF16), pltpu.VMEM((D_FF, D_MODEL), BF16)],
    )
    return pl.pallas_call(
        _ffn_kernel,
        grid_spec=grid_spec,
        out_shape=jax.ShapeDtypeStruct(xs2d.shape, F32),
        compiler_params=_cparams(("arbitrary",)),
        name="expert_ffn",
    )(blk_e, blk_new, n_used, xs2d, w1, b1, w2, b2)


def _combine_gather(dest_ref, y_hbm, buf, sem, slot):
    def issue(r, carry):
        for k in range(TOP_K):
            row0 = pl.multiple_of(dest_ref[k, r] * ROW_TILE, ROW_TILE)
            dst0 = pl.multiple_of(((slot * TOP_K + k) * TD + r) * ROW_TILE, ROW_TILE)
            pltpu.make_async_copy(
                y_hbm.at[pl.ds(row0, ROW_TILE)], buf.at[pl.ds(dst0, ROW_TILE)],
                sem.at[slot]).start(priority=k % 2)
        return carry

    lax.fori_loop(0, TD, issue, 0, unroll=8)


def _combine_drain(y_hbm, buf, sem, slot):
    def drain(_, carry):
        for _ in range(DRAIN_UNROLL):
            pltpu.make_async_copy(
                y_hbm.at[pl.ds(0, ROW_TILE)], buf.at[pl.ds(0, ROW_TILE)], sem.at[slot]).wait()
        return carry

    lax.fori_loop(0, TD * TOP_K // DRAIN_UNROLL, drain, 0)


def _combine_kernel(dcur_ref, dnext_ref, y_hbm, gates_ref, h1_ref, g_ref, o_ref, buf, tacc, sem):
    i = pl.program_id(0)
    n = pl.num_programs(0)

    @pl.when(i == 0)
    def _():
        _combine_gather(dcur_ref, y_hbm, buf, sem, 0)

    for slot in range(2):
        @pl.when((i % 2 == slot) & (i + 1 < n))
        def _(slot=slot):
            _combine_gather(dnext_ref, y_hbm, buf, sem, 1 - slot)

    for slot in range(2):
        @pl.when(i % 2 == slot)
        def _(slot=slot):
            _combine_drain(y_hbm, buf, sem, slot)

            def token(tok, carry):
                r0 = pl.multiple_of(tok * ROW_TILE, ROW_TILE)
                acc = jnp.zeros((ROW_TILE, LANES), F32)
                for k in range(TOP_K):
                    base = (slot * TOP_K + k) * TD * ROW_TILE
                    acc = acc + buf[pl.ds(base + r0, ROW_TILE), :] * gates_ref[k, pl.ds(tok, 1), :]
                tacc[pl.ds(r0, ROW_TILE), :] = acc
                return carry

            lax.fori_loop(0, TD, token, 0, unroll=8)
            moe = jnp.concatenate(
                [tacc[pl.ds(c, TD, stride=ROW_TILE), :] for c in range(ROW_TILE)], axis=1)
            acc = h1_ref[...] + moe
            ms = jnp.mean(acc * acc, axis=-1, keepdims=True)
            o_ref[...] = acc * lax.rsqrt(ms + NORM_EPS) * g_ref[...]


def _combine(dest, y2d, gates_t, h1, gain):
    t = h1.shape[0]
    n = t // TD
    return pl.pallas_call(
        _combine_kernel,
        grid=(n,),
        in_specs=[
            pl.BlockSpec((TOP_K, TD), lambda i: (0, i), memory_space=pltpu.SMEM),
            pl.BlockSpec((TOP_K, TD), lambda i: (0, jnp.minimum(i + 1, n - 1)), memory_space=pltpu.SMEM),
            pl.BlockSpec(memory_space=pl.ANY),
            pl.BlockSpec((TOP_K, TD, LANES), lambda i: (0, i, 0)),
            pl.BlockSpec((TD, D_MODEL), lambda i: (i, 0)),
            pl.BlockSpec((1, D_MODEL), lambda i: (0, 0)),
        ],
        out_specs=pl.BlockSpec((TD, D_MODEL), lambda i: (i, 0)),
        out_shape=jax.ShapeDtypeStruct((t, D_MODEL), F32),
        scratch_shapes=[
            pltpu.VMEM((2 * TOP_K * TD * ROW_TILE, LANES), F32),
            pltpu.VMEM((TD * ROW_TILE, LANES), F32),
            pltpu.SemaphoreType.DMA((2,)),
        ],
        compiler_params=_cparams(("arbitrary",)),
        name="combine",
    )(dest, dest, y2d, gates_t, h1, gain)


def _block_plan(counts, n_blocks):
    padded = (counts + TM_FFN - 1) // TM_FFN * TM_FFN
    pad_end = jnp.cumsum(padded)
    pad_start = pad_end - padded
    blk_start = jnp.arange(n_blocks, dtype=I32) * TM_FFN
    blk_e = jnp.sum((pad_end[None, :] <= blk_start[:, None]).astype(I32), axis=1)
    blk_e = jnp.minimum(blk_e, N_EXPERTS - 1)
    blk_new = jnp.concatenate([jnp.ones((1,), I32), (blk_e[1:] != blk_e[:-1]).astype(I32)])
    n_used = (pad_end[-1:] // TM_FFN).astype(I32)
    return pad_start.astype(I32), pad_end.astype(I32), blk_e, blk_new, n_used


@jax.jit
def _forward(x, meta_tokens, rel_bias, attn_norm, w_in, w_out, lambda_q1, lambda_k1, lambda_q2,
             lambda_k2, subln_gain, ffn_norm, w_router, b_router, w1, b1, w2, b2, final_norm):
    b, s, _ = x.shape
    t = b * s
    assert TQ == TK and TM_PROJ % TK == 0 and TM_PROJ % TBK == 0 and TB % TBK == 0
    assert s % (TQ * SB_QBLOCKS) == 0 and s % TB == 0 and t % TM_PROJ == 0 and t % TD == 0
    x2d = x.reshape(t, D_MODEL)

    scale = HEAD_DIM ** -0.5
    c_sbk, c_sbv, c_dfq, c_dfk, c_dfv = D_SB, 2 * D_SB, 3 * D_SB, 3 * D_SB + D_DIFF, 3 * D_SB + 2 * D_DIFF
    w_in_b = w_in[0].astype(BF16)
    g_attn = attn_norm[0][None, :]
    colscale = jnp.ones((D_IN,), F32).at[0:D_SB].set(scale).at[c_dfq:c_dfk].set(scale)[None, :]
    w_a = jnp.concatenate([w_in_b[:, :c_sbk], w_in_b[:, c_dfq:c_dfv]], axis=1)
    cs_a = jnp.concatenate([colscale[:, :c_sbk], colscale[:, c_dfq:c_dfv]], axis=1)
    a_proj, k_perm, vt_sb, vt_df = _in_proj_tokens(
        x2d, g_attn, cs_a, w_a, w_in_b[:, c_sbk:c_sbv], w_in_b[:, c_sbv:c_dfq].T, w_in_b[:, c_dfv:].T)
    mproj = _in_proj_meta(meta_tokens, g_attn, colscale, w_in_b)
    mvt_sb = mproj[:, c_sbv:c_dfq].reshape(N_META, H_SB, HEAD_DIM).transpose(1, 2, 0)
    mvt_df = mproj[:, c_dfv:].reshape(N_META, H_DIFF, 2 * HEAD_DIM).transpose(1, 2, 0)

    o_sb = _sb_attention(a_proj, k_perm, vt_sb, mproj, mvt_sb, b, s)

    bias_tiles, meta_bias = _rel_bias_tiles(rel_bias, s)
    o_df = _diff_attention(a_proj, vt_df, mproj, mvt_df, bias_tiles, meta_bias,
                           lambda_q1, lambda_k1, lambda_q2, lambda_k2,
                           subln_gain[0][:, None], b, s)

    tri = jnp.triu(jnp.ones((TM_PROJ, TM_PROJ), BF16), k=1)
    h1, xn2d, idx, gates, rank, cnt = _out_router(
        x2d, o_sb, o_df, w_out[0].astype(BF16), ffn_norm[0][None, :],
        w_router[0].T.astype(BF16), b_router[0][:, None], tri, TM_PROJ)

    a = t * TOP_K
    n_blocks = a // TM_FFN + N_EXPERTS
    a_pad = n_blocks * TM_FFN
    counts = cnt[:, 0].astype(I32)
    pad_start, pad_end, blk_e, blk_new, n_used = _block_plan(counts, n_blocks)
    dest = _route_dest(pad_start, idx, rank)

    xs = _dispatch(pad_end, n_used, dest, xn2d.reshape(t, ROW_TILE, LANES), a_pad)
    y2d = _expert_ffn(blk_e, blk_new, n_used, xs.reshape(a_pad * ROW_TILE, LANES),
                      w1[0], b1[0][:, None, :], w2[0], b2[0][:, None, :], n_blocks)
    gates_rep = jnp.broadcast_to(gates[:, :, None], (TOP_K, t, LANES))
    out = _combine(dest, y2d, gates_rep, h1, final_norm[None, :])
    return out.reshape(b, s, D_MODEL)


def kernel(x, meta_tokens, rel_bias, attn_norm, w_in, w_out, lambda_q1, lambda_k1, lambda_q2,
           lambda_k2, subln_gain, ffn_norm, w_router, b_router, w1, b1, w2, b2, final_norm):
    return _forward(x, meta_tokens, rel_bias, attn_norm, w_in, w_out, lambda_q1, lambda_k1,
                    lambda_q2, lambda_k2, subln_gain, ffn_norm, w_router, b_router, w1, b1, w2, b2,
                    final_norm)
```

```python
import functools
import math

import jax
import jax.numpy as jnp
from jax import lax
from jax.experimental import pallas as pl
from jax.experimental.pallas import tpu as pltpu

D_MODEL = 1024
N_META = 16
CHUNK = 64
HEAD_DIM = 64
H_SB = 8
H_DIFF = 4
D_SB = H_SB * HEAD_DIM
D_DIFF = H_DIFF * 2 * HEAD_DIM
D_IN = 3 * D_SB + 3 * D_DIFF
N_BUCKETS = 32
N_EXPERTS = 32
TOP_K = 4
D_FF = D_MODEL
SWIGLU_ALPHA = 1.702
SWIGLU_LIMIT = 7.0
NORM_EPS = 1e-6
SUBLN_EPS = 1e-5
NEG_BIG = -1e30
LAMBDA_INIT = 0.8 - 0.6 * math.exp(-0.3 * 0)

LANES = 128
SUBLANES = 8
ROW_TILE = D_MODEL // LANES
VMEM_LIMIT = 56 * 1024 * 1024

TM_PROJ = 512
TQ = 256
TK = 256
KCH = TK // SUBLANES
SB_PAIRS = 4
SB_QBLOCKS = 2
TB = 512
TBK = 512
KB_PER_Q = TB // TBK
DF_HEADS = 4
L_ROWS = 16
TM_FFN = 512
TD = 256
DRAIN_UNROLL = 128
RING = 3
N_BIAS_TILES = KB_PER_Q + 2
SB_EXIT = 104.0

F32 = jnp.float32
BF16 = jnp.bfloat16
I32 = jnp.int32

_NT = (((1,), (1,)), ((), ()))


def _cparams(sem, vmem=VMEM_LIMIT):
    return pltpu.CompilerParams(dimension_semantics=sem, vmem_limit_bytes=vmem)


def _inproj_kernel(x_ref, g_ref, cs_ref, w_ref, o_ref):
    x = x_ref[...]
    ms = jnp.mean(x * x, axis=-1, keepdims=True)
    xn = (x * lax.rsqrt(ms + NORM_EPS) * g_ref[...]).astype(BF16)
    y = jnp.dot(xn, w_ref[...], preferred_element_type=F32)
    o_ref[...] = (y * cs_ref[...]).astype(BF16)


def _in_proj_meta(x2d, gain, colscale, w_bf16):
    t = x2d.shape[0]
    return pl.pallas_call(
        _inproj_kernel,
        grid=(1,),
        in_specs=[
            pl.BlockSpec((t, D_MODEL), lambda i: (0, 0)),
            pl.BlockSpec((1, D_MODEL), lambda i: (0, 0)),
            pl.BlockSpec((1, D_IN), lambda i: (0, 0)),
            pl.BlockSpec((D_MODEL, D_IN), lambda i: (0, 0)),
        ],
        out_specs=pl.BlockSpec((t, D_IN), lambda i: (0, 0)),
        out_shape=jax.ShapeDtypeStruct((t, D_IN), BF16),
        compiler_params=_cparams(("arbitrary",)),
        name="in_proj_meta",
    )(x2d, gain, colscale, w_bf16)


def _rms_bf16(x, g):
    ms = jnp.mean(x * x, axis=-1, keepdims=True)
    return (x * lax.rsqrt(ms + NORM_EPS) * g).astype(BF16)


def _inproj_tokens_kernel(x_ref, g_ref, cs_ref, perm_ref, wa_ref, wk_ref, wvs_ref, wvd_ref,
                          a_ref, kp_ref, vts_ref, vtd_ref):
    xn = _rms_bf16(x_ref[...], g_ref[...])
    xnp = jnp.concatenate(
        [jnp.dot(perm_ref[...], xn[blk * TK:(blk + 1) * TK], preferred_element_type=F32)
         for blk in range(TM_PROJ // TK)], axis=0).astype(BF16)
    a_ref[...] = (jnp.dot(xn, wa_ref[...], preferred_element_type=F32) * cs_ref[...]).astype(BF16)
    kp_ref[...] = jnp.dot(xnp, wk_ref[...], preferred_element_type=F32).astype(BF16)
    vts = lax.dot_general(wvs_ref[...], xnp, _NT, preferred_element_type=F32).astype(BF16)
    for blk in range(TM_PROJ // TK):
        vts_ref[blk] = vts[:, blk * TK:(blk + 1) * TK]
    vtd = lax.dot_general(wvd_ref[...], xn, _NT, preferred_element_type=F32).astype(BF16)
    for blk in range(TM_PROJ // TBK):
        vtd_ref[blk] = vtd[:, blk * TBK:(blk + 1) * TBK]


def _chunk_order_matrix():
    dst = jnp.arange(TK, dtype=I32)
    src = (dst % SUBLANES) * KCH + dst // SUBLANES
    return (src[:, None] == jnp.arange(TK, dtype=I32)[None, :]).astype(BF16)


def _in_proj_tokens(x2d, gain, cs_a, w_a, w_k, w_vs_t, w_vd_t):
    t = x2d.shape[0]
    tm = TM_PROJ
    n_a = w_a.shape[1]
    const = lambda i: (0, 0)
    return pl.pallas_call(
        _inproj_tokens_kernel,
        grid=(t // tm,),
        in_specs=[
            pl.BlockSpec((tm, D_MODEL), lambda i: (i, 0)),
            pl.BlockSpec((1, D_MODEL), const),
            pl.BlockSpec((1, n_a), const),
            pl.BlockSpec((TK, TK), const),
            pl.BlockSpec((D_MODEL, n_a), const),
            pl.BlockSpec((D_MODEL, D_SB), const),
            pl.BlockSpec((D_SB, D_MODEL), const),
            pl.BlockSpec((D_DIFF, D_MODEL), const),
        ],
        out_specs=[
            pl.BlockSpec((tm, n_a), lambda i: (i, 0)),
            pl.BlockSpec((tm, D_SB), lambda i: (i, 0)),
            pl.BlockSpec((tm // TK, D_SB, TK), lambda i: (i, 0, 0)),
            pl.BlockSpec((tm // TBK, D_DIFF, TBK), lambda i: (i, 0, 0)),
        ],
        out_shape=[
            jax.ShapeDtypeStruct((t, n_a), BF16),
            jax.ShapeDtypeStruct((t, D_SB), BF16),
            jax.ShapeDtypeStruct((t // TK, D_SB, TK), BF16),
            jax.ShapeDtypeStruct((t // TBK, D_DIFF, TBK), BF16),
        ],
        compiler_params=_cparams(("parallel",)),
        name="in_proj",
    )(x2d, gain, cs_a, _chunk_order_matrix(), w_a, w_k, w_vs_t, w_vd_t)


def _bias_lookup(rel, rb_ref, h):
    n = jnp.abs(rel)
    n2 = n * n
    large = jnp.full(rel.shape, 8, I32)
    for k in range(1, 8):
        large = large + jnp.where(n2 >= (64 << k), 1, 0)
    bucket = jnp.where(rel > 0, N_BUCKETS // 2, 0) + jnp.where(n < 8, n, large)
    out = jnp.zeros(rel.shape, F32)
    for b in range(N_BUCKETS):
        out = jnp.where(bucket == b, rb_ref[b, h], out)
    return out


def _relbias_kernel(rb_ref, bt_ref, mb_ref):
    h = pl.program_id(0)
    krow = lax.broadcasted_iota(I32, (TBK, TB), 0)
    qcol = lax.broadcasted_iota(I32, (TBK, TB), 1)
    for d in range(N_BIAS_TILES):
        key_off = (KB_PER_Q - 1 - d) * TBK + krow
        visible = (key_off // CHUNK) <= (qcol // CHUNK)
        bt_ref[0, d] = jnp.where(visible, _bias_lookup(key_off - qcol, rb_ref, h), NEG_BIG)
    s = mb_ref.shape[2]
    mrow = lax.broadcasted_iota(I32, (N_META, s), 0)
    qpos = lax.broadcasted_iota(I32, (N_META, s), 1) + N_META
    mb_ref[0] = _bias_lookup(mrow - qpos, rb_ref, h)


def _rel_bias_tiles(rel_bias, s):
    return pl.pallas_call(
        _relbias_kernel,
        grid=(H_DIFF,),
        in_specs=[pl.BlockSpec(memory_space=pltpu.SMEM)],
        out_specs=[
            pl.BlockSpec((1, N_BIAS_TILES, TBK, TB), lambda h: (h, 0, 0, 0)),
            pl.BlockSpec((1, N_META, s), lambda h: (h, 0, 0)),
        ],
        out_shape=[
            jax.ShapeDtypeStruct((H_DIFF, N_BIAS_TILES, TBK, TB), F32),
            jax.ShapeDtypeStruct((H_DIFF, N_META, s), F32),
        ],
        compiler_params=_cparams(("arbitrary",)),
        name="rel_bias",
    )(rel_bias)


def _suffix_incl_sublanes(x):
    r = lax.broadcasted_iota(I32, x.shape, 0)
    for d in (1, 2, 4):
        shifted = pltpu.roll(x, SUBLANES - d, axis=0)
        x = x + jnp.where(r + d < SUBLANES, shifted, 0.0)
    return x


def _softplus(s):
    return jnp.maximum(s, 0.0) + jnp.log(1.0 + jnp.exp(-jnp.abs(s)))


def _head_half(qpair, half):
    lane = lax.broadcasted_iota(I32, qpair.shape, 1)
    keep = (lane >= HEAD_DIM * half) & (lane < HEAD_DIM * (half + 1))
    return jnp.where(keep, qpair, jnp.zeros_like(qpair))


def _sb_block(s, vt, carry, acc, valid):
    sp = _softplus(s)
    if valid is not None:
        sp = jnp.where(valid, sp, 0.0)
    run = jnp.zeros((SUBLANES, s.shape[1]), F32)
    parts = [None] * KCH
    for i in reversed(range(KCH)):
        run = run + sp[SUBLANES * i:SUBLANES * (i + 1), :]
        parts[i] = run
    incl = _suffix_incl_sublanes(run)
    base = (incl - run) + carry
    r_sum = jnp.concatenate([p + base for p in parts], axis=0)
    w = jnp.exp(s - r_sum)
    if valid is not None:
        w = jnp.where(valid, w, 0.0)
    acc = acc + jnp.dot(vt, w.astype(BF16), preferred_element_type=F32)
    return carry + incl[0:1, :], acc


def _sb_meta_block(s, vt, carry, acc):
    sp = _softplus(s)
    lo, hi = sp[0:SUBLANES, :], sp[SUBLANES:2 * SUBLANES, :]
    hi_incl = _suffix_incl_sublanes(hi)
    lo_incl = _suffix_incl_sublanes(lo) + hi_incl[0:1, :]
    r_sum = jnp.concatenate([lo_incl, hi_incl], axis=0) + carry
    w = jnp.exp(s - r_sum)
    return acc + jnp.dot(vt, w.astype(BF16), preferred_element_type=F32)


def _sb_kernel(q_ref, k_ref, vt_ref, mk_ref, mvt_ref, o_ref):
    step = pl.program_id(2)
    row = lax.broadcasted_iota(I32, (TK, TQ), 0)
    lane = lax.broadcasted_iota(I32, (TK, TQ), 1)
    key_off = (row % SUBLANES) * KCH + row // SUBLANES
    causal = key_off < lane

    units = []
    for pp in range(SB_PAIRS):
        for j in range(SB_QBLOCKS):
            qpair = q_ref[TQ * j:TQ * (j + 1), LANES * pp:LANES * (pp + 1)]
            for half in range(2):
                units.append((pp, j, half, step * SB_QBLOCKS + j, _head_half(qpair, half)))

    def scores(u, kb):
        pp, _, _, _, qz = u
        start = pl.multiple_of(kb * TK, TK)
        kblk = k_ref[pl.ds(start, TK), LANES * pp:LANES * (pp + 1)]
        return lax.dot_general(kblk, qz, _NT, preferred_element_type=F32)

    def values(u, kb):
        pp, _, half, _, _ = u
        r0 = 2 * HEAD_DIM * pp + HEAD_DIM * half
        return vt_ref[kb, r0:r0 + HEAD_DIM, :]

    def alive(carry):
        return (jnp.min(carry) < SB_EXIT).astype(I32)

    s_diag = [scores(u, u[3]) for u in units]
    s_prev = [scores(u, jnp.maximum(u[3] - 1, 0)) for u in units]
    carries, accs = [], []
    for u, sd, sp in zip(units, s_diag, s_prev):
        qi = u[3]
        carry = jnp.zeros((1, TQ), F32)
        acc = jnp.zeros((HEAD_DIM, TQ), F32)
        carry, acc = _sb_block(sd, values(u, qi), carry, acc, causal)
        carry2, acc2 = _sb_block(sp, values(u, jnp.maximum(qi - 1, 0)), carry, acc, None)
        carries.append(jnp.where(qi > 0, carry2, carry))
        accs.append(jnp.where(qi > 0, acc2, acc))

    cmin = carries[0]
    for c in carries[1:]:
        cmin = jnp.minimum(cmin, c)

    def slow(accs):
        out = []
        for u, carry, acc in zip(units, carries, accs):
            def cond(st):
                return (st[0] >= 0) & (st[1] > 0)

            def body(st, u=u):
                kb, _, carry, acc = st
                carry, acc = _sb_block(scores(u, kb), values(u, kb), carry, acc, None)
                return kb - 1, alive(carry), carry, acc

            _, live, carry, acc = lax.while_loop(cond, body, (u[3] - 2, alive(carry), carry, acc))

            def meta(acc, u=u, carry=carry):
                pp, _, half, _, qz = u
                sm = lax.dot_general(mk_ref[:, LANES * pp:LANES * (pp + 1)], qz, _NT,
                                     preferred_element_type=F32)
                return _sb_meta_block(sm, mvt_ref[2 * pp + half], carry, acc)

            out.append(lax.cond(live > 0, meta, lambda a: a, acc))
        return tuple(out)

    accs = lax.cond(alive(cmin) > 0, slow, lambda a: a, tuple(accs))
    for idx in range(0, len(units), 2):
        pp, j = units[idx][0], units[idx][1]
        pair_out = jnp.concatenate([accs[idx], accs[idx + 1]], axis=0).T.astype(BF16)
        o_ref[TQ * j:TQ * (j + 1), LANES * pp:LANES * (pp + 1)] = pair_out


def _sb_attention(a_proj, k_perm, vt_perm, mproj, mvt, b, s):
    nkb = s // TK
    nsteps = s // (TQ * SB_QBLOCKS)
    wl = LANES * SB_PAIRS
    mkcol0 = D_SB // wl
    return pl.pallas_call(
        _sb_kernel,
        grid=(b, H_SB // 2 // SB_PAIRS, nsteps),
        in_specs=[
            pl.BlockSpec((TQ * SB_QBLOCKS, wl), lambda bi, p, st: (bi * nsteps + st, p)),
            pl.BlockSpec((s, wl), lambda bi, p, st: (bi, p)),
            pl.BlockSpec((nkb, 2 * HEAD_DIM * SB_PAIRS, TK), lambda bi, p, st: (bi, p, 0)),
            pl.BlockSpec((N_META, wl), lambda bi, p, st: (0, mkcol0 + p)),
            pl.BlockSpec((2 * SB_PAIRS, HEAD_DIM, N_META), lambda bi, p, st: (p, 0, 0)),
        ],
        out_specs=pl.BlockSpec((TQ * SB_QBLOCKS, wl), lambda bi, p, st: (bi * nsteps + st, p)),
        out_shape=jax.ShapeDtypeStruct((b * s, D_SB), BF16),
        compiler_params=_cparams(("parallel", "parallel", "arbitrary")),
        name="sb_attn",
    )(a_proj, k_perm, vt_perm, mproj, mvt)


def _with_ones_rows(vt):
    r = lax.broadcasted_iota(I32, (L_ROWS, vt.shape[1]), 0)
    ones = jnp.where(r == 0, 1.0, 0.0).astype(vt.dtype)
    return jnp.concatenate([vt, ones], axis=0)


def _df_update(s, vt_ext, st):
    m, acc = st
    m_new = jnp.maximum(m, jnp.max(s, axis=0, keepdims=True).astype(F32))
    alpha = jnp.exp(m - m_new)
    p = jnp.exp(s - m_new.astype(BF16))
    acc = alpha * acc + jnp.dot(vt_ext, p, preferred_element_type=F32)
    return m_new, acc


def _df_kernel(q_ref, k_ref, vt_ref, mk_ref, mvt_ref, bt_ref, mb_ref,
               lq1_ref, lk1_ref, lq2_ref, lk2_ref, gain_ref, o_ref):
    qi = pl.program_id(2)
    dv = 2 * HEAD_DIM
    chains = []
    for hh in range(DF_HEADS):
        qpair = q_ref[:, LANES * hh:LANES * (hh + 1)]
        chains += [(hh, _head_half(qpair, 0)), (hh, _head_half(qpair, 1))]

    kb_last = KB_PER_Q * qi + KB_PER_Q - 1

    def scores(d):
        kb = jnp.maximum(kb_last - d, 0)
        rows = pl.ds(pl.multiple_of(kb * TBK, TBK), TBK)
        tile = jnp.minimum(d, N_BIAS_TILES - 1)
        return tuple(
            (lax.dot_general(k_ref[rows, LANES * hh:LANES * (hh + 1)], qz, _NT,
                             preferred_element_type=F32) + bt_ref[hh, tile]).astype(BF16)
            for hh, qz in chains)

    def values(hh, kb):
        return _with_ones_rows(vt_ref[kb, dv * hh:dv * (hh + 1), :])

    def init():
        return (jnp.full((1, TB), -jnp.inf, F32), jnp.zeros((dv + L_ROWS, TB), F32))

    def reduce_block(d, s_cur, st):
        vts = [values(hh, kb_last - d) for hh in range(DF_HEADS)]
        return tuple(_df_update(s_cur[c], vts[hh], st[c]) for c, (hh, _) in enumerate(chains))

    def body(d, carry):
        st, s_cur = carry
        s_nxt = scores(d + 1)
        return reduce_block(d, s_cur, st), s_nxt

    s_first = scores(0)
    st = tuple(
        _df_update((lax.dot_general(mk_ref[:, LANES * hh:LANES * (hh + 1)], qz, _NT,
                                    preferred_element_type=F32) + mb_ref[hh]).astype(BF16),
                   _with_ones_rows(mvt_ref[hh]), init())
        for hh, qz in chains)
    st, s_last = lax.fori_loop(0, kb_last, body, (st, s_first))
    st = reduce_block(kb_last, s_last, st)

    lam = (jnp.exp(jnp.sum(lq1_ref[...] * lk1_ref[...], axis=-1, keepdims=True))
           - jnp.exp(jnp.sum(lq2_ref[...] * lk2_ref[...], axis=-1, keepdims=True))
           + LAMBDA_INIT)
    for hh in range(DF_HEADS):
        acc1, acc2 = st[2 * hh][1], st[2 * hh + 1][1]
        o = acc1[:dv] / acc1[dv:dv + 1] - lam * (acc2[:dv] / acc2[dv:dv + 1])
        ms = jnp.mean(o * o, axis=0, keepdims=True)
        y = o * lax.rsqrt(ms + SUBLN_EPS) * gain_ref[...] * (1.0 - LAMBDA_INIT)
        o_ref[:, LANES * hh:LANES * (hh + 1)] = y.T.astype(BF16)


def _diff_attention(a_proj, vt, mproj, mvt, bias_tiles, meta_bias,
                    lq1, lk1, lq2, lk2, gain_col, b, s):
    nq = s // TB
    nkb = s // TBK
    qcol0 = D_SB // LANES
    kcol0 = (D_SB + D_DIFF) // LANES
    mkcol0 = (3 * D_SB + D_DIFF) // LANES
    lam_spec = pl.BlockSpec((1, HEAD_DIM), lambda bi, h, qi: (0, 0))
    wl = LANES * DF_HEADS
    qcol0, kcol0, mkcol0 = qcol0 // DF_HEADS, kcol0 // DF_HEADS, mkcol0 // DF_HEADS
    return pl.pallas_call(
        _df_kernel,
        grid=(b, H_DIFF // DF_HEADS, nq),
        in_specs=[
            pl.BlockSpec((TB, wl), lambda bi, h, qi: (bi * nq + qi, qcol0 + h)),
            pl.BlockSpec((s, wl), lambda bi, h, qi: (bi, kcol0 + h)),
            pl.BlockSpec((nkb, 2 * HEAD_DIM * DF_HEADS, TBK), lambda bi, h, qi: (bi, h, 0)),
            pl.BlockSpec((N_META, wl), lambda bi, h, qi: (0, mkcol0 + h)),
            pl.BlockSpec((DF_HEADS, 2 * HEAD_DIM, N_META), lambda bi, h, qi: (h, 0, 0)),
            pl.BlockSpec((DF_HEADS, N_BIAS_TILES, TBK, TB), lambda bi, h, qi: (h, 0, 0, 0)),
            pl.BlockSpec((DF_HEADS, N_META, TB), lambda bi, h, qi: (h, 0, qi)),
            lam_spec, lam_spec, lam_spec, lam_spec,
            pl.BlockSpec((2 * HEAD_DIM, 1), lambda bi, h, qi: (0, 0)),
        ],
        out_specs=pl.BlockSpec((TB, wl), lambda bi, h, qi: (bi * nq + qi, h)),
        out_shape=jax.ShapeDtypeStruct((b * s, D_DIFF), BF16),
        compiler_params=_cparams(("parallel", "parallel", "arbitrary")),
        name="diff_attn",
    )(a_proj, a_proj, vt, mproj, mvt, bias_tiles, meta_bias, lq1, lk1, lq2, lk2, gain_col)


def _outrouter_kernel(x_ref, osb_ref, odf_ref, wo_ref, g_ref, wrt_ref, br_ref, tri_ref,
                      h1_ref, xn_ref, idx_ref, gate_ref, rank_ref, cnt_ref, carry_ref):
    @pl.when(pl.program_id(0) == 0)
    def _():
        carry_ref[...] = jnp.zeros_like(carry_ref)

    tm = x_ref.shape[0]
    mix = jnp.concatenate([osb_ref[...], odf_ref[...]], axis=1)
    h1 = x_ref[...] + jnp.dot(mix, wo_ref[...], preferred_element_type=F32)
    h1_ref[...] = h1
    ms = jnp.mean(h1 * h1, axis=-1, keepdims=True)
    xn = h1 * lax.rsqrt(ms + NORM_EPS) * g_ref[...]
    for c in range(ROW_TILE):
        xn_ref[pl.ds(c, tm, stride=ROW_TILE), :] = xn[:, LANES * c:LANES * (c + 1)]

    logits = lax.dot_general(wrt_ref[...], xn.astype(BF16), _NT, preferred_element_type=F32) + br_ref[...]
    e_iota = lax.broadcasted_iota(I32, logits.shape, 0)
    work = logits
    vals, idxs = [], []
    for _ in range(TOP_K):
        m = jnp.max(work, axis=0, keepdims=True)
        ik = jnp.min(jnp.where(work == m, e_iota, N_EXPERTS), axis=0, keepdims=True)
        vals.append(m)
        idxs.append(ik)
        work = jnp.where(e_iota == ik, -jnp.inf, work)
    exps = [jnp.exp(v - vals[0]) for v in vals]
    den = exps[0] + exps[1] + exps[2] + exps[3]
    onehot = jnp.zeros(logits.shape, F32)
    for ik in idxs:
        onehot = onehot + jnp.where(e_iota == ik, 1.0, 0.0)
    prefix = jnp.dot(onehot.astype(BF16), tri_ref[...], preferred_element_type=F32)
    pos = prefix + carry_ref[:, 0:1]
    for k in range(TOP_K):
        idx_ref[k:k + 1, :] = idxs[k]
        gate_ref[k:k + 1, :] = exps[k] / den
        rank_ref[k:k + 1, :] = jnp.sum(jnp.where(e_iota == idxs[k], pos, 0.0), axis=0,
                                       keepdims=True).astype(I32)
    carry_ref[...] = carry_ref[...] + jnp.sum(onehot, axis=1, keepdims=True)
    cnt_ref[...] = carry_ref[...]


def _out_router(x2d, o_sb, o_df, wo_bf16, gain, wr_t, br_col, tri, tm):
    t = x2d.shape[0]
    const = lambda i: (0, 0)
    return pl.pallas_call(
        _outrouter_kernel,
        grid=(t // tm,),
        in_specs=[
            pl.BlockSpec((tm, D_MODEL), lambda i: (i, 0)),
            pl.BlockSpec((tm, D_SB), lambda i: (i, 0)),
            pl.BlockSpec((tm, D_DIFF), lambda i: (i, 0)),
            pl.BlockSpec((D_SB + D_DIFF, D_MODEL), const),
            pl.BlockSpec((1, D_MODEL), const),
            pl.BlockSpec((N_EXPERTS, D_MODEL), const),
            pl.BlockSpec((N_EXPERTS, 1), const),
            pl.BlockSpec((tm, tm), const),
        ],
        out_specs=[
            pl.BlockSpec((tm, D_MODEL), lambda i: (i, 0)),
            pl.BlockSpec((tm * ROW_TILE, LANES), lambda i: (i, 0)),
            pl.BlockSpec((TOP_K, tm), lambda i: (0, i)),
            pl.BlockSpec((TOP_K, tm), lambda i: (0, i)),
            pl.BlockSpec((TOP_K, tm), lambda i: (0, i)),
            pl.BlockSpec((N_EXPERTS, LANES), const),
        ],
        out_shape=[
            jax.ShapeDtypeStruct((t, D_MODEL), F32),
            jax.ShapeDtypeStruct((t * ROW_TILE, LANES), F32),
            jax.ShapeDtypeStruct((TOP_K, t), I32),
            jax.ShapeDtypeStruct((TOP_K, t), F32),
            jax.ShapeDtypeStruct((TOP_K, t), I32),
            jax.ShapeDtypeStruct((N_EXPERTS, LANES), F32),
        ],
        scratch_shapes=[pltpu.VMEM((N_EXPERTS, LANES), F32)],
        compiler_params=_cparams(("arbitrary",)),
        name="out_router",
    )(x2d, o_sb, o_df, wo_bf16, gain, wr_t, br_col, tri)


def _dest_kernel(ps_ref, idx_ref, rank_ref, dest_ref):
    idx = idx_ref[...]
    off = jnp.zeros(idx.shape, I32)
    for e in range(N_EXPERTS):
        off = jnp.where(idx == e, ps_ref[e], off)
    dest_ref[...] = rank_ref[...] + off


def _route_dest(pad_start, idx, rank):
    t = idx.shape[1]
    tt = min(t, 8192)
    grid_spec = pltpu.PrefetchScalarGridSpec(
        num_scalar_prefetch=1,
        grid=(t // tt,),
        in_specs=[pl.BlockSpec((TOP_K, tt), lambda i, ps: (0, i)),
                  pl.BlockSpec((TOP_K, tt), lambda i, ps: (0, i))],
        out_specs=pl.BlockSpec((TOP_K, tt), lambda i, ps: (0, i)),
    )
    return pl.pallas_call(
        _dest_kernel,
        grid_spec=grid_spec,
        out_shape=jax.ShapeDtypeStruct((TOP_K, t), I32),
        compiler_params=_cparams(("parallel",)),
        name="route_dest",
    )(pad_start, idx, rank)


def _zero_fill_padding(pe_ref, nu_ref, xs_hbm, zbuf, zsem, first_tail_block):
    zbuf[...] = jnp.zeros_like(zbuf)
    conds, copies = [], []
    for e in range(N_EXPERTS):
        prev_end = pe_ref[e - 1] if e > 0 else 0
        conds.append(pe_ref[e] > prev_end)
        start = jnp.maximum(pe_ref[e] - TM_FFN, 0)
        copies.append(pltpu.make_async_copy(zbuf, xs_hbm.at[pl.ds(start, TM_FFN)], zsem))
    for j in range(N_EXPERTS):
        blk = first_tail_block + j
        conds.append(blk >= nu_ref[0])
        copies.append(pltpu.make_async_copy(zbuf, xs_hbm.at[pl.ds(blk * TM_FFN, TM_FFN)], zsem))
    for cond, c in zip(conds, copies):
        pl.when(cond)(c.start)
    for cond, c in zip(conds, copies):
        pl.when(cond)(c.wait)


def _dispatch_kernel(pe_ref, nu_ref, dest_ref, xn_hbm, xs_hbm, zbuf, ring, lsem, sem, zsem):
    step = pl.program_id(0)
    n = pl.num_programs(0)
    first_tail_block = xs_hbm.shape[0] // TM_FFN - N_EXPERTS

    def load(tile, slot):
        return pltpu.make_async_copy(xn_hbm.at[pl.ds(tile * TD, TD)], ring.at[slot], lsem.at[slot])

    @pl.when(step == 0)
    def _():
        load(0, 0).start()

        @pl.when(n > 1)
        def _():
            load(1, 1).start()

        _zero_fill_padding(pe_ref, nu_ref, xs_hbm, zbuf, zsem, first_tail_block)

    def drain(slot):
        def body(_, carry):
            for _ in range(DRAIN_UNROLL):
                pltpu.make_async_copy(ring.at[0, 0], xs_hbm.at[0], sem.at[slot]).wait()
            return carry

        lax.fori_loop(0, TD * TOP_K // DRAIN_UNROLL, body, 0)

    for slot in range(RING):
        @pl.when(step % RING == slot)
        def _(slot=slot):
            prev = (slot + RING - 1) % RING
            load(step, slot).wait()

            def issue(r, carry):
                for k in range(TOP_K):
                    pltpu.make_async_copy(
                        ring.at[slot, r], xs_hbm.at[dest_ref[k, r]], sem.at[slot]).start(priority=k % 2)
                return carry

            lax.fori_loop(0, TD, issue, 0, unroll=8)

            @pl.when(step > 0)
            def _():
                drain(prev)

            @pl.when(step + 2 < n)
            def _():
                load(step + 2, prev).start()

            @pl.when(step == n - 1)
            def _():
                drain(slot)


def _dispatch(pad_end, n_used, dest, xn3, a_pad):
    t = dest.shape[1]
    grid_spec = pltpu.PrefetchScalarGridSpec(
        num_scalar_prefetch=2,
        grid=(t // TD,),
        in_specs=[
            pl.BlockSpec((TOP_K, TD), lambda i, pe, nu: (0, i), memory_space=pltpu.SMEM),
            pl.BlockSpec(memory_space=pl.ANY),
        ],
        out_specs=pl.BlockSpec(memory_space=pl.ANY),
        scratch_shapes=[
            pltpu.VMEM((TM_FFN, ROW_TILE, LANES), F32),
            pltpu.VMEM((RING, TD, ROW_TILE, LANES), F32),
            pltpu.SemaphoreType.DMA((RING,)),
            pltpu.SemaphoreType.DMA((RING,)),
            pltpu.SemaphoreType.DMA(()),
        ],
    )
    return pl.pallas_call(
        _dispatch_kernel,
        grid_spec=grid_spec,
        out_shape=jax.ShapeDtypeStruct((a_pad, ROW_TILE, LANES), F32),
        compiler_params=_cparams(("arbitrary",)),
        name="dispatch",
    )(pad_end, n_used, dest, xn3)


def _ffn_kernel(be_ref, new_ref, nu_ref, xs_ref, w1_ref, b1_ref, w2_ref, b2_ref, y_ref, w1b, w2b):
    i = pl.program_id(0)

    @pl.when(i >= nu_ref[0])
    def _():
        y_ref[...] = jnp.zeros_like(y_ref)

    @pl.when((i < nu_ref[0]) & (new_ref[i] > 0))
    def _():
        w1b[...] = w1_ref[...].astype(BF16)
        w2b[...] = w2_ref[...].astype(BF16)

    @pl.when(i < nu_ref[0])
    def _():
        x = jnp.concatenate(
            [xs_ref[pl.ds(c, TM_FFN, stride=ROW_TILE), :] for c in range(ROW_TILE)], axis=1).astype(BF16)
        hu = jnp.dot(x, w1b[...], preferred_element_type=F32) + b1_ref[...]
        gate = jnp.minimum(hu[:, :D_FF], SWIGLU_LIMIT)
        lin = jnp.clip(hu[:, D_FF:], -SWIGLU_LIMIT, SWIGLU_LIMIT)
        act = gate * jax.nn.sigmoid(SWIGLU_ALPHA * gate) * (lin + 1.0)
        y = jnp.dot(act.astype(BF16), w2b[...], preferred_element_type=F32) + b2_ref[...]
        for c in range(ROW_TILE):
            y_ref[pl.ds(c, TM_FFN, stride=ROW_TILE), :] = y[:, LANES * c:LANES * (c + 1)]


def _expert_ffn(blk_e, blk_new, n_used, xs2d, w1, b1, w2, b2, n_blocks):
    rows = TM_FFN * ROW_TILE

    def xmap(i, be, new, nu):
        return (jnp.minimum(i, nu[0] - 1), 0)

    def wmap(i, be, new, nu):
        return (be[i], 0, 0)

    grid_spec = pltpu.PrefetchScalarGridSpec(
        num_scalar_prefetch=3,
        grid=(n_blocks,),
        in_specs=[
            pl.BlockSpec((rows, LANES), xmap),
            pl.BlockSpec((None, D_MODEL, 2 * D_FF), wmap),
            pl.BlockSpec((None, 1, 2 * D_FF), wmap),
            pl.BlockSpec((None, D_FF, D_MODEL), wmap),
            pl.BlockSpec((None, 1, D_MODEL), wmap),
        ],
        out_specs=pl.BlockSpec((rows, LANES), lambda i, be, new, nu: (i, 0)),
        scratch_shapes=[pltpu.VMEM((D_MODEL, 2 * D_FF), BF16), pltpu.VMEM((D_FF, D_MODEL), BF16)],
    )
    return pl.pallas_call(
        _ffn_kernel,
        grid_spec=grid_spec,
        out_shape=jax.ShapeDtypeStruct(xs2d.shape, F32),
        compiler_params=_cparams(("arbitrary",)),
        name="expert_ffn",
    )(blk_e, blk_new, n_used, xs2d, w1, b1, w2, b2)


def _combine_gather(dest_ref, y_hbm, buf, sem, slot):
    def issue(r, carry):
        for k in range(TOP_K):
            row0 = pl.multiple_of(dest_ref[k, r] * ROW_TILE, ROW_TILE)
            dst0 = pl.multiple_of(((slot * TOP_K + k) * TD + r) * ROW_TILE, ROW_TILE)
            pltpu.make_async_copy(
                y_hbm.at[pl.ds(row0, ROW_TILE)], buf.at[pl.ds(dst0, ROW_TILE)],
                sem.at[slot]).start(priority=k % 2)
        return carry

    lax.fori_loop(0, TD, issue, 0, unroll=8)


def _combine_drain(y_hbm, buf, sem, slot):
    def drain(_, carry):
        for _ in range(DRAIN_UNROLL):
            pltpu.make_async_copy(
                y_hbm.at[pl.ds(0, ROW_TILE)], buf.at[pl.ds(0, ROW_TILE)], sem.at[slot]).wait()
        return carry

    lax.fori_loop(0, TD * TOP_K // DRAIN_UNROLL, drain, 0)


def _combine_kernel(dcur_ref, dnext_ref, y_hbm, gates_ref, h1_ref, g_ref, o_ref, buf, tacc, sem):
    i = pl.program_id(0)
    n = pl.num_programs(0)

    @pl.when(i == 0)
    def _():
        _combine_gather(dcur_ref, y_hbm, buf, sem, 0)

    for slot in range(2):
        @pl.when((i % 2 == slot) & (i + 1 < n))
        def _(slot=slot):
            _combine_gather(dnext_ref, y_hbm, buf, sem, 1 - slot)

    for slot in range(2):
        @pl.when(i % 2 == slot)
        def _(slot=slot):
            _combine_drain(y_hbm, buf, sem, slot)

            def token(tok, carry):
                r0 = pl.multiple_of(tok * ROW_TILE, ROW_TILE)
                acc = jnp.zeros((ROW_TILE, LANES), F32)
                for k in range(TOP_K):
                    base = (slot * TOP_K + k) * TD * ROW_TILE
                    acc = acc + buf[pl.ds(base + r0, ROW_TILE), :] * gates_ref[k, pl.ds(tok, 1), :]
                tacc[pl.ds(r0, ROW_TILE), :] = acc
                return carry

            lax.fori_loop(0, TD, token, 0, unroll=8)
            moe = jnp.concatenate(
                [tacc[pl.ds(c, TD, stride=ROW_TILE), :] for c in range(ROW_TILE)], axis=1)
            acc = h1_ref[...] + moe
            ms = jnp.mean(acc * acc, axis=-1, keepdims=True)
            o_ref[...] = acc * lax.rsqrt(ms + NORM_EPS) * g_ref[...]


def _combine(dest, y2d, gates_t, h1, gain):
    t = h1.shape[0]
    n = t // TD
    return pl.pallas_call(
        _combine_kernel,
        grid=(n,),
        in_specs=[
            pl.BlockSpec((TOP_K, TD), lambda i: (0, i), memory_space=pltpu.SMEM),
            pl.BlockSpec((TOP_K, TD), lambda i: (0, jnp.minimum(i + 1, n - 1)), memory_space=pltpu.SMEM),
            pl.BlockSpec(memory_space=pl.ANY),
            pl.BlockSpec((TOP_K, TD, LANES), lambda i: (0, i, 0)),
            pl.BlockSpec((TD, D_MODEL), lambda i: (i, 0)),
            pl.BlockSpec((1, D_MODEL), lambda i: (0, 0)),
        ],
        out_specs=pl.BlockSpec((TD, D_MODEL), lambda i: (i, 0)),
        out_shape=jax.ShapeDtypeStruct((t, D_MODEL), F32),
        scratch_shapes=[
            pltpu.VMEM((2 * TOP_K * TD * ROW_TILE, LANES), F32),
            pltpu.VMEM((TD * ROW_TILE, LANES), F32),
            pltpu.SemaphoreType.DMA((2,)),
        ],
        compiler_params=_cparams(("arbitrary",)),
        name="combine",
    )(dest, dest, y2d, gates_t, h1, gain)


def _block_plan(counts, n_blocks):
    padded = (counts + TM_FFN - 1) // TM_FFN * TM_FFN
    pad_end = jnp.cumsum(padded)
    pad_start = pad_end - padded
    blk_start = jnp.arange(n_blocks, dtype=I32) * TM_FFN
    blk_e = jnp.sum((pad_end[None, :] <= blk_start[:, None]).astype(I32), axis=1)
    blk_e = jnp.minimum(blk_e, N_EXPERTS - 1)
    blk_new = jnp.concatenate([jnp.ones((1,), I32), (blk_e[1:] != blk_e[:-1]).astype(I32)])
    n_used = (pad_end[-1:] // TM_FFN).astype(I32)
    return pad_start.astype(I32), pad_end.astype(I32), blk_e, blk_new, n_used


@jax.jit
def _forward(x, meta_tokens, rel_bias, attn_norm, w_in, w_out, lambda_q1, lambda_k1, lambda_q2,
             lambda_k2, subln_gain, ffn_norm, w_router, b_router, w1, b1, w2, b2, final_norm):
    b, s, _ = x.shape
    t = b * s
    assert TQ == TK and TM_PROJ % TK == 0 and TM_PROJ % TBK == 0 and TB % TBK == 0
    assert s % (TQ * SB_QBLOCKS) == 0 and s % TB == 0 and t % TM_PROJ == 0 and t % TD == 0
    x2d = x.reshape(t, D_MODEL)

    scale = HEAD_DIM ** -0.5
    c_sbk, c_sbv, c_dfq, c_dfk, c_dfv = D_SB, 2 * D_SB, 3 * D_SB, 3 * D_SB + D_DIFF, 3 * D_SB + 2 * D_DIFF
    w_in_b = w_in[0].astype(BF16)
    g_attn = attn_norm[0][None, :]
    colscale = jnp.ones((D_IN,), F32).at[0:D_SB].set(scale).at[c_dfq:c_dfk].set(scale)[None, :]
    w_a = jnp.concatenate([w_in_b[:, :c_sbk], w_in_b[:, c_dfq:c_dfv]], axis=1)
    cs_a = jnp.concatenate([colscale[:, :c_sbk], colscale[:, c_dfq:c_dfv]], axis=1)
    a_proj, k_perm, vt_sb, vt_df = _in_proj_tokens(
        x2d, g_attn, cs_a, w_a, w_in_b[:, c_sbk:c_sbv], w_in_b[:, c_sbv:c_dfq].T, w_in_b[:, c_dfv:].T)
    mproj = _in_proj_meta(meta_tokens, g_attn, colscale, w_in_b)
    mvt_sb = mproj[:, c_sbv:c_dfq].reshape(N_META, H_SB, HEAD_DIM).transpose(1, 2, 0)
    mvt_df = mproj[:, c_dfv:].reshape(N_META, H_DIFF, 2 * HEAD_DIM).transpose(1, 2, 0)

    o_sb = _sb_attention(a_proj, k_perm, vt_sb, mproj, mvt_sb, b, s)

    bias_tiles, meta_bias = _rel_bias_tiles(rel_bias, s)
    o_df = _diff_attention(a_proj, vt_df, mproj, mvt_df, bias_tiles, meta_bias,
                           lambda_q1, lambda_k1, lambda_q2, lambda_k2,
                           subln_gain[0][:, None], b, s)

    tri = jnp.triu(jnp.ones((TM_PROJ, TM_PROJ), BF16), k=1)
    h1, xn2d, idx, gates, rank, cnt = _out_router(
        x2d, o_sb, o_df, w_out[0].astype(BF16), ffn_norm[0][None, :],
        w_router[0].T.astype(BF16), b_router[0][:, None], tri, TM_PROJ)

    a = t * TOP_K
    n_blocks = a // TM_FFN + N_EXPERTS
    a_pad = n_blocks * TM_FFN
    counts = cnt[:, 0].astype(I32)
    pad_start, pad_end, blk_e, blk_new, n_used = _block_plan(counts, n_blocks)
    dest = _route_dest(pad_start, idx, rank)

    xs = _dispatch(pad_end, n_used, dest, xn2d.reshape(t, ROW_TILE, LANES), a_pad)
    y2d = _expert_ffn(blk_e, blk_new, n_used, xs.reshape(a_pad * ROW_TILE, LANES),
                      w1[0], b1[0][:, None, :], w2[0], b2[0][:, None, :], n_blocks)
    gates_rep = jnp.broadcast_to(gates[:, :, None], (TOP_K, t, LANES))
    out = _combine(dest, y2d, gates_rep, h1, final_norm[None, :])
    return out.reshape(b, s, D_MODEL)


def kernel(x, meta_tokens, rel_bias, attn_norm, w_in, w_out, lambda_q1, lambda_k1, lambda_q2,
           lambda_k2, subln_gain, ffn_norm, w_router, b_router, w1, b1, w2, b2, final_norm):
    return _forward(x, meta_tokens, rel_bias, attn_norm, w_in, w_out, lambda_q1, lambda_k1,
                    lambda_q2, lambda_k2, subln_gain, ffn_norm, w_router, b_router, w1, b1, w2, b2,
                    final_norm)
```

```python
import functools
import math

import jax
import jax.numpy as jnp
from jax import lax
from jax.experimental import pallas as pl
from jax.experimental.pallas import tpu as pltpu

D_MODEL = 1024
N_META = 16
CHUNK = 64
HEAD_DIM = 64
H_SB = 8
H_DIFF = 4
D_SB = H_SB * HEAD_DIM
D_DIFF = H_DIFF * 2 * HEAD_DIM
D_IN = 3 * D_SB + 3 * D_DIFF
N_BUCKETS = 32
N_EXPERTS = 32
TOP_K = 4
D_FF = D_MODEL
SWIGLU_ALPHA = 1.702
SWIGLU_LIMIT = 7.0
NORM_EPS = 1e-6
SUBLN_EPS = 1e-5
NEG_BIG = -1e30
LAMBDA_INIT = 0.8 - 0.6 * math.exp(-0.3 * 0)

LANES = 128
SUBLANES = 8
ROW_TILE = D_MODEL // LANES
VMEM_LIMIT = 56 * 1024 * 1024

TM_PROJ = 1024
TQ = 256
TK = 256
KCH = TK // SUBLANES
SB_PAIRS = 4
SB_QBLOCKS = 4
TB = 512
TBK = 512
KB_PER_Q = TB // TBK
DF_HEADS = 4
L_ROWS = 16
TM_FFN = 512
TD = 256
DRAIN_UNROLL = 128
RING = 3
N_BIAS_TILES = KB_PER_Q + 2
SB_EXIT = 104.0

F32 = jnp.float32
BF16 = jnp.bfloat16
I32 = jnp.int32

_NT = (((1,), (1,)), ((), ()))


def _cparams(sem, vmem=VMEM_LIMIT):
    return pltpu.CompilerParams(dimension_semantics=sem, vmem_limit_bytes=vmem)


def _inproj_kernel(x_ref, g_ref, cs_ref, w_ref, o_ref):
    x = x_ref[...]
    ms = jnp.mean(x * x, axis=-1, keepdims=True)
    xn = (x * lax.rsqrt(ms + NORM_EPS) * g_ref[...]).astype(BF16)
    y = jnp.dot(xn, w_ref[...], preferred_element_type=F32)
    o_ref[...] = (y * cs_ref[...]).astype(BF16)


def _in_proj_meta(x2d, gain, colscale, w_bf16):
    t = x2d.shape[0]
    return pl.pallas_call(
        _inproj_kernel,
        grid=(1,),
        in_specs=[
            pl.BlockSpec((t, D_MODEL), lambda i: (0, 0)),
            pl.BlockSpec((1, D_MODEL), lambda i: (0, 0)),
            pl.BlockSpec((1, D_IN), lambda i: (0, 0)),
            pl.BlockSpec((D_MODEL, D_IN), lambda i: (0, 0)),
        ],
        out_specs=pl.BlockSpec((t, D_IN), lambda i: (0, 0)),
        out_shape=jax.ShapeDtypeStruct((t, D_IN), BF16),
        compiler_params=_cparams(("arbitrary",)),
        name="in_proj_meta",
    )(x2d, gain, colscale, w_bf16)


def _rms_bf16(x, g):
    ms = jnp.mean(x * x, axis=-1, keepdims=True)
    return (x * lax.rsqrt(ms + NORM_EPS) * g).astype(BF16)


def _inproj_tokens_kernel(x_ref, g_ref, cs_ref, perm_ref, wa_ref, wk_ref, wvs_ref, wvd_ref,
                          a_ref, kp_ref, vts_ref, vtd_ref):
    xn = _rms_bf16(x_ref[...], g_ref[...])
    xnp = jnp.concatenate(
        [jnp.dot(perm_ref[...], xn[blk * TK:(blk + 1) * TK], preferred_element_type=F32)
         for blk in range(TM_PROJ // TK)], axis=0).astype(BF16)
    a_ref[...] = (jnp.dot(xn, wa_ref[...], preferred_element_type=F32) * cs_ref[...]).astype(BF16)
    kp_ref[...] = jnp.dot(xnp, wk_ref[...], preferred_element_type=F32).astype(BF16)
    vts = lax.dot_general(wvs_ref[...], xnp, _NT, preferred_element_type=F32).astype(BF16)
    for blk in range(TM_PROJ // TK):
        vts_ref[blk] = vts[:, blk * TK:(blk + 1) * TK]
    vtd = lax.dot_general(wvd_ref[...], xn, _NT, preferred_element_type=F32).astype(BF16)
    for blk in range(TM_PROJ // TBK):
        vtd_ref[blk] = vtd[:, blk * TBK:(blk + 1) * TBK]


def _chunk_order_matrix():
    dst = jnp.arange(TK, dtype=I32)
    src = (dst % SUBLANES) * KCH + dst // SUBLANES
    return (src[:, None] == jnp.arange(TK, dtype=I32)[None, :]).astype(BF16)


def _in_proj_tokens(x2d, gain, cs_a, w_a, w_k, w_vs_t, w_vd_t):
    t = x2d.shape[0]
    tm = TM_PROJ
    n_a = w_a.shape[1]
    const = lambda i: (0, 0)
    return pl.pallas_call(
        _inproj_tokens_kernel,
        grid=(t // tm,),
        in_specs=[
            pl.BlockSpec((tm, D_MODEL), lambda i: (i, 0)),
            pl.BlockSpec((1, D_MODEL), const),
            pl.BlockSpec((1, n_a), const),
            pl.BlockSpec((TK, TK), const),
            pl.BlockSpec((D_MODEL, n_a), const),
            pl.BlockSpec((D_MODEL, D_SB), const),
            pl.BlockSpec((D_SB, D_MODEL), const),
            pl.BlockSpec((D_DIFF, D_MODEL), const),
        ],
        out_specs=[
            pl.BlockSpec((tm, n_a), lambda i: (i, 0)),
            pl.BlockSpec((tm, D_SB), lambda i: (i, 0)),
            pl.BlockSpec((tm // TK, D_SB, TK), lambda i: (i, 0, 0)),
            pl.BlockSpec((tm // TBK, D_DIFF, TBK), lambda i: (i, 0, 0)),
        ],
        out_shape=[
            jax.ShapeDtypeStruct((t, n_a), BF16),
            jax.ShapeDtypeStruct((t, D_SB), BF16),
            jax.ShapeDtypeStruct((t // TK, D_SB, TK), BF16),
            jax.ShapeDtypeStruct((t // TBK, D_DIFF, TBK), BF16),
        ],
        compiler_params=_cparams(("parallel",)),
        name="in_proj",
    )(x2d, gain, cs_a, _chunk_order_matrix(), w_a, w_k, w_vs_t, w_vd_t)


def _bias_lookup(rel, rb_ref, h):
    n = jnp.abs(rel)
    n2 = n * n
    large = jnp.full(rel.shape, 8, I32)
    for k in range(1, 8):
        large = large + jnp.where(n2 >= (64 << k), 1, 0)
    bucket = jnp.where(rel > 0, N_BUCKETS // 2, 0) + jnp.where(n < 8, n, large)
    out = jnp.zeros(rel.shape, F32)
    for b in range(N_BUCKETS):
        out = jnp.where(bucket == b, rb_ref[b, h], out)
    return out


def _relbias_kernel(rb_ref, bt_ref, mb_ref):
    h = pl.program_id(0)
    krow = lax.broadcasted_iota(I32, (TBK, TB), 0)
    qcol = lax.broadcasted_iota(I32, (TBK, TB), 1)
    for d in range(N_BIAS_TILES):
        key_off = (KB_PER_Q - 1 - d) * TBK + krow
        visible = (key_off // CHUNK) <= (qcol // CHUNK)
        bt_ref[0, d] = jnp.where(visible, _bias_lookup(key_off - qcol, rb_ref, h), NEG_BIG)
    s = mb_ref.shape[2]
    mrow = lax.broadcasted_iota(I32, (N_META, s), 0)
    qpos = lax.broadcasted_iota(I32, (N_META, s), 1) + N_META
    mb_ref[0] = _bias_lookup(mrow - qpos, rb_ref, h)


def _rel_bias_tiles(rel_bias, s):
    return pl.pallas_call(
        _relbias_kernel,
        grid=(H_DIFF,),
        in_specs=[pl.BlockSpec(memory_space=pltpu.SMEM)],
        out_specs=[
            pl.BlockSpec((1, N_BIAS_TILES, TBK, TB), lambda h: (h, 0, 0, 0)),
            pl.BlockSpec((1, N_META, s), lambda h: (h, 0, 0)),
        ],
        out_shape=[
            jax.ShapeDtypeStruct((H_DIFF, N_BIAS_TILES, TBK, TB), F32),
            jax.ShapeDtypeStruct((H_DIFF, N_META, s), F32),
        ],
        compiler_params=_cparams(("arbitrary",)),
        name="rel_bias",
    )(rel_bias)


def _suffix_incl_sublanes(x):
    r = lax.broadcasted_iota(I32, x.shape, 0)
    for d in (1, 2, 4):
        shifted = pltpu.roll(x, SUBLANES - d, axis=0)
        x = x + jnp.where(r + d < SUBLANES, shifted, 0.0)
    return x


def _softplus(s):
    return jnp.maximum(s, 0.0) + jnp.log(1.0 + jnp.exp(-jnp.abs(s)))


def _head_half(qpair, half):
    lane = lax.broadcasted_iota(I32, qpair.shape, 1)
    keep = (lane >= HEAD_DIM * half) & (lane < HEAD_DIM * (half + 1))
    return jnp.where(keep, qpair, jnp.zeros_like(qpair))


def _sb_block(s, vt, carry, acc, valid):
    sp = _softplus(s)
    if valid is not None:
        sp = jnp.where(valid, sp, 0.0)
    run = jnp.zeros((SUBLANES, s.shape[1]), F32)
    parts = [None] * KCH
    for i in reversed(range(KCH)):
        run = run + sp[SUBLANES * i:SUBLANES * (i + 1), :]
        parts[i] = run
    incl = _suffix_incl_sublanes(run)
    base = (incl - run) + carry
    r_sum = jnp.concatenate([p + base for p in parts], axis=0)
    w = jnp.exp(s - r_sum)
    if valid is not None:
        w = jnp.where(valid, w, 0.0)
    acc = acc + jnp.dot(vt, w.astype(BF16), preferred_element_type=F32)
    return carry + incl[0:1, :], acc


def _sb_meta_block(s, vt, carry, acc):
    sp = _softplus(s)
    lo, hi = sp[0:SUBLANES, :], sp[SUBLANES:2 * SUBLANES, :]
    hi_incl = _suffix_incl_sublanes(hi)
    lo_incl = _suffix_incl_sublanes(lo) + hi_incl[0:1, :]
    r_sum = jnp.concatenate([lo_incl, hi_incl], axis=0) + carry
    w = jnp.exp(s - r_sum)
    return acc + jnp.dot(vt, w.astype(BF16), preferred_element_type=F32)


def _sb_kernel(q_ref, k_ref, vt_ref, mk_ref, mvt_ref, o_ref):
    step = pl.program_id(2)
    row = lax.broadcasted_iota(I32, (TK, TQ), 0)
    lane = lax.broadcasted_iota(I32, (TK, TQ), 1)
    key_off = (row % SUBLANES) * KCH + row // SUBLANES
    causal = key_off < lane

    units = []
    for pp in range(SB_PAIRS):
        for j in range(SB_QBLOCKS):
            qpair = q_ref[TQ * j:TQ * (j + 1), LANES * pp:LANES * (pp + 1)]
            for half in range(2):
                units.append((pp, j, half, step * SB_QBLOCKS + j, _head_half(qpair, half)))

    def scores(u, kb):
        pp, _, _, _, qz = u
        start = pl.multiple_of(kb * TK, TK)
        kblk = k_ref[pl.ds(start, TK), LANES * pp:LANES * (pp + 1)]
        return lax.dot_general(kblk, qz, _NT, preferred_element_type=F32)

    def values(u, kb):
        pp, _, half, _, _ = u
        r0 = 2 * HEAD_DIM * pp + HEAD_DIM * half
        return vt_ref[kb, r0:r0 + HEAD_DIM, :]

    def alive(carry):
        return (jnp.min(carry) < SB_EXIT).astype(I32)

    s_diag = [scores(u, u[3]) for u in units]
    s_prev = [scores(u, jnp.maximum(u[3] - 1, 0)) for u in units]
    carries, accs = [], []
    for u, sd, sp in zip(units, s_diag, s_prev):
        qi = u[3]
        carry = jnp.zeros((1, TQ), F32)
        acc = jnp.zeros((HEAD_DIM, TQ), F32)
        carry, acc = _sb_block(sd, values(u, qi), carry, acc, causal)
        carry2, acc2 = _sb_block(sp, values(u, jnp.maximum(qi - 1, 0)), carry, acc, None)
        carries.append(jnp.where(qi > 0, carry2, carry))
        accs.append(jnp.where(qi > 0, acc2, acc))

    cmin = carries[0]
    for c in carries[1:]:
        cmin = jnp.minimum(cmin, c)

    def slow(accs):
        out = []
        for u, carry, acc in zip(units, carries, accs):
            def cond(st):
                return (st[0] >= 0) & (st[1] > 0)

            def body(st, u=u):
                kb, _, carry, acc = st
                carry, acc = _sb_block(scores(u, kb), values(u, kb), carry, acc, None)
                return kb - 1, alive(carry), carry, acc

            _, live, carry, acc = lax.while_loop(cond, body, (u[3] - 2, alive(carry), carry, acc))

            def meta(acc, u=u, carry=carry):
                pp, _, half, _, qz = u
                sm = lax.dot_general(mk_ref[:, LANES * pp:LANES * (pp + 1)], qz, _NT,
                                     preferred_element_type=F32)
                return _sb_meta_block(sm, mvt_ref[2 * pp + half], carry, acc)

            out.append(lax.cond(live > 0, meta, lambda a: a, acc))
        return tuple(out)

    accs = lax.cond(alive(cmin) > 0, slow, lambda a: a, tuple(accs))
    for idx in range(0, len(units), 2):
        pp, j = units[idx][0], units[idx][1]
        pair_out = jnp.concatenate([accs[idx], accs[idx + 1]], axis=0).T.astype(BF16)
        o_ref[TQ * j:TQ * (j + 1), LANES * pp:LANES * (pp + 1)] = pair_out


def _sb_attention(a_proj, k_perm, vt_perm, mproj, mvt, b, s):
    nkb = s // TK
    nsteps = s // (TQ * SB_QBLOCKS)
    wl = LANES * SB_PAIRS
    mkcol0 = D_SB // wl
    return pl.pallas_call(
        _sb_kernel,
        grid=(b, H_SB // 2 // SB_PAIRS, nsteps),
        in_specs=[
            pl.BlockSpec((TQ * SB_QBLOCKS, wl), lambda bi, p, st: (bi * nsteps + st, p)),
            pl.BlockSpec((s, wl), lambda bi, p, st: (bi, p)),
            pl.BlockSpec((nkb, 2 * HEAD_DIM * SB_PAIRS, TK), lambda bi, p, st: (bi, p, 0)),
            pl.BlockSpec((N_META, wl), lambda bi, p, st: (0, mkcol0 + p)),
            pl.BlockSpec((2 * SB_PAIRS, HEAD_DIM, N_META), lambda bi, p, st: (p, 0, 0)),
        ],
        out_specs=pl.BlockSpec((TQ * SB_QBLOCKS, wl), lambda bi, p, st: (bi * nsteps + st, p)),
        out_shape=jax.ShapeDtypeStruct((b * s, D_SB), BF16),
        compiler_params=_cparams(("parallel", "parallel", "arbitrary")),
        name="sb_attn",
    )(a_proj, k_perm, vt_perm, mproj, mvt)


def _with_ones_rows(vt):
    r = lax.broadcasted_iota(I32, (L_ROWS, vt.shape[1]), 0)
    ones = jnp.where(r == 0, 1.0, 0.0).astype(vt.dtype)
    return jnp.concatenate([vt, ones], axis=0)


def _df_update(s, vt_ext, st):
    m, acc = st
    m_new = jnp.maximum(m, jnp.max(s, axis=0, keepdims=True).astype(F32))
    alpha = jnp.exp(m - m_new)
    p = jnp.exp(s - m_new.astype(BF16))
    acc = alpha * acc + jnp.dot(vt_ext, p, preferred_element_type=F32)
    return m_new, acc


def _df_kernel(q_ref, k_ref, vt_ref, mk_ref, mvt_ref, bt_ref, mb_ref,
               lq1_ref, lk1_ref, lq2_ref, lk2_ref, gain_ref, o_ref):
    qi = pl.program_id(2)
    dv = 2 * HEAD_DIM
    chains = []
    for hh in range(DF_HEADS):
        qpair = q_ref[:, LANES * hh:LANES * (hh + 1)]
        chains += [(hh, _head_half(qpair, 0)), (hh, _head_half(qpair, 1))]

    kb_last = KB_PER_Q * qi + KB_PER_Q - 1

    def scores(d):
        kb = jnp.maximum(kb_last - d, 0)
        rows = pl.ds(pl.multiple_of(kb * TBK, TBK), TBK)
        tile = jnp.minimum(d, N_BIAS_TILES - 1)
        return tuple(
            (lax.dot_general(k_ref[rows, LANES * hh:LANES * (hh + 1)], qz, _NT,
                             preferred_element_type=F32) + bt_ref[hh, tile]).astype(BF16)
            for hh, qz in chains)

    def values(hh, kb):
        return _with_ones_rows(vt_ref[kb, dv * hh:dv * (hh + 1), :])

    def init():
        return (jnp.full((1, TB), -jnp.inf, F32), jnp.zeros((dv + L_ROWS, TB), F32))

    def reduce_block(d, s_cur, st):
        vts = [values(hh, kb_last - d) for hh in range(DF_HEADS)]
        return tuple(_df_update(s_cur[c], vts[hh], st[c]) for c, (hh, _) in enumerate(chains))

    def body(d, carry):
        st, s_cur = carry
        s_nxt = scores(d + 1)
        return reduce_block(d, s_cur, st), s_nxt

    s_first = scores(0)
    st = tuple(
        _df_update((lax.dot_general(mk_ref[:, LANES * hh:LANES * (hh + 1)], qz, _NT,
                                    preferred_element_type=F32) + mb_ref[hh]).astype(BF16),
                   _with_ones_rows(mvt_ref[hh]), init())
        for hh, qz in chains)
    st, s_last = lax.fori_loop(0, kb_last, body, (st, s_first))
    st = reduce_block(kb_last, s_last, st)

    lam = (jnp.exp(jnp.sum(lq1_ref[...] * lk1_ref[...], axis=-1, keepdims=True))
           - jnp.exp(jnp.sum(lq2_ref[...] * lk2_ref[...], axis=-1, keepdims=True))
           + LAMBDA_INIT)
    for hh in range(DF_HEADS):
        acc1, acc2 = st[2 * hh][1], st[2 * hh + 1][1]
        o = acc1[:dv] / acc1[dv:dv + 1] - lam * (acc2[:dv] / acc2[dv:dv + 1])
        ms = jnp.mean(o * o, axis=0, keepdims=True)
        y = o * lax.rsqrt(ms + SUBLN_EPS) * gain_ref[...] * (1.0 - LAMBDA_INIT)
        o_ref[:, LANES * hh:LANES * (hh + 1)] = y.T.astype(BF16)


def _diff_attention(a_proj, vt, mproj, mvt, bias_tiles, meta_bias,
                    lq1, lk1, lq2, lk2, gain_col, b, s):
    nq = s // TB
    nkb = s // TBK
    qcol0 = D_SB // LANES
    kcol0 = (D_SB + D_DIFF) // LANES
    mkcol0 = (3 * D_SB + D_DIFF) // LANES
    lam_spec = pl.BlockSpec((1, HEAD_DIM), lambda bi, h, qi: (0, 0))
    wl = LANES * DF_HEADS
    qcol0, kcol0, mkcol0 = qcol0 // DF_HEADS, kcol0 // DF_HEADS, mkcol0 // DF_HEADS
    return pl.pallas_call(
        _df_kernel,
        grid=(b, H_DIFF // DF_HEADS, nq),
        in_specs=[
            pl.BlockSpec((TB, wl), lambda bi, h, qi: (bi * nq + qi, qcol0 + h)),
            pl.BlockSpec((s, wl), lambda bi, h, qi: (bi, kcol0 + h)),
            pl.BlockSpec((nkb, 2 * HEAD_DIM * DF_HEADS, TBK), lambda bi, h, qi: (bi, h, 0)),
            pl.BlockSpec((N_META, wl), lambda bi, h, qi: (0, mkcol0 + h)),
            pl.BlockSpec((DF_HEADS, 2 * HEAD_DIM, N_META), lambda bi, h, qi: (h, 0, 0)),
            pl.BlockSpec((DF_HEADS, N_BIAS_TILES, TBK, TB), lambda bi, h, qi: (h, 0, 0, 0)),
            pl.BlockSpec((DF_HEADS, N_META, TB), lambda bi, h, qi: (h, 0, qi)),
            lam_spec, lam_spec, lam_spec, lam_spec,
            pl.BlockSpec((2 * HEAD_DIM, 1), lambda bi, h, qi: (0, 0)),
        ],
        out_specs=pl.BlockSpec((TB, wl), lambda bi, h, qi: (bi * nq + qi, h)),
        out_shape=jax.ShapeDtypeStruct((b * s, D_DIFF), BF16),
        compiler_params=_cparams(("parallel", "parallel", "arbitrary")),
        name="diff_attn",
    )(a_proj, a_proj, vt, mproj, mvt, bias_tiles, meta_bias, lq1, lk1, lq2, lk2, gain_col)


def _outrouter_kernel(x_ref, osb_ref, odf_ref, wo_ref, g_ref, wrt_ref, br_ref, tri_ref,
                      h1_ref, xn_ref, idx_ref, gate_ref, rank_ref, cnt_ref, carry_ref):
    @pl.when(pl.program_id(0) == 0)
    def _():
        carry_ref[...] = jnp.zeros_like(carry_ref)

    tm = x_ref.shape[0]
    mix = jnp.concatenate([osb_ref[...], odf_ref[...]], axis=1)
    h1 = x_ref[...] + jnp.dot(mix, wo_ref[...], preferred_element_type=F32)
    h1_ref[...] = h1
    ms = jnp.mean(h1 * h1, axis=-1, keepdims=True)
    xn = h1 * lax.rsqrt(ms + NORM_EPS) * g_ref[...]
    for c in range(ROW_TILE):
        xn_ref[pl.ds(c, tm, stride=ROW_TILE), :] = xn[:, LANES * c:LANES * (c + 1)]

    logits = lax.dot_general(wrt_ref[...], xn.astype(BF16), _NT, preferred_element_type=F32) + br_ref[...]
    e_iota = lax.broadcasted_iota(I32, logits.shape, 0)
    work = logits
    vals, idxs = [], []
    for _ in range(TOP_K):
        m = jnp.max(work, axis=0, keepdims=True)
        ik = jnp.min(jnp.where(work == m, e_iota, N_EXPERTS), axis=0, keepdims=True)
        vals.append(m)
        idxs.append(ik)
        work = jnp.where(e_iota == ik, -jnp.inf, work)
    exps = [jnp.exp(v - vals[0]) for v in vals]
    den = exps[0] + exps[1] + exps[2] + exps[3]
    onehot = jnp.zeros(logits.shape, F32)
    for ik in idxs:
        onehot = onehot + jnp.where(e_iota == ik, 1.0, 0.0)
    prefix = jnp.dot(onehot.astype(BF16), tri_ref[...], preferred_element_type=F32)
    pos = prefix + carry_ref[:, 0:1]
    for k in range(TOP_K):
        idx_ref[k:k + 1, :] = idxs[k]
        gate_ref[k:k + 1, :] = exps[k] / den
        rank_ref[k:k + 1, :] = jnp.sum(jnp.where(e_iota == idxs[k], pos, 0.0), axis=0,
                                       keepdims=True).astype(I32)
    carry_ref[...] = carry_ref[...] + jnp.sum(onehot, axis=1, keepdims=True)
    cnt_ref[...] = carry_ref[...]


def _out_router(x2d, o_sb, o_df, wo_bf16, gain, wr_t, br_col, tri, tm):
    t = x2d.shape[0]
    const = lambda i: (0, 0)
    return pl.pallas_call(
        _outrouter_kernel,
        grid=(t // tm,),
        in_specs=[
            pl.BlockSpec((tm, D_MODEL), lambda i: (i, 0)),
            pl.BlockSpec((tm, D_SB), lambda i: (i, 0)),
            pl.BlockSpec((tm, D_DIFF), lambda i: (i, 0)),
            pl.BlockSpec((D_SB + D_DIFF, D_MODEL), const),
            pl.BlockSpec((1, D_MODEL), const),
            pl.BlockSpec((N_EXPERTS, D_MODEL), const),
            pl.BlockSpec((N_EXPERTS, 1), const),
            pl.BlockSpec((tm, tm), const),
        ],
        out_specs=[
            pl.BlockSpec((tm, D_MODEL), lambda i: (i, 0)),
            pl.BlockSpec((tm * ROW_TILE, LANES), lambda i: (i, 0)),
            pl.BlockSpec((TOP_K, tm), lambda i: (0, i)),
            pl.BlockSpec((TOP_K, tm), lambda i: (0, i)),
            pl.BlockSpec((TOP_K, tm), lambda i: (0, i)),
            pl.BlockSpec((N_EXPERTS, LANES), const),
        ],
        out_shape=[
            jax.ShapeDtypeStruct((t, D_MODEL), F32),
            jax.ShapeDtypeStruct((t * ROW_TILE, LANES), F32),
            jax.ShapeDtypeStruct((TOP_K, t), I32),
            jax.ShapeDtypeStruct((TOP_K, t), F32),
            jax.ShapeDtypeStruct((TOP_K, t), I32),
            jax.ShapeDtypeStruct((N_EXPERTS, LANES), F32),
        ],
        scratch_shapes=[pltpu.VMEM((N_EXPERTS, LANES), F32)],
        compiler_params=_cparams(("arbitrary",)),
        name="out_router",
    )(x2d, o_sb, o_df, wo_bf16, gain, wr_t, br_col, tri)


def _dest_kernel(ps_ref, idx_ref, rank_ref, dest_ref):
    idx = idx_ref[...]
    off = jnp.zeros(idx.shape, I32)
    for e in range(N_EXPERTS):
        off = jnp.where(idx == e, ps_ref[e], off)
    dest_ref[...] = rank_ref[...] + off


def _route_dest(pad_start, idx, rank):
    t = idx.shape[1]
    tt = min(t, 8192)
    grid_spec = pltpu.PrefetchScalarGridSpec(
        num_scalar_prefetch=1,
        grid=(t // tt,),
        in_specs=[pl.BlockSpec((TOP_K, tt), lambda i, ps: (0, i)),
                  pl.BlockSpec((TOP_K, tt), lambda i, ps: (0, i))],
        out_specs=pl.BlockSpec((TOP_K, tt), lambda i, ps: (0, i)),
    )
    return pl.pallas_call(
        _dest_kernel,
        grid_spec=grid_spec,
        out_shape=jax.ShapeDtypeStruct((TOP_K, t), I32),
        compiler_params=_cparams(("parallel",)),
        name="route_dest",
    )(pad_start, idx, rank)


def _zero_fill_padding(pe_ref, nu_ref, xs_hbm, zbuf, zsem, first_tail_block):
    zbuf[...] = jnp.zeros_like(zbuf)
    conds, copies = [], []
    for e in range(N_EXPERTS):
        prev_end = pe_ref[e - 1] if e > 0 else 0
        conds.append(pe_ref[e] > prev_end)
        start = jnp.maximum(pe_ref[e] - TM_FFN, 0)
        copies.append(pltpu.make_async_copy(zbuf, xs_hbm.at[pl.ds(start, TM_FFN)], zsem))
    for j in range(N_EXPERTS):
        blk = first_tail_block + j
        conds.append(blk >= nu_ref[0])
        copies.append(pltpu.make_async_copy(zbuf, xs_hbm.at[pl.ds(blk * TM_FFN, TM_FFN)], zsem))
    for cond, c in zip(conds, copies):
        pl.when(cond)(c.start)
    for cond, c in zip(conds, copies):
        pl.when(cond)(c.wait)


def _dispatch_kernel(pe_ref, nu_ref, dest_ref, xn_hbm, xs_hbm, zbuf, ring, lsem, sem, zsem):
    step = pl.program_id(0)
    n = pl.num_programs(0)
    first_tail_block = xs_hbm.shape[0] // TM_FFN - N_EXPERTS

    def load(tile, slot):
        return pltpu.make_async_copy(xn_hbm.at[pl.ds(tile * TD, TD)], ring.at[slot], lsem.at[slot])

    @pl.when(step == 0)
    def _():
        load(0, 0).start()

        @pl.when(n > 1)
        def _():
            load(1, 1).start()

        _zero_fill_padding(pe_ref, nu_ref, xs_hbm, zbuf, zsem, first_tail_block)

    def drain(slot):
        def body(_, carry):
            for _ in range(DRAIN_UNROLL):
                pltpu.make_async_copy(ring.at[0, 0], xs_hbm.at[0], sem.at[slot]).wait()
            return carry

        lax.fori_loop(0, TD * TOP_K // DRAIN_UNROLL, body, 0)

    for slot in range(RING):
        @pl.when(step % RING == slot)
        def _(slot=slot):
            prev = (slot + RING - 1) % RING
            load(step, slot).wait()

            def issue(r, carry):
                for k in range(TOP_K):
                    pltpu.make_async_copy(
                        ring.at[slot, r], xs_hbm.at[dest_ref[k, r]], sem.at[slot]).start(priority=k % 2)
                return carry

            lax.fori_loop(0, TD, issue, 0, unroll=8)

            @pl.when(step > 0)
            def _():
                drain(prev)

            @pl.when(step + 2 < n)
            def _():
                load(step + 2, prev).start()

            @pl.when(step == n - 1)
            def _():
                drain(slot)


def _dispatch(pad_end, n_used, dest, xn3, a_pad):
    t = dest.shape[1]
    grid_spec = pltpu.PrefetchScalarGridSpec(
        num_scalar_prefetch=2,
        grid=(t // TD,),
        in_specs=[
            pl.BlockSpec((TOP_K, TD), lambda i, pe, nu: (0, i), memory_space=pltpu.SMEM),
            pl.BlockSpec(memory_space=pl.ANY),
        ],
        out_specs=pl.BlockSpec(memory_space=pl.ANY),
        scratch_shapes=[
            pltpu.VMEM((TM_FFN, ROW_TILE, LANES), F32),
            pltpu.VMEM((RING, TD, ROW_TILE, LANES), F32),
            pltpu.SemaphoreType.DMA((RING,)),
            pltpu.SemaphoreType.DMA((RING,)),
            pltpu.SemaphoreType.DMA(()),
        ],
    )
    return pl.pallas_call(
        _dispatch_kernel,
        grid_spec=grid_spec,
        out_shape=jax.ShapeDtypeStruct((a_pad, ROW_TILE, LANES), F32),
        compiler_params=_cparams(("arbitrary",)),
        name="dispatch",
    )(pad_end, n_used, dest, xn3)


def _ffn_kernel(be_ref, new_ref, nu_ref, xs_ref, w1_ref, b1_ref, w2_ref, b2_ref, y_ref, w1b, w2b):
    i = pl.program_id(0)

    @pl.when(i >= nu_ref[0])
    def _():
        y_ref[...] = jnp.zeros_like(y_ref)

    @pl.when((i < nu_ref[0]) & (new_ref[i] > 0))
    def _():
        w1b[...] = w1_ref[...].astype(BF16)
        w2b[...] = w2_ref[...].astype(BF16)

    @pl.when(i < nu_ref[0])
    def _():
        x = jnp.concatenate(
            [xs_ref[pl.ds(c, TM_FFN, stride=ROW_TILE), :] for c in range(ROW_TILE)], axis=1).astype(BF16)
        hu = jnp.dot(x, w1b[...], preferred_element_type=F32) + b1_ref[...]
        gate = jnp.minimum(hu[:, :D_FF], SWIGLU_LIMIT)
        lin = jnp.clip(hu[:, D_FF:], -SWIGLU_LIMIT, SWIGLU_LIMIT)
        act = gate * jax.nn.sigmoid(SWIGLU_ALPHA * gate) * (lin + 1.0)
        y = jnp.dot(act.astype(BF16), w2b[...], preferred_element_type=F32) + b2_ref[...]
        for c in range(ROW_TILE):
            y_ref[pl.ds(c, TM_FFN, stride=ROW_TILE), :] = y[:, LANES * c:LANES * (c + 1)]


def _expert_ffn(blk_e, blk_new, n_used, xs2d, w1, b1, w2, b2, n_blocks):
    rows = TM_FFN * ROW_TILE

    def xmap(i, be, new, nu):
        return (jnp.minimum(i, nu[0] - 1), 0)

    def wmap(i, be, new, nu):
        return (be[i], 0, 0)

    grid_spec = pltpu.PrefetchScalarGridSpec(
        num_scalar_prefetch=3,
        grid=(n_blocks,),
        in_specs=[
            pl.BlockSpec((rows, LANES), xmap),
            pl.BlockSpec((None, D_MODEL, 2 * D_FF), wmap),
            pl.BlockSpec((None, 1, 2 * D_FF), wmap),
            pl.BlockSpec((None, D_FF, D_MODEL), wmap),
            pl.BlockSpec((None, 1, D_MODEL), wmap),
        ],
        out_specs=pl.BlockSpec((rows, LANES), lambda i, be, new, nu: (i, 0)),
        scratch_shapes=[pltpu.VMEM((D_MODEL, 2 * D_FF), BF16), pltpu.VMEM((D_FF, D_MODEL), BF16)],
    )
    return pl.pallas_call(
        _ffn_kernel,
        grid_spec=grid_spec,
        out_shape=jax.ShapeDtypeStruct(xs2d.shape, F32),
        compiler_params=_cparams(("arbitrary",)),
        name="expert_ffn",
    )(blk_e, blk_new, n_used, xs2d, w1, b1, w2, b2)


def _combine_gather(dest_ref, y_hbm, buf, sem, slot):
    def issue(r, carry):
        for k in range(TOP_K):
            row0 = pl.multiple_of(dest_ref[k, r] * ROW_TILE, ROW_TILE)
            dst0 = pl.multiple_of(((slot * TOP_K + k) * TD + r) * ROW_TILE, ROW_TILE)
            pltpu.make_async_copy(
                y_hbm.at[pl.ds(row0, ROW_TILE)], buf.at[pl.ds(dst0, ROW_TILE)],
                sem.at[slot]).start(priority=k % 2)
        return carry

    lax.fori_loop(0, TD, issue, 0, unroll=8)


def _combine_drain(y_hbm, buf, sem, slot):
    def drain(_, carry):
        for _ in range(DRAIN_UNROLL):
            pltpu.make_async_copy(
                y_hbm.at[pl.ds(0, ROW_TILE)], buf.at[pl.ds(0, ROW_TILE)], sem.at[slot]).wait()
        return carry

    lax.fori_loop(0, TD * TOP_K // DRAIN_UNROLL, drain, 0)


def _combine_kernel(dcur_ref, dnext_ref, y_hbm, gates_ref, h1_ref, g_ref, o_ref, buf, tacc, sem):
    i = pl.program_id(0)
    n = pl.num_programs(0)

    @pl.when(i == 0)
    def _():
        _combine_gather(dcur_ref, y_hbm, buf, sem, 0)

    for slot in range(2):
        @pl.when((i % 2 == slot) & (i + 1 < n))
        def _(slot=slot):
            _combine_gather(dnext_ref, y_hbm, buf, sem, 1 - slot)

    for slot in range(2):
        @pl.when(i % 2 == slot)
        def _(slot=slot):
            _combine_drain(y_hbm, buf, sem, slot)

            def token(tok, carry):
                r0 = pl.multiple_of(tok * ROW_TILE, ROW_TILE)
                acc = jnp.zeros((ROW_TILE, LANES), F32)
                for k in range(TOP_K):
                    base = (slot * TOP_K + k) * TD * ROW_TILE
                    acc = acc + buf[pl.ds(base + r0, ROW_TILE), :] * gates_ref[k, pl.ds(tok, 1), :]
                tacc[pl.ds(r0, ROW_TILE), :] = acc
                return carry

            lax.fori_loop(0, TD, token, 0, unroll=8)
            moe = jnp.concatenate(
                [tacc[pl.ds(c, TD, stride=ROW_TILE), :] for c in range(ROW_TILE)], axis=1)
            acc = h1_ref[...] + moe
            ms = jnp.mean(acc * acc, axis=-1, keepdims=True)
            o_ref[...] = acc * lax.rsqrt(ms + NORM_EPS) * g_ref[...]


def _combine(dest, y2d, gates_t, h1, gain):
    t = h1.shape[0]
    n = t // TD
    return pl.pallas_call(
        _combine_kernel,
        grid=(n,),
        in_specs=[
            pl.BlockSpec((TOP_K, TD), lambda i: (0, i), memory_space=pltpu.SMEM),
            pl.BlockSpec((TOP_K, TD), lambda i: (0, jnp.minimum(i + 1, n - 1)), memory_space=pltpu.SMEM),
            pl.BlockSpec(memory_space=pl.ANY),
            pl.BlockSpec((TOP_K, TD, LANES), lambda i: (0, i, 0)),
            pl.BlockSpec((TD, D_MODEL), lambda i: (i, 0)),
            pl.BlockSpec((1, D_MODEL), lambda i: (0, 0)),
        ],
        out_specs=pl.BlockSpec((TD, D_MODEL), lambda i: (i, 0)),
        out_shape=jax.ShapeDtypeStruct((t, D_MODEL), F32),
        scratch_shapes=[
            pltpu.VMEM((2 * TOP_K * TD * ROW_TILE, LANES), F32),
            pltpu.VMEM((TD * ROW_TILE, LANES), F32),
            pltpu.SemaphoreType.DMA((2,)),
        ],
        compiler_params=_cparams(("arbitrary",)),
        name="combine",
    )(dest, dest, y2d, gates_t, h1, gain)


def _block_plan(counts, n_blocks):
    padded = (counts + TM_FFN - 1) // TM_FFN * TM_FFN
    pad_end = jnp.cumsum(padded)
    pad_start = pad_end - padded
    blk_start = jnp.arange(n_blocks, dtype=I32) * TM_FFN
    blk_e = jnp.sum((pad_end[None, :] <= blk_start[:, None]).astype(I32), axis=1)
    blk_e = jnp.minimum(blk_e, N_EXPERTS - 1)
    blk_new = jnp.concatenate([jnp.ones((1,), I32), (blk_e[1:] != blk_e[:-1]).astype(I32)])
    n_used = (pad_end[-1:] // TM_FFN).astype(I32)
    return pad_start.astype(I32), pad_end.astype(I32), blk_e, blk_new, n_used


@jax.jit
def _forward(x, meta_tokens, rel_bias, attn_norm, w_in, w_out, lambda_q1, lambda_k1, lambda_q2,
             lambda_k2, subln_gain, ffn_norm, w_router, b_router, w1, b1, w2, b2, final_norm):
    b, s, _ = x.shape
    t = b * s
    assert TQ == TK and TM_PROJ % TK == 0 and TM_PROJ % TBK == 0 and TB % TBK == 0
    assert s % (TQ * SB_QBLOCKS) == 0 and s % TB == 0 and t % TM_PROJ == 0 and t % TD == 0
    x2d = x.reshape(t, D_MODEL)

    scale = HEAD_DIM ** -0.5
    c_sbk, c_sbv, c_dfq, c_dfk, c_dfv = D_SB, 2 * D_SB, 3 * D_SB, 3 * D_SB + D_DIFF, 3 * D_SB + 2 * D_DIFF
    w_in_b = w_in[0].astype(BF16)
    g_attn = attn_norm[0][None, :]
    colscale = jnp.ones((D_IN,), F32).at[0:D_SB].set(scale).at[c_dfq:c_dfk].set(scale)[None, :]
    w_a = jnp.concatenate([w_in_b[:, :c_sbk], w_in_b[:, c_dfq:c_dfv]], axis=1)
    cs_a = jnp.concatenate([colscale[:, :c_sbk], colscale[:, c_dfq:c_dfv]], axis=1)
    a_proj, k_perm, vt_sb, vt_df = _in_proj_tokens(
        x2d, g_attn, cs_a, w_a, w_in_b[:, c_sbk:c_sbv], w_in_b[:, c_sbv:c_dfq].T, w_in_b[:, c_dfv:].T)
    mproj = _in_proj_meta(meta_tokens, g_attn, colscale, w_in_b)
    mvt_sb = mproj[:, c_sbv:c_dfq].reshape(N_META, H_SB, HEAD_DIM).transpose(1, 2, 0)
    mvt_df = mproj[:, c_dfv:].reshape(N_META, H_DIFF, 2 * HEAD_DIM).transpose(1, 2, 0)

    o_sb = _sb_attention(a_proj, k_perm, vt_sb, mproj, mvt_sb, b, s)

    bias_tiles, meta_bias = _rel_bias_tiles(rel_bias, s)
    o_df = _diff_attention(a_proj, vt_df, mproj, mvt_df, bias_tiles, meta_bias,
                           lambda_q1, lambda_k1, lambda_q2, lambda_k2,
                           subln_gain[0][:, None], b, s)

    tri = jnp.triu(jnp.ones((TM_PROJ, TM_PROJ), BF16), k=1)
    h1, xn2d, idx, gates, rank, cnt = _out_router(
        x2d, o_sb, o_df, w_out[0].astype(BF16), ffn_norm[0][None, :],
        w_router[0].T.astype(BF16), b_router[0][:, None], tri, TM_PROJ)

    a = t * TOP_K
    n_blocks = a // TM_FFN + N_EXPERTS
    a_pad = n_blocks * TM_FFN
    counts = cnt[:, 0].astype(I32)
    pad_start, pad_end, blk_e, blk_new, n_used = _block_plan(counts, n_blocks)
    dest = _route_dest(pad_start, idx, rank)

    xs = _dispatch(pad_end, n_used, dest, xn2d.reshape(t, ROW_TILE, LANES), a_pad)
    y2d = _expert_ffn(blk_e, blk_new, n_used, xs.reshape(a_pad * ROW_TILE, LANES),
                      w1[0], b1[0][:, None, :], w2[0], b2[0][:, None, :], n_blocks)
    gates_rep = jnp.broadcast_to(gates[:, :, None], (TOP_K, t, LANES))
    out = _combine(dest, y2d, gates_rep, h1, final_norm[None, :])
    return out.reshape(b, s, D_MODEL)


def kernel(x, meta_tokens, rel_bias, attn_norm, w_in, w_out, lambda_q1, lambda_k1, lambda_q2,
           lambda_k2, subln_gain, ffn_norm, w_router, b_router, w1, b1, w2, b2, final_norm):
    return _forward(x, meta_tokens, rel_bias, attn_norm, w_in, w_out, lambda_q1, lambda_k1,
                    lambda_q2, lambda_k2, subln_gain, ffn_norm, w_router, b_router, w1, b1, w2, b2,
                    final_norm)
```
